```python
import jax, jax.numpy as jnp
from jax import lax
import numpy as np

D_MODEL = 1024
BATCH = 16
SEQ = 256
DEPTH = 1
DEC_BATCH = 2
DEC_SEQ = 4096
PAST_LEN = 512

GRID_W = 64
HEAD_DIM = 64
D_A = D_MODEL
N_HEADS_A = D_A // HEAD_DIM
D_B = D_MODEL // 2
POOL_WINDOWS = (2, 4, 8, 16)
POOL_GROUPS = 4
POOL_GROUP_DIM = D_B // POOL_GROUPS
LORA_W = 64
LORA_A = 64
LORA_G = 128
D_FF = 2816
N_MOD = 9
SHIFT_W = 3 * D_A + 2 * LORA_W + 2 * LORA_A + LORA_G
MIX_IN = SHIFT_W + D_B + 2 * D_MODEL
ALPHA = (2 * DEPTH) ** 0.25
BETA = (8 * DEPTH) ** -0.25
LN_EPS = 1e-5
GN_EPS = 64e-5

kernel_name = 'hybrid_rwkv7_pool_diffusion_step'


def _layer_norm(x, g, b):
    xf = x.astype(jnp.float32)
    mu = jnp.mean(xf, -1, keepdims=True)
    var = jnp.mean(jnp.square(xf - mu), -1, keepdims=True)
    return ((xf - mu) * lax.rsqrt(var + LN_EPS) * g + b).astype(x.dtype)


def _swiglu(h, w_in, w_out):
    gate, up = jnp.split(h @ w_in, 2, axis=-1)
    return (jax.nn.silu(gate) * up) @ w_out


def _modulation(cvec, w_mod, b_mod):
    m = jax.nn.silu(cvec) @ w_mod + b_mod
    return m.reshape(cvec.shape[0], 1, N_MOD, D_MODEL)


def _shift_seq(p):
    b, t, c = p.shape
    q = p.reshape(b, t, c // 2, 2)
    prev = jnp.pad(q[:, :-1, :, 0], ((0, 0), (1, 0), (0, 0)))
    nxt = jnp.pad(q[:, 1:, :, 1], ((0, 0), (0, 1), (0, 0)))
    return jnp.stack([prev, nxt], axis=-1).reshape(b, t, c)


def _shift_grid(p):
    b, t, c = p.shape
    rows = t // GRID_W
    q = p.reshape(b, rows, GRID_W, c // 4, 4)
    left = jnp.pad(q[:, :, :-1, :, 0], ((0, 0), (0, 0), (1, 0), (0, 0)))
    right = jnp.pad(q[:, :, 1:, :, 1], ((0, 0), (0, 0), (0, 1), (0, 0)))
    up = jnp.pad(q[:, :-1, :, :, 2], ((0, 0), (1, 0), (0, 0), (0, 0)))
    down = jnp.pad(q[:, 1:, :, :, 3], ((0, 0), (0, 1), (0, 0), (0, 0)))
    return jnp.stack([left, right, up, down], axis=-1).reshape(b, t, c)


def _centred_pool_residual(p):
    b, t, c = p.shape
    pf = p.astype(jnp.float32)
    pos = jnp.arange(t)
    outs = []
    for gi, w in enumerate(POOL_WINDOWS):
        pg = pf[..., gi * POOL_GROUP_DIM:(gi + 1) * POOL_GROUP_DIM]
        cs = jnp.concatenate([jnp.zeros((b, 1, POOL_GROUP_DIM), jnp.float32), jnp.cumsum(pg, axis=1)], axis=1)
        lo = jnp.clip(pos - w // 2, 0, t)
        hi = jnp.clip(pos + w - w // 2, 0, t)
        mean = (cs[:, hi] - cs[:, lo]) / (hi - lo).astype(jnp.float32)[:, None]
        outs.append(mean - pg)
    return jnp.concatenate(outs, axis=-1).astype(p.dtype)


def _heads(z):
    return z.reshape(z.shape[:-1] + (N_HEADS_A, HEAD_DIM))


def _to_scan(z):
    z = jnp.concatenate([z[:1], jnp.flip(z[1:], axis=2)], axis=0)
    return jnp.moveaxis(z, 2, 0)


def _from_scan(y):
    y = jnp.moveaxis(y, 0, 2)
    return y[0] + jnp.flip(y[1], axis=1)


def _rwkv7_step(S, inp):
    r, w, k, v, kk, a = inp
    sa = jnp.einsum('dbhvk,dbhk->dbhv', S, kk)
    S = S * w[..., None, :] - sa[..., :, None] * (kk * a)[..., None, :] + v[..., :, None] * k[..., None, :]
    return S, jnp.einsum('dbhvk,dbhk->dbhv', S, r)


def _token_mixer(h, s0, latent, p):
    f32 = jnp.float32
    proj = h @ p['w_mix_in']
    ps = proj[..., :SHIFT_W]
    pool_in = proj[..., SHIFT_W:SHIFT_W + D_B]
    gate_logits = proj[..., SHIFT_W + D_B:]
    shifted = _shift_grid(ps) if latent else _shift_seq(ps)
    ps = (ps + p['mu_shift'] * (shifted - ps)).astype(f32)
    b, t, _ = ps.shape
    o1, o2, o3 = D_A, 2 * D_A, 3 * D_A
    o4 = o3 + 2 * LORA_W
    o5 = o4 + 2 * LORA_A
    r, k, v = ps[..., :o1], ps[..., o1:o2], ps[..., o2:o3]
    w_down = ps[..., o3:o4].reshape(b, t, 2, LORA_W)
    a_down = ps[..., o4:o5].reshape(b, t, 2, LORA_A)
    g_down = ps[..., o5:]
    w_raw = p['w0'][:, None, None, :] + jnp.einsum('btdr,drc->dbtc', jnp.tanh(w_down), p['w_up'])
    decay = jnp.exp(-jnp.exp(-jax.nn.softplus(-w_raw) - 0.5))
    a = jax.nn.sigmoid(p['a0'][:, None, None, :] + jnp.einsum('btdr,drc->dbtc', a_down, p['a_up']))
    g = jax.nn.sigmoid(g_down) @ p['g_up']
    kk = _heads(k * p['k_k'])
    kk = kk / jnp.maximum(jnp.sqrt(jnp.sum(kk * kk, -1, keepdims=True)), 1e-12)
    k_dir = k[None] * (1.0 + (a - 1.0) * p['k_a'])
    r_h, k_h, v_h = _heads(r), _heads(k), _heads(v)
    both = lambda z: jnp.broadcast_to(z[None], (2,) + z.shape)
    xs = (_to_scan(both(r_h)), _to_scan(_heads(decay)), _to_scan(_heads(k_dir)),
          _to_scan(both(v_h)), _to_scan(both(kk)), _to_scan(_heads(a)))
    s_final, y = lax.scan(_rwkv7_step, s0.astype(f32), xs)
    y = _from_scan(y)
    mu = jnp.mean(y, -1, keepdims=True)
    var = jnp.mean(jnp.square(y - mu), -1, keepdims=True)
    y = ((y - mu) * lax.rsqrt(var + GN_EPS)).reshape(b, t, D_A) * p['lnx_g'] + p['lnx_b']
    bonus = jnp.sum(r_h * k_h * p['r_k'], -1, keepdims=True) * v_h
    y = (y + bonus.reshape(b, t, D_A)) * g
    y_a = y.astype(h.dtype) @ p['w_o_rwkv']
    u = _centred_pool_residual(pool_in).reshape(b, t, POOL_GROUPS, POOL_GROUP_DIM)
    u = jnp.einsum('btgc,gcd->btgd', u, p['w_pool']).reshape(b, t, D_B) * p['pool_scale']
    y_b = u @ p['w_o_pool']
    gate_a = jax.nn.sigmoid(gate_logits[..., :D_MODEL])
    gate_b = jax.nn.sigmoid(gate_logits[..., D_MODEL:])
    return (gate_a * y_a + gate_b * y_b) @ p['w_out'], s_final


def _trunk_layer(x, mod, s0, latent, p):
    m = [mod[:, :, i, :] for i in range(N_MOD)]
    h = x * (1.0 + m[1]) + m[0]
    x = _layer_norm(ALPHA * x + 0.5 * m[2] * _swiglu(h, p['ffn_in'][0], p['ffn_out'][0]), p['ln_g'][0], p['ln_b'][0])
    h = x * (1.0 + m[4]) + m[3]
    mix, s_final = _token_mixer(h, s0, latent, p)
    x = _layer_norm(ALPHA * x + m[5] * mix, p['ln_g'][1], p['ln_b'][1])
    h = x * (1.0 + m[7]) + m[6]
    x = _layer_norm(ALPHA * x + 0.5 * m[8] * _swiglu(h, p['ffn_in'][1], p['ffn_out'][1]), p['ln_g'][2], p['ln_b'][2])
    return x, s_final


def setup_inputs(seed: int = 0) -> dict:
    key = jax.random.key(seed)
    ks = jax.random.split(key, 32)
    f32 = jnp.float32
    nrm = lambda k, shape, scale: jax.random.normal(k, shape, f32) * scale
    return {
        'x_prompt': nrm(ks[0], (BATCH, SEQ, D_MODEL), 1.0),
        'x_sample': nrm(ks[1], (DEC_BATCH, DEC_SEQ, D_MODEL), 1.0),
        'c': nrm(ks[2], (DEC_BATCH, D_MODEL), 1.0),
        'state_rwkv': nrm(ks[3], (DEC_BATCH, DEPTH, 2, N_HEADS_A, HEAD_DIM, HEAD_DIM), 0.5),
        'c_ctx': nrm(ks[4], (D_MODEL,), 1.0),
        'w_mod': nrm(ks[5], (DEPTH, D_MODEL, N_MOD * D_MODEL), D_MODEL ** -0.5),
        'b_mod': nrm(ks[6], (DEPTH, N_MOD * D_MODEL), 0.01),
        'ln_g': 1.0 + nrm(ks[7], (DEPTH, 3, D_MODEL), 0.05),
        'ln_b': nrm(ks[8], (DEPTH, 3, D_MODEL), 0.02),
        'ffn_in': nrm(ks[9], (DEPTH, 2, D_MODEL, 2 * D_FF), D_MODEL ** -0.5),
        'ffn_out': nrm(ks[10], (DEPTH, 2, D_FF, D_MODEL), BETA * D_FF ** -0.5),
        'w_mix_in': nrm(ks[11], (DEPTH, D_MODEL, MIX_IN), D_MODEL ** -0.5),
        'mu_shift': jax.random.uniform(ks[12], (DEPTH, SHIFT_W), f32),
        'w0': nrm(ks[13], (DEPTH, 2, D_A), 0.5),
        'w_up': nrm(ks[14], (DEPTH, 2, LORA_W, D_A), LORA_W ** -0.5),
        'a0': nrm(ks[15], (DEPTH, 2, D_A), 0.5),
        'a_up': nrm(ks[16], (DEPTH, 2, LORA_A, D_A), LORA_A ** -0.5),
        'g_up': nrm(ks[17], (DEPTH, LORA_G, D_A), LORA_G ** -0.5),
        'k_k': 0.85 + nrm(ks[18], (DEPTH, D_A), 0.05),
        'k_a': 1.0 + nrm(ks[19], (DEPTH, D_A), 0.05),
        'r_k': nrm(ks[20], (DEPTH, N_HEADS_A, HEAD_DIM), 0.1),
        'lnx_g': 1.0 + nrm(ks[21], (DEPTH, D_A), 0.05),
        'lnx_b': nrm(ks[22], (DEPTH, D_A), 0.02),
        'w_o_rwkv': nrm(ks[23], (DEPTH, D_A, D_MODEL), BETA * D_A ** -0.5),
        'w_pool': nrm(ks[24], (DEPTH, POOL_GROUPS, POOL_GROUP_DIM, POOL_GROUP_DIM), POOL_GROUP_DIM ** -0.5),
        'pool_scale': 1.0 + nrm(ks[25], (DEPTH, D_B), 0.05),
        'w_o_pool': nrm(ks[26], (DEPTH, D_B, D_MODEL), BETA * D_B ** -0.5),
        'w_out': nrm(ks[27], (DEPTH, D_MODEL, D_MODEL), BETA * D_MODEL ** -0.5),
    }


def reference(x_prompt, x_sample, c, state_rwkv, c_ctx, w_mod, b_mod, ln_g, ln_b, ffn_in, ffn_out,
              w_mix_in, mu_shift, w0, w_up, a0, a_up, g_up, k_k, k_a, r_k, lnx_g, lnx_b,
              w_o_rwkv, w_pool, pool_scale, w_o_pool, w_out):
    y_p = x_prompt
    y_s = x_sample
    ctx_states = []
    for l in range(DEPTH):
        p = {
            'ln_g': ln_g[l], 'ln_b': ln_b[l], 'ffn_in': ffn_in[l], 'ffn_out': ffn_out[l],
            'w_mix_in': w_mix_in[l], 'mu_shift': mu_shift[l], 'w0': w0[l], 'w_up': w_up[l],
            'a0': a0[l], 'a_up': a_up[l], 'g_up': g_up[l], 'k_k': k_k[l], 'k_a': k_a[l],
            'r_k': r_k[l], 'lnx_g': lnx_g[l], 'lnx_b': lnx_b[l], 'w_o_rwkv': w_o_rwkv[l],
            'w_pool': w_pool[l], 'pool_scale': pool_scale[l], 'w_o_pool': w_o_pool[l], 'w_out': w_out[l],
        }
        mod_ctx = _modulation(c_ctx[None, :], w_mod[l], b_mod[l])
        s0_ctx = jnp.zeros((2, y_p.shape[0], N_HEADS_A, HEAD_DIM, HEAD_DIM), jnp.float32)
        y_p, s_ctx = _trunk_layer(y_p, mod_ctx, s0_ctx, False, p)
        ctx_states.append(jnp.moveaxis(s_ctx, 0, 1))
        mod_lat = _modulation(c, w_mod[l], b_mod[l])
        s0_lat = jnp.moveaxis(state_rwkv[:, l], 1, 0)
        y_s, _ = _trunk_layer(y_s, mod_lat, s0_lat, True, p)
    new_state_rwkv = jnp.stack(ctx_states, axis=1).astype(x_prompt.dtype)
    return (y_p, y_s, new_state_rwkv)
```

```python
import functools
import math

import jax
import jax.numpy as jnp
from jax import lax
from jax.experimental import pallas as pl
from jax.experimental.pallas import tpu as pltpu

F32 = jnp.float32
BF16 = jnp.bfloat16

D_MODEL = 1024
DEPTH = 1
GRID_W = 64
HEAD_DIM = 64
D_A = D_MODEL
N_HEADS = D_A // HEAD_DIM
D_B = D_MODEL // 2
POOL_WINDOWS = (2, 4, 8, 16)
POOL_GROUP_DIM = D_B // len(POOL_WINDOWS)
LORA_W = 64
LORA_A = 64
LORA_G = 128
D_FF = 2816
N_MOD = 9
SHIFT_W = 3 * D_A + 2 * LORA_W + 2 * LORA_A + LORA_G
ALPHA = (2 * DEPTH) ** 0.25
LN_EPS = 1e-5
GN_EPS = 64e-5
DECAY_SCALE = math.exp(-0.5)

LANES = 128
HALO = 64
POOL_HALO = 8
CHUNK = 64
PAIR = 2 * HEAD_DIM
VMEM_LIMIT = 56 * 1024 * 1024


def _cparams(sem):
    return pltpu.CompilerParams(dimension_semantics=sem, vmem_limit_bytes=VMEM_LIMIT)


def _mod_p2(x, n):
    assert n & (n - 1) == 0
    return x & (n - 1)


def _div_p2(x, n):
    assert n & (n - 1) == 0
    return x >> (n.bit_length() - 1)


def _sigmoid(x):
    return 1.0 / (1.0 + jnp.exp(-x))


def _silu(x):
    return x / (1.0 + jnp.exp(-x))


def _dot(a, b):
    return jnp.dot(a.astype(BF16), b.astype(BF16), preferred_element_type=F32)


def _split2(x):
    hi = x.astype(BF16)
    lo = (x - hi.astype(F32)).astype(BF16)
    return hi, lo


def _split3(x):
    hi = x.astype(BF16)
    r1 = x - hi.astype(F32)
    mid = r1.astype(BF16)
    lo = (r1 - mid.astype(F32)).astype(BF16)
    return hi, mid, lo


_NN = (((1,), (0,)), ((), ()))
_NT = (((1,), (1,)), ((), ()))
_TN = (((0,), (0,)), ((), ()))


def _dg(a, b, dims):
    return lax.dot_general(a, b, dims, preferred_element_type=F32)


def _mm3(a, b, dims=_NN):
    ah, al = _split2(a)
    bh, bl = _split2(b)
    return _dg(ah, bh, dims) + (_dg(ah, bl, dims) + _dg(al, bh, dims))


def _layer_norm(z, g, b):
    mu = jnp.mean(z, axis=-1, keepdims=True)
    d = z - mu
    var = jnp.mean(d * d, axis=-1, keepdims=True)
    return d * lax.rsqrt(var + LN_EPS) * g + b


def _head_sum(x, ones_bd):
    hi, lo = _split2(x)
    outs = []
    for c in range(x.shape[1] // LANES):
        sl = slice(c * LANES, (c + 1) * LANES)
        outs.append(jnp.dot(hi[:, sl], ones_bd, preferred_element_type=F32)
                    + jnp.dot(lo[:, sl], ones_bd, preferred_element_type=F32))
    return jnp.concatenate(outs, axis=1)


def _mod_kernel(c_ref, w_ref, b_ref, o_ref):
    o_ref[...] = _dot(_silu(c_ref[...]), w_ref[...]) + b_ref[...]


def _modulation(cvecs, w_mod, b_mod):
    rows = cvecs.shape[0]
    n = w_mod.shape[1]
    tn = 1024
    return pl.pallas_call(
        _mod_kernel,
        grid=(n // tn,),
        in_specs=[pl.BlockSpec((rows, D_MODEL), lambda j: (0, 0)),
                  pl.BlockSpec((D_MODEL, tn), lambda j: (0, j)),
                  pl.BlockSpec((1, tn), lambda j: (0, j))],
        out_specs=pl.BlockSpec((rows, tn), lambda j: (0, j)),
        out_shape=jax.ShapeDtypeStruct((rows, n), F32),
        compiler_params=_cparams(("arbitrary",)),
        name="modulation",
    )(cvecs, w_mod, b_mod.reshape(1, n))


def _ffn_kernel(x_ref, mod_ref, wg_ref, wu_ref, wo_ref, lng_ref, lnb_ref, o_ref, h_scr, acc_scr, *, mi):
    j = pl.program_id(1)

    @pl.when(j == 0)
    def _():
        shift = mod_ref[mi:mi + 1, :]
        scale = mod_ref[mi + 1:mi + 2, :]
        h_scr[...] = (x_ref[...] * (1.0 + scale) + shift).astype(BF16)
        acc_scr[...] = jnp.zeros_like(acc_scr)

    h = h_scr[...]
    gate = jnp.dot(h, wg_ref[...], preferred_element_type=F32)
    up = jnp.dot(h, wu_ref[...], preferred_element_type=F32)
    acc_scr[...] += _dot(_silu(gate) * up, wo_ref[...])

    @pl.when(j == pl.num_programs(1) - 1)
    def _():
        g = mod_ref[mi + 2:mi + 3, :]
        z = ALPHA * x_ref[...] + 0.5 * g * acc_scr[...]
        o_ref[...] = _layer_norm(z, lng_ref[...], lnb_ref[...])


def _ffn(x, mod, mod_of_tile, w_in, w_out, ln_g, ln_b, mi, tm):
    rows = x.shape[0]
    tf = 256
    nf = D_FF // tf
    return pl.pallas_call(
        functools.partial(_ffn_kernel, mi=mi),
        grid=(rows // tm, nf),
        in_specs=[pl.BlockSpec((tm, D_MODEL), lambda i, j: (i, 0)),
                  pl.BlockSpec((None, N_MOD, D_MODEL), lambda i, j: (mod_of_tile(i, tm), 0, 0)),
                  pl.BlockSpec((D_MODEL, tf), lambda i, j: (0, j)),
                  pl.BlockSpec((D_MODEL, tf), lambda i, j: (0, j + nf)),
                  pl.BlockSpec((tf, D_MODEL), lambda i, j: (j, 0)),
                  pl.BlockSpec((1, D_MODEL), lambda i, j: (0, 0)),
                  pl.BlockSpec((1, D_MODEL), lambda i, j: (0, 0))],
        out_specs=pl.BlockSpec((tm, D_MODEL), lambda i, j: (i, 0)),
        out_shape=jax.ShapeDtypeStruct((rows, D_MODEL), F32),
        scratch_shapes=[pltpu.VMEM((tm, D_MODEL), BF16), pltpu.VMEM((tm, D_MODEL), F32)],
        compiler_params=_cparams(("parallel", "arbitrary")),
        name="ffn",
    )(x, mod, w_in, w_in, w_out, ln_g.reshape(1, -1), ln_b.reshape(1, -1))


def _proj_kernel(x_ref, mod_ref, w_ref, o_ref, *, mi):
    shift = mod_ref[mi:mi + 1, :]
    scale = mod_ref[mi + 1:mi + 2, :]
    h = (x_ref[...] * (1.0 + scale) + shift).astype(BF16)
    o_ref[...] = jnp.dot(h, w_ref[...], preferred_element_type=F32)


def _proj(x, mod, mod_of_tile, w, mi, tm, tn):
    rows = x.shape[0]
    n = w.shape[1]
    return pl.pallas_call(
        functools.partial(_proj_kernel, mi=mi),
        grid=(n // tn, rows // tm),
        in_specs=[pl.BlockSpec((tm, D_MODEL), lambda j, i: (i, 0)),
                  pl.BlockSpec((None, N_MOD, D_MODEL), lambda j, i: (mod_of_tile(i, tm), 0, 0)),
                  pl.BlockSpec((D_MODEL, tn), lambda j, i: (0, j))],
        out_specs=pl.BlockSpec((tm, tn), lambda j, i: (i, j)),
        out_shape=jax.ShapeDtypeStruct((rows, n), F32),
        compiler_params=_cparams(("parallel", "parallel")),
        name="mix_in_proj",
    )(x, mod, w)


def _prep_kernel(cur_ref, prev_ref, next_ref, mu_ref, wup_ref, aup_ref, gup_ref, w0_ref, a0_ref,
                 kk_ref, ka_ref, rk_ref, ones_ref,
                 r_o, v_o, kk_o, lw_o, kd_o, b_o, bonus_o, g_o, ext_scr, *, seq_len, latent, tm):
    i = pl.program_id(0)
    ext_scr[0:HALO, :] = prev_ref[...]
    ext_scr[HALO:HALO + tm, :] = cur_ref[...]
    ext_scr[HALO + tm:HALO + tm + HALO, :] = next_ref[...]

    t = _mod_p2(i * tm + lax.broadcasted_iota(jnp.int32, (tm, LANES), 0), seq_len)
    lane = lax.broadcasted_iota(jnp.int32, (tm, LANES), 1)

    def mixed_cols(c):
        sl = slice(c * LANES, (c + 1) * LANES)
        cur = ext_scr[HALO:HALO + tm, sl]
        before = ext_scr[HALO - 1:HALO - 1 + tm, sl]
        after = ext_scr[HALO + 1:HALO + 1 + tm, sl]
        if latent:
            col = _mod_p2(t, GRID_W)
            up = ext_scr[0:tm, sl]
            down = ext_scr[2 * HALO:2 * HALO + tm, sl]
            which = _mod_p2(lane, 4)
            shifted = jnp.where(
                which == 0, jnp.where(col > 0, before, 0.0),
                jnp.where(which == 1, jnp.where(col < GRID_W - 1, after, 0.0),
                          jnp.where(which == 2, jnp.where(t >= GRID_W, up, 0.0),
                                    jnp.where(t < seq_len - GRID_W, down, 0.0))))
        else:
            shifted = jnp.where(_mod_p2(lane, 2) == 0, jnp.where(t > 0, before, 0.0),
                                jnp.where(t < seq_len - 1, after, 0.0))
        return cur + mu_ref[:, sl] * (shifted - cur)

    def mixed(c0, c1):
        return jnp.concatenate([mixed_cols(c) for c in range(c0, c1)], axis=1)

    nb = D_A // LANES
    r = mixed(0, nb)
    k = mixed(nb, 2 * nb)
    v = mixed(2 * nb, 3 * nb)
    w_down = mixed_cols(3 * nb)
    a_down = mixed_cols(3 * nb + 1)
    g_down = mixed_cols(3 * nb + 2)

    ones_bd = ones_ref[...]
    r_o[...] = r
    v_o[...] = v
    kkraw = k * kk_ref[...]
    ss = _head_sum(kkraw * kkraw, ones_bd)
    kk = kkraw / jnp.maximum(jnp.sqrt(ss), 1e-12)
    kk_o[...] = kk
    bonus_o[...] = _head_sum(r * k * rk_ref[...], ones_bd) * v
    g_o[...] = _dot(_sigmoid(g_down), gup_ref[...])

    w_raw = _dot(jnp.tanh(w_down), wup_ref[...])
    a_raw = _dot(a_down, aup_ref[...])
    for d in range(2):
        sl = slice(d * D_A, (d + 1) * D_A)
        lw_o[d] = -DECAY_SCALE * _sigmoid(w0_ref[d:d + 1, :] + w_raw[:, sl])
        a = _sigmoid(a0_ref[d:d + 1, :] + a_raw[:, sl])
        kd_o[d] = k * (1.0 + (a - 1.0) * ka_ref[...])
        b_o[d] = kk * a


def _prep(ps, p, seq_len, latent, tm):
    rows = ps.shape[0]
    hb = tm // HALO
    n_halo_blocks = rows // HALO
    row1 = lambda a: a.reshape(1, -1)
    full = lambda shape: pl.BlockSpec(shape, lambda i: (0,) * len(shape))
    out_rows = pl.BlockSpec((tm, D_A), lambda i: (i, 0))
    out_dir = pl.BlockSpec((2, tm, D_A), lambda i: (0, i, 0))
    sds = jax.ShapeDtypeStruct
    return pl.pallas_call(
        functools.partial(_prep_kernel, seq_len=seq_len, latent=latent, tm=tm),
        grid=(rows // tm,),
        in_specs=[pl.BlockSpec((tm, SHIFT_W), lambda i: (i, 0)),
                  pl.BlockSpec((HALO, SHIFT_W), lambda i: (jnp.maximum(i * hb - 1, 0), 0)),
                  pl.BlockSpec((HALO, SHIFT_W), lambda i: (jnp.minimum((i + 1) * hb, n_halo_blocks - 1), 0)),
                  full((1, SHIFT_W)), full((LANES, 2 * D_A)), full((LANES, 2 * D_A)), full((LORA_G, D_A)),
                  full((2, D_A)), full((2, D_A)), full((1, D_A)), full((1, D_A)), full((1, D_A)),
                  full((LANES, LANES))],
        out_specs=[out_rows, out_rows, out_rows, out_dir, out_dir, out_dir, out_rows, out_rows],
        out_shape=[sds((rows, D_A), F32), sds((rows, D_A), F32), sds((rows, D_A), F32),
                   sds((2, rows, D_A), F32), sds((2, rows, D_A), F32), sds((2, rows, D_A), F32),
                   sds((rows, D_A), F32), sds((rows, D_A), F32)],
        scratch_shapes=[pltpu.VMEM((tm + 2 * HALO, SHIFT_W), F32)],
        compiler_params=_cparams(("parallel",)),
        name="shift_prep",
    )(ps, ps, ps, row1(p['mu_shift']), p['wup_bd'], p['aup_bd'], p['g_up'], p['w0'], p['a0'],
      row1(p['k_k']), row1(p['k_a']), row1(p['r_k']), p['ones_bd'])


def _scan_kernel(*refs, has_s0, want_state):
    if has_s0:
        (r_ref, v_ref, kk_ref, lw_ref, kd_ref, b_ref, s0_ref), rest = refs[:7], refs[7:]
    else:
        (r_ref, v_ref, kk_ref, lw_ref, kd_ref, b_ref), rest = refs[:6], refs[6:]
    if want_state:
        y_ref, sout_ref, s_scr = rest
    else:
        y_ref, s_scr = rest
    d = pl.program_id(0)
    c = pl.program_id(3)
    C = CHUNK

    row = lax.broadcasted_iota(jnp.int32, (PAIR, PAIR), 0)
    col = lax.broadcasted_iota(jnp.int32, (PAIR, PAIR), 1)
    same_head = _div_p2(row, HEAD_DIM) == _div_p2(col, HEAD_DIM)

    @pl.when(c == 0)
    def _():
        if has_s0:
            z = jnp.zeros((HEAD_DIM, HEAD_DIM), F32)
            top = jnp.concatenate([s0_ref[0], z], axis=1)
            bot = jnp.concatenate([z, s0_ref[1]], axis=1)
            s_scr[...] = jnp.concatenate([top, bot], axis=0)
        else:
            s_scr[...] = jnp.zeros_like(s_scr)

    sign = 1 - 2 * d
    order = (_mod_p2(row, C) - _mod_p2(col, C)) * sign
    strict = (order > 0) & same_head
    incl = (order >= 0)

    r = r_ref[...]
    v = v_ref[...]
    kk = kk_ref[...]
    lw = lw_ref[...]
    kd = kd_ref[...]
    b = b_ref[...]

    lane = lax.broadcasted_iota(jnp.int32, (C, PAIR), 1)
    head0 = lane < HEAD_DIM

    def stack(z):
        return jnp.concatenate([jnp.where(head0, z, 0.0), jnp.where(head0, 0.0, z)], axis=0)

    tri = jnp.where(incl[:C, :C], 1.0, 0.0).astype(BF16)
    l1, l2, l3 = _split3(lw)
    cs = (jnp.dot(tri, l1, preferred_element_type=F32)
          + (jnp.dot(tri, l2, preferred_element_type=F32) + jnp.dot(tri, l3, preferred_element_type=F32)))
    tot = jnp.sum(lw, axis=0, keepdims=True)
    w_inc = jnp.exp(cs)
    w_inv = jnp.exp(-cs)
    w_exc = jnp.exp(cs - lw)
    w_rest = jnp.exp(tot - cs)
    r_t = r * w_inc
    kk_t = kk * w_exc
    k_h = kd * w_inv
    b_h = b * w_inv

    S = s_scr[...]
    kk_st = stack(kk_t)
    v_st = stack(v)
    kb_st = jnp.concatenate([stack(k_h), stack(b_h)], axis=0)
    lhs = jnp.concatenate([kk_st, r_t], axis=0)
    m = _mm3(lhs, kb_st, _NT)
    a_k = jnp.where(strict, m[:2 * C, :2 * C], 0.0)
    x = jnp.where(strict, -m[:2 * C, 2 * C:], 0.0)
    incl_cat = jnp.concatenate([incl[:C, :], incl[:C, :]], axis=1)
    p_cat = jnp.where(incl_cat, m[2 * C:, :], 0.0)

    z = _mm3(kk_st, S, _NT) + _mm3(a_k, v_st)
    pows = [x]
    for _ in range(int(math.log2(C)) - 1):
        pows.append(_mm3(pows[-1], pows[-1]))
    for xp in reversed(pows):
        z = z + _mm3(xp, z)
    u_st = z

    y = _mm3(r_t, S, _NT) + _mm3(p_cat, jnp.concatenate([v_st, -u_st], axis=0))
    y_ref[...] = y

    u = u_st[:C] + u_st[C:]
    upd = _mm3(jnp.concatenate([v, -u], axis=0),
               jnp.concatenate([kd * w_rest, b * w_rest], axis=0), _TN)
    s_new = S * jnp.exp(tot) + jnp.where(same_head, upd, 0.0)
    s_scr[...] = s_new

    if want_state:
        @pl.when(c == pl.num_programs(3) - 1)
        def _():
            sout_ref[0] = s_new[:HEAD_DIM, :HEAD_DIM]
            sout_ref[1] = s_new[HEAD_DIM:, HEAD_DIM:]


def _scan(r, v, kk, lw, kd, b, s0, batch, seq_len, want_state):
    rows = r.shape[0]
    nc = seq_len // CHUNK
    n_pairs = D_A // PAIR

    def chunk_idx(d, bi, c):
        return bi * nc + c + d * (nc - 1 - 2 * c)

    shared = pl.BlockSpec((CHUNK, PAIR), lambda d, bi, p, c: (chunk_idx(d, bi, c), p))
    per_dir = pl.BlockSpec((None, CHUNK, PAIR), lambda d, bi, p, c: (d, chunk_idx(d, bi, c), p))
    in_specs = [shared, shared, shared, per_dir, per_dir, per_dir]
    args = [r, v, kk, lw, kd, b]
    if s0 is not None:
        in_specs.append(pl.BlockSpec((None, None, 2, HEAD_DIM, HEAD_DIM), lambda d, bi, p, c: (bi, d, p, 0, 0)))
        args.append(s0)
    out_specs = [per_dir]
    out_shape = [jax.ShapeDtypeStruct((2, rows, D_A), F32)]
    if want_state:
        out_specs.append(pl.BlockSpec((None, None, 2, HEAD_DIM, HEAD_DIM), lambda d, bi, p, c: (bi, d, p, 0, 0)))
        out_shape.append(jax.ShapeDtypeStruct((batch, 2, N_HEADS, HEAD_DIM, HEAD_DIM), F32))
    return pl.pallas_call(
        functools.partial(_scan_kernel, has_s0=s0 is not None, want_state=want_state),
        grid=(2, batch, n_pairs, nc),
        in_specs=in_specs,
        out_specs=out_specs,
        out_shape=out_shape,
        scratch_shapes=[pltpu.VMEM((PAIR, PAIR), F32)],
        compiler_params=_cparams(("parallel", "parallel", "parallel", "arbitrary")),
        name="rwkv7_scan",
    )(*args)


def _post_kernel(y_ref, bonus_ref, g_ref, pool_ref, pprev_ref, pnext_ref, gates_ref, x_ref, mod_ref,
                 lnxg_ref, lnxb_ref, worwkv_ref, wpool_ref, pscale_ref, wopool_ref, wout_ref,
                 lng_ref, lnb_ref, ones_ref, o_ref, ext_scr, *, seq_len, tm):
    i = pl.program_id(0)
    ones_bd = ones_ref[...]
    y = y_ref[0] + y_ref[1]
    mu = _head_sum(y, ones_bd) * (1.0 / HEAD_DIM)
    dlt = y - mu
    var = _head_sum(dlt * dlt, ones_bd) * (1.0 / HEAD_DIM)
    yn = dlt * lax.rsqrt(var + GN_EPS) * lnxg_ref[...] + lnxb_ref[...]
    ya = _dot((yn + bonus_ref[...]) * g_ref[...], worwkv_ref[...])

    ext_scr[0:POOL_HALO, :] = pprev_ref[...]
    ext_scr[POOL_HALO:POOL_HALO + tm, :] = pool_ref[...]
    ext_scr[POOL_HALO + tm:POOL_HALO + tm + POOL_HALO, :] = pnext_ref[...]
    t = _mod_p2(i * tm + lax.broadcasted_iota(jnp.int32, (tm, LANES), 0), seq_len)
    us = []
    for gi, w in enumerate(POOL_WINDOWS):
        sl = slice(gi * POOL_GROUP_DIM, (gi + 1) * POOL_GROUP_DIM)
        acc = jnp.zeros((tm, POOL_GROUP_DIM), F32)
        for off in range(-(w // 2), w - w // 2):
            rows = ext_scr[POOL_HALO + off:POOL_HALO + off + tm, sl]
            valid = (t + off >= 0) & (t + off < seq_len)
            acc = acc + jnp.where(valid, rows, 0.0)
        cnt = jnp.minimum(t + (w - w // 2), seq_len) - jnp.maximum(t - w // 2, 0)
        resid = acc / cnt.astype(F32) - ext_scr[POOL_HALO:POOL_HALO + tm, sl]
        us.append(_dot(resid, wpool_ref[gi]))
    u = jnp.concatenate(us, axis=1) * pscale_ref[...]
    yb = _dot(u, wopool_ref[...])

    gates = gates_ref[...]
    merged = _sigmoid(gates[:, :D_MODEL]) * ya + _sigmoid(gates[:, D_MODEL:]) * yb
    mix = _dot(merged, wout_ref[...])
    z = ALPHA * x_ref[...] + mod_ref[5:6, :] * mix
    o_ref[...] = _layer_norm(z, lng_ref[...], lnb_ref[...])


def _post(y, bonus, g, pool_in, gates, x, mod, mod_of_tile, p, seq_len, tm):
    rows = x.shape[0]
    hb = tm // POOL_HALO
    n_halo_blocks = rows // POOL_HALO
    row1 = lambda a: a.reshape(1, -1)
    full = lambda shape: pl.BlockSpec(shape, lambda i: (0,) * len(shape))
    tile = lambda width: pl.BlockSpec((tm, width), lambda i: (i, 0))
    return pl.pallas_call(
        functools.partial(_post_kernel, seq_len=seq_len, tm=tm),
        grid=(rows // tm,),
        in_specs=[pl.BlockSpec((2, tm, D_A), lambda i: (0, i, 0)), tile(D_A), tile(D_A), tile(D_B),
                  pl.BlockSpec((POOL_HALO, D_B), lambda i: (jnp.maximum(i * hb - 1, 0), 0)),
                  pl.BlockSpec((POOL_HALO, D_B), lambda i: (jnp.minimum((i + 1) * hb, n_halo_blocks - 1), 0)),
                  tile(2 * D_MODEL), tile(D_MODEL),
                  pl.BlockSpec((None, N_MOD, D_MODEL), lambda i: (mod_of_tile(i, tm), 0, 0)),
                  full((1, D_A)), full((1, D_A)), full((D_A, D_MODEL)),
                  full((len(POOL_WINDOWS), POOL_GROUP_DIM, POOL_GROUP_DIM)), full((1, D_B)),
                  full((D_B, D_MODEL)), full((D_MODEL, D_MODEL)), full((1, D_MODEL)), full((1, D_MODEL)),
                  full((LANES, LANES))],
        out_specs=tile(D_MODEL),
        out_shape=jax.ShapeDtypeStruct((rows, D_MODEL), F32),
        scratch_shapes=[pltpu.VMEM((tm + 2 * POOL_HALO, D_B), F32)],
        compiler_params=_cparams(("parallel",)),
        name="mixer_out",
    )(y, bonus, g, pool_in, pool_in, pool_in, gates, x, mod, row1(p['lnx_g']), row1(p['lnx_b']),
      p['w_o_rwkv'], p['w_pool'], row1(p['pool_scale']), p['w_o_pool'], p['w_out'],
      row1(p['ln_g'][1]), row1(p['ln_b'][1]), p['ones_bd'])


def _trunk_layer(x, mod, mod_of_tile, s0, batch, seq_len, latent, want_state, p):
    tm = 256
    x1 = _ffn(x, mod, mod_of_tile, p['ffn_in'][0], p['ffn_out'][0], p['ln_g'][0], p['ln_b'][0], 0, 512)
    ps = _proj(x1, mod, mod_of_tile, p['w_mix_shift'], 3, 512, 1152)
    pool_in = _proj(x1, mod, mod_of_tile, p['w_mix_pool'], 3, 512, D_B)
    gates = _proj(x1, mod, mod_of_tile, p['w_mix_gate'], 3, 512, 1024)
    r, v, kk, lw, kd, b, bonus, g = _prep(ps, p, seq_len, latent, tm)
    outs = _scan(r, v, kk, lw, kd, b, s0, batch, seq_len, want_state)
    y = outs[0]
    x2 = _post(y, bonus, g, pool_in, gates, x1, mod, mod_of_tile, p, seq_len, tm)
    x3 = _ffn(x2, mod, mod_of_tile, p['ffn_in'][1], p['ffn_out'][1], p['ln_g'][2], p['ln_b'][2], 6, 512)
    return x3, (outs[1] if want_state else None)


def _block_diag2(w):
    z = jnp.zeros_like(w[0])
    return jnp.concatenate([jnp.concatenate([w[0], z], axis=1), jnp.concatenate([z, w[1]], axis=1)], axis=0)


def kernel(x_prompt, x_sample, c, state_rwkv, c_ctx, w_mod, b_mod, ln_g, ln_b, ffn_in, ffn_out, w_mix_in,
           mu_shift, w0, w_up, a0, a_up, g_up, k_k, k_a, r_k, lnx_g, lnx_b, w_o_rwkv, w_pool, pool_scale,
           w_o_pool, w_out):
    batch, seq, _ = x_prompt.shape
    dec_batch, dec_seq, _ = x_sample.shape
    y_p = x_prompt.reshape(batch * seq, D_MODEL)
    y_s = x_sample.reshape(dec_batch * dec_seq, D_MODEL)
    cvecs = jnp.concatenate([c_ctx[None, :], c, jnp.zeros((8 - 1 - dec_batch, D_MODEL), F32)], axis=0)
    ii = lax.broadcasted_iota(jnp.int32, (LANES, LANES), 0) // HEAD_DIM
    jj = lax.broadcasted_iota(jnp.int32, (LANES, LANES), 1) // HEAD_DIM
    ones_bd = (ii == jj).astype(BF16)
    ctx_states = []
    for l in range(DEPTH):
        wm = w_mix_in[l]
        p = {
            'ln_g': ln_g[l], 'ln_b': ln_b[l],
            'ffn_in': ffn_in[l].astype(BF16), 'ffn_out': ffn_out[l].astype(BF16),
            'w_mix_shift': wm[:, :SHIFT_W].astype(BF16),
            'w_mix_pool': wm[:, SHIFT_W:SHIFT_W + D_B].astype(BF16),
            'w_mix_gate': wm[:, SHIFT_W + D_B:].astype(BF16),
            'mu_shift': mu_shift[l], 'w0': w0[l], 'a0': a0[l],
            'wup_bd': _block_diag2(w_up[l]).astype(BF16), 'aup_bd': _block_diag2(a_up[l]).astype(BF16),
            'g_up': g_up[l].astype(BF16), 'k_k': k_k[l], 'k_a': k_a[l], 'r_k': r_k[l],
            'lnx_g': lnx_g[l], 'lnx_b': lnx_b[l], 'w_o_rwkv': w_o_rwkv[l].astype(BF16),
            'w_pool': w_pool[l].astype(BF16), 'pool_scale': pool_scale[l],
            'w_o_pool': w_o_pool[l].astype(BF16), 'w_out': w_out[l].astype(BF16), 'ones_bd': ones_bd,
        }
        mod = _modulation(cvecs, w_mod[l], b_mod[l]).reshape(8, N_MOD, D_MODEL)
        y_p, s_ctx = _trunk_layer(y_p, mod, lambda i, tm: 0, None, batch, seq, False, True, p)
        ctx_states.append(s_ctx)
        y_s, _ = _trunk_layer(y_s, mod, lambda i, tm: 1 + (i * tm) // dec_seq, state_rwkv[:, l],
                              dec_batch, dec_seq, True, False, p)
    new_state = jnp.stack(ctx_states, axis=1).astype(x_prompt.dtype)
    return (y_p.reshape(batch, seq, D_MODEL), y_s.reshape(dec_batch, dec_seq, D_MODEL), new_state)
```

```python
import functools
import math

import jax
import jax.numpy as jnp
from jax import lax
from jax.experimental import pallas as pl
from jax.experimental.pallas import tpu as pltpu

F32 = jnp.float32
BF16 = jnp.bfloat16

D_MODEL = 1024
DEPTH = 1
GRID_W = 64
HEAD_DIM = 64
D_A = D_MODEL
N_HEADS = D_A // HEAD_DIM
D_B = D_MODEL // 2
POOL_WINDOWS = (2, 4, 8, 16)
POOL_GROUP_DIM = D_B // len(POOL_WINDOWS)
LORA_W = 64
LORA_A = 64
LORA_G = 128
D_FF = 2816
N_MOD = 9
SHIFT_W = 3 * D_A + 2 * LORA_W + 2 * LORA_A + LORA_G
ALPHA = (2 * DEPTH) ** 0.25
LN_EPS = 1e-5
GN_EPS = 64e-5
DECAY_SCALE = math.exp(-0.5)

LANES = 128
HALO = 64
POOL_HALO = 8
CHUNK = 64
PAIR = 2 * HEAD_DIM
VMEM_LIMIT = 56 * 1024 * 1024


def _cparams(sem):
    return pltpu.CompilerParams(dimension_semantics=sem, vmem_limit_bytes=VMEM_LIMIT)


def _mod_p2(x, n):
    assert n & (n - 1) == 0
    return x & (n - 1)


def _div_p2(x, n):
    assert n & (n - 1) == 0
    return x >> (n.bit_length() - 1)


def _sigmoid(x):
    return 1.0 / (1.0 + jnp.exp(-x))


def _silu(x):
    return x / (1.0 + jnp.exp(-x))


def _dot(a, b):
    return jnp.dot(a.astype(BF16), b.astype(BF16), preferred_element_type=F32)


def _split2(x):
    hi = x.astype(BF16)
    lo = (x - hi.astype(F32)).astype(BF16)
    return hi, lo


def _split3(x):
    hi = x.astype(BF16)
    r1 = x - hi.astype(F32)
    mid = r1.astype(BF16)
    lo = (r1 - mid.astype(F32)).astype(BF16)
    return hi, mid, lo


_NN = (((1,), (0,)), ((), ()))
_NT = (((1,), (1,)), ((), ()))
_TN = (((0,), (0,)), ((), ()))


def _dg(a, b, dims):
    return lax.dot_general(a, b, dims, preferred_element_type=F32)


def _mm3(a, b, dims=_NN):
    ah, al = _split2(a)
    bh, bl = _split2(b)
    return _dg(ah, bh, dims) + (_dg(ah, bl, dims) + _dg(al, bh, dims))


def _layer_norm(z, g, b):
    mu = jnp.mean(z, axis=-1, keepdims=True)
    d = z - mu
    var = jnp.mean(d * d, axis=-1, keepdims=True)
    return d * lax.rsqrt(var + LN_EPS) * g + b


def _head_sum(x, ones_bd):
    hi, lo = _split2(x)
    outs = []
    for c in range(x.shape[1] // LANES):
        sl = slice(c * LANES, (c + 1) * LANES)
        outs.append(jnp.dot(hi[:, sl], ones_bd, preferred_element_type=F32)
                    + jnp.dot(lo[:, sl], ones_bd, preferred_element_type=F32))
    return jnp.concatenate(outs, axis=1)


def _mod_kernel(c_ref, w_ref, b_ref, o_ref):
    o_ref[...] = _dot(_silu(c_ref[...]), w_ref[...]) + b_ref[...]


def _modulation(cvecs, w_mod, b_mod):
    rows = cvecs.shape[0]
    n = w_mod.shape[1]
    tn = 1024
    return pl.pallas_call(
        _mod_kernel,
        grid=(n // tn,),
        in_specs=[pl.BlockSpec((rows, D_MODEL), lambda j: (0, 0)),
                  pl.BlockSpec((D_MODEL, tn), lambda j: (0, j)),
                  pl.BlockSpec((1, tn), lambda j: (0, j))],
        out_specs=pl.BlockSpec((rows, tn), lambda j: (0, j)),
        out_shape=jax.ShapeDtypeStruct((rows, n), F32),
        compiler_params=_cparams(("arbitrary",)),
        name="modulation",
    )(cvecs, w_mod, b_mod.reshape(1, n))


def _ffn_kernel(x_ref, mod_ref, wg_ref, wu_ref, wo_ref, lng_ref, lnb_ref, o_ref, h_scr, acc_scr, *, mi):
    j = pl.program_id(1)

    @pl.when(j == 0)
    def _():
        shift = mod_ref[mi:mi + 1, :]
        scale = mod_ref[mi + 1:mi + 2, :]
        h_scr[...] = (x_ref[...] * (1.0 + scale) + shift).astype(BF16)
        acc_scr[...] = jnp.zeros_like(acc_scr)

    h = h_scr[...]
    gate = jnp.dot(h, wg_ref[...], preferred_element_type=F32)
    up = jnp.dot(h, wu_ref[...], preferred_element_type=F32)
    acc_scr[...] += _dot(_silu(gate) * up, wo_ref[...])

    @pl.when(j == pl.num_programs(1) - 1)
    def _():
        g = mod_ref[mi + 2:mi + 3, :]
        z = ALPHA * x_ref[...] + 0.5 * g * acc_scr[...]
        o_ref[...] = _layer_norm(z, lng_ref[...], lnb_ref[...])


def _ffn(x, mod, mod_of_tile, w_in, w_out, ln_g, ln_b, mi, tm):
    rows = x.shape[0]
    tf = 256
    nf = D_FF // tf
    return pl.pallas_call(
        functools.partial(_ffn_kernel, mi=mi),
        grid=(rows // tm, nf),
        in_specs=[pl.BlockSpec((tm, D_MODEL), lambda i, j: (i, 0)),
                  pl.BlockSpec((None, N_MOD, D_MODEL), lambda i, j: (mod_of_tile(i, tm), 0, 0)),
                  pl.BlockSpec((D_MODEL, tf), lambda i, j: (0, j)),
                  pl.BlockSpec((D_MODEL, tf), lambda i, j: (0, j + nf)),
                  pl.BlockSpec((tf, D_MODEL), lambda i, j: (j, 0)),
                  pl.BlockSpec((1, D_MODEL), lambda i, j: (0, 0)),
                  pl.BlockSpec((1, D_MODEL), lambda i, j: (0, 0))],
        out_specs=pl.BlockSpec((tm, D_MODEL), lambda i, j: (i, 0)),
        out_shape=jax.ShapeDtypeStruct((rows, D_MODEL), F32),
        scratch_shapes=[pltpu.VMEM((tm, D_MODEL), BF16), pltpu.VMEM((tm, D_MODEL), F32)],
        compiler_params=_cparams(("parallel", "arbitrary")),
        name="ffn",
    )(x, mod, w_in, w_in, w_out, ln_g.reshape(1, -1), ln_b.reshape(1, -1))


def _proj_kernel(x_ref, mod_ref, w_ref, o_ref, *, mi):
    shift = mod_ref[mi:mi + 1, :]
    scale = mod_ref[mi + 1:mi + 2, :]
    h = (x_ref[...] * (1.0 + scale) + shift).astype(BF16)
    o_ref[...] = jnp.dot(h, w_ref[...], preferred_element_type=F32)


def _proj(x, mod, mod_of_tile, w, mi, tm, tn):
    rows = x.shape[0]
    n = w.shape[1]
    return pl.pallas_call(
        functools.partial(_proj_kernel, mi=mi),
        grid=(n // tn, rows // tm),
        in_specs=[pl.BlockSpec((tm, D_MODEL), lambda j, i: (i, 0)),
                  pl.BlockSpec((None, N_MOD, D_MODEL), lambda j, i: (mod_of_tile(i, tm), 0, 0)),
                  pl.BlockSpec((D_MODEL, tn), lambda j, i: (0, j))],
        out_specs=pl.BlockSpec((tm, tn), lambda j, i: (i, j)),
        out_shape=jax.ShapeDtypeStruct((rows, n), F32),
        compiler_params=_cparams(("parallel", "parallel")),
        name="mix_in_proj",
    )(x, mod, w)


def _prep_kernel(cur_ref, prev_ref, next_ref, mu_ref, wup_ref, aup_ref, gup_ref, w0_ref, a0_ref,
                 kk_ref, ka_ref, rk_ref, ones_ref,
                 r_o, v_o, kk_o, lw_o, kd_o, b_o, bonus_o, g_o, ext_scr, *, seq_len, latent, tm):
    i = pl.program_id(0)
    ext_scr[0:HALO, :] = prev_ref[...]
    ext_scr[HALO:HALO + tm, :] = cur_ref[...]
    ext_scr[HALO + tm:HALO + tm + HALO, :] = next_ref[...]

    t = _mod_p2(i * tm + lax.broadcasted_iota(jnp.int32, (tm, LANES), 0), seq_len)
    lane = lax.broadcasted_iota(jnp.int32, (tm, LANES), 1)

    def mixed_cols(c):
        sl = slice(c * LANES, (c + 1) * LANES)
        cur = ext_scr[HALO:HALO + tm, sl]
        before = ext_scr[HALO - 1:HALO - 1 + tm, sl]
        after = ext_scr[HALO + 1:HALO + 1 + tm, sl]
        if latent:
            col = _mod_p2(t, GRID_W)
            up = ext_scr[0:tm, sl]
            down = ext_scr[2 * HALO:2 * HALO + tm, sl]
            which = _mod_p2(lane, 4)
            shifted = jnp.where(
                which == 0, jnp.where(col > 0, before, 0.0),
                jnp.where(which == 1, jnp.where(col < GRID_W - 1, after, 0.0),
                          jnp.where(which == 2, jnp.where(t >= GRID_W, up, 0.0),
                                    jnp.where(t < seq_len - GRID_W, down, 0.0))))
        else:
            shifted = jnp.where(_mod_p2(lane, 2) == 0, jnp.where(t > 0, before, 0.0),
                                jnp.where(t < seq_len - 1, after, 0.0))
        return cur + mu_ref[:, sl] * (shifted - cur)

    def mixed(c0, c1):
        return jnp.concatenate([mixed_cols(c) for c in range(c0, c1)], axis=1)

    nb = D_A // LANES
    r = mixed(0, nb)
    k = mixed(nb, 2 * nb)
    v = mixed(2 * nb, 3 * nb)
    w_down = mixed_cols(3 * nb)
    a_down = mixed_cols(3 * nb + 1)
    g_down = mixed_cols(3 * nb + 2)

    ones_bd = ones_ref[...]
    r_o[...] = r
    v_o[...] = v
    kkraw = k * kk_ref[...]
    ss = _head_sum(kkraw * kkraw, ones_bd)
    kk = kkraw / jnp.maximum(jnp.sqrt(ss), 1e-12)
    kk_o[...] = kk
    bonus_o[...] = _head_sum(r * k * rk_ref[...], ones_bd) * v
    g_o[...] = _dot(_sigmoid(g_down), gup_ref[...])

    w_raw = _dot(jnp.tanh(w_down), wup_ref[...])
    a_raw = _dot(a_down, aup_ref[...])
    for d in range(2):
        sl = slice(d * D_A, (d + 1) * D_A)
        lw_o[d] = -DECAY_SCALE * _sigmoid(w0_ref[d:d + 1, :] + w_raw[:, sl])
        a = _sigmoid(a0_ref[d:d + 1, :] + a_raw[:, sl])
        kd_o[d] = k * (1.0 + (a - 1.0) * ka_ref[...])
        b_o[d] = kk * a


def _prep(ps, p, seq_len, latent, tm):
    rows = ps.shape[0]
    hb = tm // HALO
    n_halo_blocks = rows // HALO
    row1 = lambda a: a.reshape(1, -1)
    full = lambda shape: pl.BlockSpec(shape, lambda i: (0,) * len(shape))
    out_rows = pl.BlockSpec((tm, D_A), lambda i: (i, 0))
    out_dir = pl.BlockSpec((2, tm, D_A), lambda i: (0, i, 0))
    sds = jax.ShapeDtypeStruct
    return pl.pallas_call(
        functools.partial(_prep_kernel, seq_len=seq_len, latent=latent, tm=tm),
        grid=(rows // tm,),
        in_specs=[pl.BlockSpec((tm, SHIFT_W), lambda i: (i, 0)),
                  pl.BlockSpec((HALO, SHIFT_W), lambda i: (jnp.maximum(i * hb - 1, 0), 0)),
                  pl.BlockSpec((HALO, SHIFT_W), lambda i: (jnp.minimum((i + 1) * hb, n_halo_blocks - 1), 0)),
                  full((1, SHIFT_W)), full((LANES, 2 * D_A)), full((LANES, 2 * D_A)), full((LORA_G, D_A)),
                  full((2, D_A)), full((2, D_A)), full((1, D_A)), full((1, D_A)), full((1, D_A)),
                  full((LANES, LANES))],
        out_specs=[out_rows, out_rows, out_rows, out_dir, out_dir, out_dir, out_rows, out_rows],
        out_shape=[sds((rows, D_A), F32), sds((rows, D_A), F32), sds((rows, D_A), F32),
                   sds((2, rows, D_A), F32), sds((2, rows, D_A), F32), sds((2, rows, D_A), F32),
                   sds((rows, D_A), F32), sds((rows, D_A), F32)],
        scratch_shapes=[pltpu.VMEM((tm + 2 * HALO, SHIFT_W), F32)],
        compiler_params=_cparams(("parallel",)),
        name="shift_prep",
    )(ps, ps, ps, row1(p['mu_shift']), p['wup_bd'], p['aup_bd'], p['g_up'], p['w0'], p['a0'],
      row1(p['k_k']), row1(p['k_a']), row1(p['r_k']), p['ones_bd'])


SCAN_SPLITS = {
    'gram': (1, 1), 'akv': (1, 1), 'square': (1, 1), 'apply': (1, 1), 'pkv': (1, 1), 'pbq': (1, 1),
    'trans': (1, 1), 'sadd': (2, 2), 'y_state': (1, 1), 's_state': (1, 1),
}
SCAN_GROUP = 8


def _parts(x, n):
    return (x.astype(BF16),) if n == 1 else _split2(x)


def _mmp(ap, bp, dims=_NN):
    out = _dg(ap[0], bp[0], dims)
    if len(bp) > 1:
        out = out + _dg(ap[0], bp[1], dims)
    if len(ap) > 1:
        out = out + _dg(ap[1], bp[0], dims)
    return out


def _mms(site, a, b, dims=_NN):
    na, nb = SCAN_SPLITS[site]
    return _mmp(_parts(a, na), _parts(b, nb), dims)


def _scan_kernel(*refs, has_s0, want_state, group):
    if has_s0:
        (r_ref, v_ref, kk_ref, lw_ref, kd_ref, b_ref, s0_ref), rest = refs[:7], refs[7:]
    else:
        (r_ref, v_ref, kk_ref, lw_ref, kd_ref, b_ref), rest = refs[:6], refs[6:]
    if want_state:
        y_ref, sout_ref, s_scr = rest
    else:
        y_ref, s_scr = rest
    d = pl.program_id(0)
    c = pl.program_id(3)
    C = CHUNK

    @pl.when(c == 0)
    def _():
        if has_s0:
            z = jnp.zeros((HEAD_DIM, HEAD_DIM), F32)
            for p in range(group):
                top = jnp.concatenate([s0_ref[2 * p], z], axis=1)
                bot = jnp.concatenate([z, s0_ref[2 * p + 1]], axis=1)
                s_scr[p] = jnp.concatenate([top, bot], axis=0)
        else:
            s_scr[...] = jnp.zeros_like(s_scr)

    row = lax.broadcasted_iota(jnp.int32, (PAIR, PAIR), 0)
    col = lax.broadcasted_iota(jnp.int32, (PAIR, PAIR), 1)
    same_head = _div_p2(row, HEAD_DIM) == _div_p2(col, HEAD_DIM)
    sign = 1 - 2 * d
    order = (_mod_p2(row, C) - _mod_p2(col, C)) * sign
    strict = (order > 0) & same_head
    incl = (order >= 0)[:C, :]
    tri = jnp.where(incl[:, :C], 1.0, 0.0).astype(BF16)
    head0 = lax.broadcasted_iota(jnp.int32, (C, PAIR), 1) < HEAD_DIM

    def stack(z):
        return jnp.concatenate([jnp.where(head0, z, 0.0), jnp.where(head0, 0.0, z)], axis=0)

    def each(fn, *lists):
        return [fn(*args) for args in zip(*lists)]

    def cumsum(lw):
        l1, l2, l3 = _split3(lw)
        return (jnp.dot(tri, l1, preferred_element_type=F32)
                + (jnp.dot(tri, l2, preferred_element_type=F32) + jnp.dot(tri, l3, preferred_element_type=F32)))

    sls = [slice(p * PAIR, (p + 1) * PAIR) for p in range(group)]
    lw = [lw_ref[:, sl] for sl in sls]
    cs = each(cumsum, lw)
    tot = [jnp.sum(z, axis=0, keepdims=True) for z in lw]
    w_inv = [jnp.exp(-z) for z in cs]
    w_rest = each(lambda t, z: jnp.exp(t - z), tot, cs)
    r_t = each(lambda sl, z: r_ref[:, sl] * jnp.exp(z), sls, cs)
    kk_st = each(lambda sl, z, l: stack(kk_ref[:, sl] * jnp.exp(z - l)), sls, cs, lw)
    v_st = [stack(v_ref[:, sl]) for sl in sls]
    kb_st = each(lambda sl, wi: jnp.concatenate([stack(kd_ref[:, sl] * wi), stack(b_ref[:, sl] * wi)], axis=0),
                 sls, w_inv)
    m = each(lambda a, rt, kb: _mms('gram', jnp.concatenate([a, rt], axis=0), kb, _NT), kk_st, r_t, kb_st)
    a_k = [jnp.where(strict, z[:2 * C, :2 * C], 0.0) for z in m]
    x = [jnp.where(strict, -z[:2 * C, 2 * C:], 0.0) for z in m]
    p_k = [jnp.where(incl, z[2 * C:, :2 * C], 0.0) for z in m]
    p_b = [jnp.where(incl, z[2 * C:, 2 * C:], 0.0) for z in m]

    q = each(lambda a, ak, vs: jnp.concatenate([a, _mms('akv', ak, vs)], axis=1), kk_st, a_k, v_st)
    pows = [x]
    for _ in range(int(math.log2(C)) - 1):
        pows.append([_mms('square', z, z) for z in pows[-1]])
    for xp in reversed(pows):
        q = each(lambda xi, qi: qi + _mms('apply', xi, qi), xp, q)

    ry = each(lambda rt, pk, vs, pb, qi: jnp.concatenate([rt, _mms('pkv', pk, vs)], axis=1) - _mms('pbq', pb, qi),
              r_t, p_k, v_st, p_b, q)
    gq = each(lambda qi, sl, wr: _mms('trans', qi, stack(b_ref[:, sl] * wr), _TN), q, sls, w_rest)
    s_add = each(lambda vs, sl, wr, g: _mms('sadd', vs, stack(kd_ref[:, sl] * wr), _TN) - g[PAIR:],
                 v_st, sls, w_rest, gq)

    s_old = [s_scr[p] for p in range(group)]
    for p, sl in enumerate(sls):
        y_ref[:, sl] = _mms('y_state', ry[p][:, :PAIR], s_old[p], _NT) + ry[p][:, PAIR:]
    s_new = each(lambda s, t, g, sa: s * jnp.exp(t) - _mms('s_state', s, g[:PAIR]) + sa, s_old, tot, gq, s_add)
    for p in range(group):
        s_scr[p] = s_new[p]

    if want_state:
        @pl.when(c == pl.num_programs(3) - 1)
        def _():
            for p in range(group):
                sout_ref[2 * p] = s_new[p][:HEAD_DIM, :HEAD_DIM]
                sout_ref[2 * p + 1] = s_new[p][HEAD_DIM:, HEAD_DIM:]


def _scan(r, v, kk, lw, kd, b, s0, batch, seq_len, want_state):
    rows = r.shape[0]
    nc = seq_len // CHUNK
    group = SCAN_GROUP
    n_groups = D_A // (PAIR * group)
    gw = group * PAIR

    def chunk_idx(d, bi, c):
        return bi * nc + c + d * (nc - 1 - 2 * c)

    shared = pl.BlockSpec((CHUNK, gw), lambda d, bi, p, c: (chunk_idx(d, bi, c), p))
    per_dir = pl.BlockSpec((None, CHUNK, gw), lambda d, bi, p, c: (d, chunk_idx(d, bi, c), p))
    state = pl.BlockSpec((None, None, 2 * group, HEAD_DIM, HEAD_DIM), lambda d, bi, p, c: (bi, d, p, 0, 0))
    in_specs = [shared, shared, shared, per_dir, per_dir, per_dir]
    args = [r, v, kk, lw, kd, b]
    if s0 is not None:
        in_specs.append(state)
        args.append(s0)
    out_specs = [per_dir]
    out_shape = [jax.ShapeDtypeStruct((2, rows, D_A), F32)]
    if want_state:
        out_specs.append(state)
        out_shape.append(jax.ShapeDtypeStruct((batch, 2, N_HEADS, HEAD_DIM, HEAD_DIM), F32))
    return pl.pallas_call(
        functools.partial(_scan_kernel, has_s0=s0 is not None, want_state=want_state, group=group),
        grid=(2, batch, n_groups, nc),
        in_specs=in_specs,
        out_specs=out_specs,
        out_shape=out_shape,
        scratch_shapes=[pltpu.VMEM((group, PAIR, PAIR), F32)],
        compiler_params=_cparams(("parallel", "parallel", "parallel", "arbitrary")),
        name="rwkv7_scan",
    )(*args)


def _post_kernel(y_ref, bonus_ref, g_ref, pool_ref, pprev_ref, pnext_ref, gates_ref, x_ref, mod_ref,
                 lnxg_ref, lnxb_ref, worwkv_ref, wpool_ref, pscale_ref, wopool_ref, wout_ref,
                 lng_ref, lnb_ref, ones_ref, o_ref, ext_scr, *, seq_len, tm):
    i = pl.program_id(0)
    ones_bd = ones_ref[...]
    y = y_ref[0] + y_ref[1]
    mu = _head_sum(y, ones_bd) * (1.0 / HEAD_DIM)
    dlt = y - mu
    var = _head_sum(dlt * dlt, ones_bd) * (1.0 / HEAD_DIM)
    yn = dlt * lax.rsqrt(var + GN_EPS) * lnxg_ref[...] + lnxb_ref[...]
    ya = _dot((yn + bonus_ref[...]) * g_ref[...], worwkv_ref[...])

    ext_scr[0:POOL_HALO, :] = pprev_ref[...]
    ext_scr[POOL_HALO:POOL_HALO + tm, :] = pool_ref[...]
    ext_scr[POOL_HALO + tm:POOL_HALO + tm + POOL_HALO, :] = pnext_ref[...]
    t = _mod_p2(i * tm + lax.broadcasted_iota(jnp.int32, (tm, LANES), 0), seq_len)
    us = []
    for gi, w in enumerate(POOL_WINDOWS):
        sl = slice(gi * POOL_GROUP_DIM, (gi + 1) * POOL_GROUP_DIM)
        acc = jnp.zeros((tm, POOL_GROUP_DIM), F32)
        for off in range(-(w // 2), w - w // 2):
            rows = ext_scr[POOL_HALO + off:POOL_HALO + off + tm, sl]
            valid = (t + off >= 0) & (t + off < seq_len)
            acc = acc + jnp.where(valid, rows, 0.0)
        cnt = jnp.minimum(t + (w - w // 2), seq_len) - jnp.maximum(t - w // 2, 0)
        resid = acc / cnt.astype(F32) - ext_scr[POOL_HALO:POOL_HALO + tm, sl]
        us.append(_dot(resid, wpool_ref[gi]))
    u = jnp.concatenate(us, axis=1) * pscale_ref[...]
    yb = _dot(u, wopool_ref[...])

    gates = gates_ref[...]
    merged = _sigmoid(gates[:, :D_MODEL]) * ya + _sigmoid(gates[:, D_MODEL:]) * yb
    mix = _dot(merged, wout_ref[...])
    z = ALPHA * x_ref[...] + mod_ref[5:6, :] * mix
    o_ref[...] = _layer_norm(z, lng_ref[...], lnb_ref[...])


def _post(y, bonus, g, pool_in, gates, x, mod, mod_of_tile, p, seq_len, tm):
    rows = x.shape[0]
    hb = tm // POOL_HALO
    n_halo_blocks = rows // POOL_HALO
    row1 = lambda a: a.reshape(1, -1)
    full = lambda shape: pl.BlockSpec(shape, lambda i: (0,) * len(shape))
    tile = lambda width: pl.BlockSpec((tm, width), lambda i: (i, 0))
    return pl.pallas_call(
        functools.partial(_post_kernel, seq_len=seq_len, tm=tm),
        grid=(rows // tm,),
        in_specs=[pl.BlockSpec((2, tm, D_A), lambda i: (0, i, 0)), tile(D_A), tile(D_A), tile(D_B),
                  pl.BlockSpec((POOL_HALO, D_B), lambda i: (jnp.maximum(i * hb - 1, 0), 0)),
                  pl.BlockSpec((POOL_HALO, D_B), lambda i: (jnp.minimum((i + 1) * hb, n_halo_blocks - 1), 0)),
                  tile(2 * D_MODEL), tile(D_MODEL),
                  pl.BlockSpec((None, N_MOD, D_MODEL), lambda i: (mod_of_tile(i, tm), 0, 0)),
                  full((1, D_A)), full((1, D_A)), full((D_A, D_MODEL)),
                  full((len(POOL_WINDOWS), POOL_GROUP_DIM, POOL_GROUP_DIM)), full((1, D_B)),
                  full((D_B, D_MODEL)), full((D_MODEL, D_MODEL)), full((1, D_MODEL)), full((1, D_MODEL)),
                  full((LANES, LANES))],
        out_specs=tile(D_MODEL),
        out_shape=jax.ShapeDtypeStruct((rows, D_MODEL), F32),
        scratch_shapes=[pltpu.VMEM((tm + 2 * POOL_HALO, D_B), F32)],
        compiler_params=_cparams(("parallel",)),
        name="mixer_out",
    )(y, bonus, g, pool_in, pool_in, pool_in, gates, x, mod, row1(p['lnx_g']), row1(p['lnx_b']),
      p['w_o_rwkv'], p['w_pool'], row1(p['pool_scale']), p['w_o_pool'], p['w_out'],
      row1(p['ln_g'][1]), row1(p['ln_b'][1]), p['ones_bd'])


def _trunk_layer(x, mod, mod_of_tile, s0, batch, seq_len, latent, want_state, p):
    tm = 256
    x1 = _ffn(x, mod, mod_of_tile, p['ffn_in'][0], p['ffn_out'][0], p['ln_g'][0], p['ln_b'][0], 0, 512)
    ps = _proj(x1, mod, mod_of_tile, p['w_mix_shift'], 3, 512, 1152)
    pool_in = _proj(x1, mod, mod_of_tile, p['w_mix_pool'], 3, 512, D_B)
    gates = _proj(x1, mod, mod_of_tile, p['w_mix_gate'], 3, 512, 1024)
    r, v, kk, lw, kd, b, bonus, g = _prep(ps, p, seq_len, latent, tm)
    outs = _scan(r, v, kk, lw, kd, b, s0, batch, seq_len, want_state)
    y = outs[0]
    x2 = _post(y, bonus, g, pool_in, gates, x1, mod, mod_of_tile, p, seq_len, tm)
    x3 = _ffn(x2, mod, mod_of_tile, p['ffn_in'][1], p['ffn_out'][1], p['ln_g'][2], p['ln_b'][2], 6, 512)
    return x3, (outs[1] if want_state else None)


def _block_diag2(w):
    z = jnp.zeros_like(w[0])
    return jnp.concatenate([jnp.concatenate([w[0], z], axis=1), jnp.concatenate([z, w[1]], axis=1)], axis=0)


def kernel(x_prompt, x_sample, c, state_rwkv, c_ctx, w_mod, b_mod, ln_g, ln_b, ffn_in, ffn_out, w_mix_in,
           mu_shift, w0, w_up, a0, a_up, g_up, k_k, k_a, r_k, lnx_g, lnx_b, w_o_rwkv, w_pool, pool_scale,
           w_o_pool, w_out):
    batch, seq, _ = x_prompt.shape
    dec_batch, dec_seq, _ = x_sample.shape
    y_p = x_prompt.reshape(batch * seq, D_MODEL)
    y_s = x_sample.reshape(dec_batch * dec_seq, D_MODEL)
    cvecs = jnp.concatenate([c_ctx[None, :], c, jnp.zeros((8 - 1 - dec_batch, D_MODEL), F32)], axis=0)
    ii = lax.broadcasted_iota(jnp.int32, (LANES, LANES), 0) // HEAD_DIM
    jj = lax.broadcasted_iota(jnp.int32, (LANES, LANES), 1) // HEAD_DIM
    ones_bd = (ii == jj).astype(BF16)
    ctx_states = []
    for l in range(DEPTH):
        wm = w_mix_in[l]
        p = {
            'ln_g': ln_g[l], 'ln_b': ln_b[l],
            'ffn_in': ffn_in[l].astype(BF16), 'ffn_out': ffn_out[l].astype(BF16),
            'w_mix_shift': wm[:, :SHIFT_W].astype(BF16),
            'w_mix_pool': wm[:, SHIFT_W:SHIFT_W + D_B].astype(BF16),
            'w_mix_gate': wm[:, SHIFT_W + D_B:].astype(BF16),
            'mu_shift': mu_shift[l], 'w0': w0[l], 'a0': a0[l],
            'wup_bd': _block_diag2(w_up[l]).astype(BF16), 'aup_bd': _block_diag2(a_up[l]).astype(BF16),
            'g_up': g_up[l].astype(BF16), 'k_k': k_k[l], 'k_a': k_a[l], 'r_k': r_k[l],
            'lnx_g': lnx_g[l], 'lnx_b': lnx_b[l], 'w_o_rwkv': w_o_rwkv[l].astype(BF16),
            'w_pool': w_pool[l].astype(BF16), 'pool_scale': pool_scale[l],
            'w_o_pool': w_o_pool[l].astype(BF16), 'w_out': w_out[l].astype(BF16), 'ones_bd': ones_bd,
        }
        mod = _modulation(cvecs, w_mod[l], b_mod[l]).reshape(8, N_MOD, D_MODEL)
        y_p, s_ctx = _trunk_layer(y_p, mod, lambda i, tm: 0, None, batch, seq, False, True, p)
        ctx_states.append(s_ctx)
        y_s, _ = _trunk_layer(y_s, mod, lambda i, tm: 1 + (i * tm) // dec_seq, state_rwkv[:, l],
                              dec_batch, dec_seq, True, False, p)
    new_state = jnp.stack(ctx_states, axis=1).astype(x_prompt.dtype)
    return (y_p.reshape(batch, seq, D_MODEL), y_s.reshape(dec_batch, dec_seq, D_MODEL), new_state)
```

```python
import functools
import math

import jax
import jax.numpy as jnp
from jax import lax
from jax.experimental import pallas as pl
from jax.experimental.pallas import tpu as pltpu

F32 = jnp.float32
BF16 = jnp.bfloat16

D_MODEL = 1024
DEPTH = 1
GRID_W = 64
HEAD_DIM = 64
D_A = D_MODEL
N_HEADS = D_A // HEAD_DIM
D_B = D_MODEL // 2
POOL_WINDOWS = (2, 4, 8, 16)
POOL_GROUP_DIM = D_B // len(POOL_WINDOWS)
LORA_W = 64
LORA_A = 64
LORA_G = 128
D_FF = 2816
N_MOD = 9
SHIFT_W = 3 * D_A + 2 * LORA_W + 2 * LORA_A + LORA_G
ALPHA = (2 * DEPTH) ** 0.25
LN_EPS = 1e-5
GN_EPS = 64e-5
DECAY_SCALE = math.exp(-0.5)

LANES = 128
HALO = 64
POOL_HALO = 8
CHUNK = 64
PAIR = 2 * HEAD_DIM
VMEM_LIMIT = 56 * 1024 * 1024


def _cparams(sem):
    return pltpu.CompilerParams(dimension_semantics=sem, vmem_limit_bytes=VMEM_LIMIT)


def _mod_p2(x, n):
    assert n & (n - 1) == 0
    return x & (n - 1)


def _div_p2(x, n):
    assert n & (n - 1) == 0
    return x >> (n.bit_length() - 1)


def _sigmoid(x):
    return 1.0 / (1.0 + jnp.exp(-x))


def _silu(x):
    return x / (1.0 + jnp.exp(-x))


def _dot(a, b):
    return jnp.dot(a.astype(BF16), b.astype(BF16), preferred_element_type=F32)


def _split2(x):
    hi = x.astype(BF16)
    lo = (x - hi.astype(F32)).astype(BF16)
    return hi, lo


def _split3(x):
    hi = x.astype(BF16)
    r1 = x - hi.astype(F32)
    mid = r1.astype(BF16)
    lo = (r1 - mid.astype(F32)).astype(BF16)
    return hi, mid, lo


_NN = (((1,), (0,)), ((), ()))
_NT = (((1,), (1,)), ((), ()))
_TN = (((0,), (0,)), ((), ()))


def _dg(a, b, dims):
    return lax.dot_general(a, b, dims, preferred_element_type=F32)


def _mm3(a, b, dims=_NN):
    ah, al = _split2(a)
    bh, bl = _split2(b)
    return _dg(ah, bh, dims) + (_dg(ah, bl, dims) + _dg(al, bh, dims))


def _layer_norm(z, g, b):
    mu = jnp.mean(z, axis=-1, keepdims=True)
    d = z - mu
    var = jnp.mean(d * d, axis=-1, keepdims=True)
    return d * lax.rsqrt(var + LN_EPS) * g + b


def _head_sum(x, ones_bd):
    hi, lo = _split2(x)
    outs = []
    for c in range(x.shape[1] // LANES):
        sl = slice(c * LANES, (c + 1) * LANES)
        outs.append(jnp.dot(hi[:, sl], ones_bd, preferred_element_type=F32)
                    + jnp.dot(lo[:, sl], ones_bd, preferred_element_type=F32))
    return jnp.concatenate(outs, axis=1)


def _mod_kernel(c_ref, w_ref, b_ref, o_ref):
    o_ref[...] = _dot(_silu(c_ref[...]), w_ref[...]) + b_ref[...]


def _modulation(cvecs, w_mod, b_mod):
    rows = cvecs.shape[0]
    n = w_mod.shape[1]
    tn = 1024
    return pl.pallas_call(
        _mod_kernel,
        grid=(n // tn,),
        in_specs=[pl.BlockSpec((rows, D_MODEL), lambda j: (0, 0)),
                  pl.BlockSpec((D_MODEL, tn), lambda j: (0, j)),
                  pl.BlockSpec((1, tn), lambda j: (0, j))],
        out_specs=pl.BlockSpec((rows, tn), lambda j: (0, j)),
        out_shape=jax.ShapeDtypeStruct((rows, n), F32),
        compiler_params=_cparams(("arbitrary",)),
        name="modulation",
    )(cvecs, w_mod, b_mod.reshape(1, n))


def _ffn_kernel(x_ref, mod_ref, wi_ref, wo_ref, lng_ref, lnb_ref, o_ref, *, mi):
    x = x_ref[...]
    h = (x * (1.0 + mod_ref[mi + 1:mi + 2, :]) + mod_ref[mi:mi + 1, :]).astype(BF16)
    gate = jnp.dot(h, wi_ref[:, :D_FF], preferred_element_type=F32)
    up = jnp.dot(h, wi_ref[:, D_FF:], preferred_element_type=F32)
    ff = _dot(_silu(gate) * up, wo_ref[...])
    z = ALPHA * x + 0.5 * mod_ref[mi + 2:mi + 3, :] * ff
    o_ref[...] = _layer_norm(z, lng_ref[...], lnb_ref[...])


def _ffn(x, mod, mod_of_tile, w_in, w_out, ln_g, ln_b, mi, tm):
    rows = x.shape[0]
    resident = lambda shape: pl.BlockSpec(shape, lambda i: (0,) * len(shape), pipeline_mode=pl.Buffered(1))
    return pl.pallas_call(
        functools.partial(_ffn_kernel, mi=mi),
        grid=(rows // tm,),
        in_specs=[pl.BlockSpec((tm, D_MODEL), lambda i: (i, 0)),
                  pl.BlockSpec((None, N_MOD, D_MODEL), lambda i: (mod_of_tile(i, tm), 0, 0)),
                  resident((D_MODEL, 2 * D_FF)), resident((D_FF, D_MODEL)),
                  resident((1, D_MODEL)), resident((1, D_MODEL))],
        out_specs=pl.BlockSpec((tm, D_MODEL), lambda i: (i, 0)),
        out_shape=jax.ShapeDtypeStruct((rows, D_MODEL), F32),
        compiler_params=_cparams(("parallel",)),
        name="ffn",
    )(x, mod, w_in, w_out, ln_g.reshape(1, -1), ln_b.reshape(1, -1))


def _proj_kernel(x_ref, mod_ref, w_ref, o_ref, *, mi):
    shift = mod_ref[mi:mi + 1, :]
    scale = mod_ref[mi + 1:mi + 2, :]
    h = (x_ref[...] * (1.0 + scale) + shift).astype(BF16)
    o_ref[...] = jnp.dot(h, w_ref[...], preferred_element_type=F32)


def _proj(x, mod, mod_of_tile, w, mi, tm, tn):
    rows = x.shape[0]
    n = w.shape[1]
    return pl.pallas_call(
        functools.partial(_proj_kernel, mi=mi),
        grid=(n // tn, rows // tm),
        in_specs=[pl.BlockSpec((tm, D_MODEL), lambda j, i: (i, 0)),
                  pl.BlockSpec((None, N_MOD, D_MODEL), lambda j, i: (mod_of_tile(i, tm), 0, 0)),
                  pl.BlockSpec((D_MODEL, tn), lambda j, i: (0, j))],
        out_specs=pl.BlockSpec((tm, tn), lambda j, i: (i, j)),
        out_shape=jax.ShapeDtypeStruct((rows, n), F32),
        compiler_params=_cparams(("parallel", "parallel")),
        name="mix_in_proj",
    )(x, mod, w)


def _prep_kernel(cur_ref, prev_ref, next_ref, mu_ref, wup_ref, aup_ref, gup_ref, w0_ref, a0_ref,
                 kk_ref, ka_ref, rk_ref, ones_ref,
                 r_o, v_o, kk_o, lw_o, kd_o, b_o, bonus_o, g_o, ext_scr, *, seq_len, latent, tm):
    i = pl.program_id(0)
    ext_scr[0:HALO, :] = prev_ref[...]
    ext_scr[HALO:HALO + tm, :] = cur_ref[...]
    ext_scr[HALO + tm:HALO + tm + HALO, :] = next_ref[...]

    t = _mod_p2(i * tm + lax.broadcasted_iota(jnp.int32, (tm, LANES), 0), seq_len)
    lane = lax.broadcasted_iota(jnp.int32, (tm, LANES), 1)

    def mixed_cols(c):
        sl = slice(c * LANES, (c + 1) * LANES)
        cur = ext_scr[HALO:HALO + tm, sl]
        before = ext_scr[HALO - 1:HALO - 1 + tm, sl]
        after = ext_scr[HALO + 1:HALO + 1 + tm, sl]
        if latent:
            col = _mod_p2(t, GRID_W)
            up = ext_scr[0:tm, sl]
            down = ext_scr[2 * HALO:2 * HALO + tm, sl]
            which = _mod_p2(lane, 4)
            shifted = jnp.where(
                which == 0, jnp.where(col > 0, before, 0.0),
                jnp.where(which == 1, jnp.where(col < GRID_W - 1, after, 0.0),
                          jnp.where(which == 2, jnp.where(t >= GRID_W, up, 0.0),
                                    jnp.where(t < seq_len - GRID_W, down, 0.0))))
        else:
            shifted = jnp.where(_mod_p2(lane, 2) == 0, jnp.where(t > 0, before, 0.0),
                                jnp.where(t < seq_len - 1, after, 0.0))
        return cur + mu_ref[:, sl] * (shifted - cur)

    def mixed(c0, c1):
        return jnp.concatenate([mixed_cols(c) for c in range(c0, c1)], axis=1)

    nb = D_A // LANES
    r = mixed(0, nb)
    k = mixed(nb, 2 * nb)
    v = mixed(2 * nb, 3 * nb)
    w_down = mixed_cols(3 * nb)
    a_down = mixed_cols(3 * nb + 1)
    g_down = mixed_cols(3 * nb + 2)

    ones_bd = ones_ref[...]
    r_o[...] = r
    v_o[...] = v
    kkraw = k * kk_ref[...]
    ss = _head_sum(kkraw * kkraw, ones_bd)
    kk = kkraw / jnp.maximum(jnp.sqrt(ss), 1e-12)
    kk_o[...] = kk
    bonus_o[...] = _head_sum(r * k * rk_ref[...], ones_bd) * v
    g_o[...] = _dot(_sigmoid(g_down), gup_ref[...])

    w_raw = _dot(jnp.tanh(w_down), wup_ref[...])
    a_raw = _dot(a_down, aup_ref[...])
    for d in range(2):
        sl = slice(d * D_A, (d + 1) * D_A)
        lw_o[d] = -DECAY_SCALE * _sigmoid(w0_ref[d:d + 1, :] + w_raw[:, sl])
        a = _sigmoid(a0_ref[d:d + 1, :] + a_raw[:, sl])
        kd_o[d] = k * (1.0 + (a - 1.0) * ka_ref[...])
        b_o[d] = kk * a


def _prep(ps, p, seq_len, latent, tm):
    rows = ps.shape[0]
    hb = tm // HALO
    n_halo_blocks = rows // HALO
    row1 = lambda a: a.reshape(1, -1)
    full = lambda shape: pl.BlockSpec(shape, lambda i: (0,) * len(shape))
    out_rows = pl.BlockSpec((tm, D_A), lambda i: (i, 0))
    out_dir = pl.BlockSpec((2, tm, D_A), lambda i: (0, i, 0))
    sds = jax.ShapeDtypeStruct
    return pl.pallas_call(
        functools.partial(_prep_kernel, seq_len=seq_len, latent=latent, tm=tm),
        grid=(rows // tm,),
        in_specs=[pl.BlockSpec((tm, SHIFT_W), lambda i: (i, 0)),
                  pl.BlockSpec((HALO, SHIFT_W), lambda i: (jnp.maximum(i * hb - 1, 0), 0)),
                  pl.BlockSpec((HALO, SHIFT_W), lambda i: (jnp.minimum((i + 1) * hb, n_halo_blocks - 1), 0)),
                  full((1, SHIFT_W)), full((LANES, 2 * D_A)), full((LANES, 2 * D_A)), full((LORA_G, D_A)),
                  full((2, D_A)), full((2, D_A)), full((1, D_A)), full((1, D_A)), full((1, D_A)),
                  full((LANES, LANES))],
        out_specs=[out_rows, out_rows, out_rows, out_dir, out_dir, out_dir, out_rows, out_rows],
        out_shape=[sds((rows, D_A), F32), sds((rows, D_A), F32), sds((rows, D_A), F32),
                   sds((2, rows, D_A), F32), sds((2, rows, D_A), F32), sds((2, rows, D_A), F32),
                   sds((rows, D_A), F32), sds((rows, D_A), F32)],
        scratch_shapes=[pltpu.VMEM((tm + 2 * HALO, SHIFT_W), F32)],
        compiler_params=_cparams(("parallel",)),
        name="shift_prep",
    )(ps, ps, ps, row1(p['mu_shift']), p['wup_bd'], p['aup_bd'], p['g_up'], p['w0'], p['a0'],
      row1(p['k_k']), row1(p['k_a']), row1(p['r_k']), p['ones_bd'])


SCAN_SPLITS = {
    'gram': (1, 1), 'akv': (1, 1), 'apply': (1, 1), 'pkv': (1, 1), 'pbq': (1, 1),
    'inv_a': (1, 1), 'inv_b': (1, 1),
    'trans': (1, 1), 'sadd': (2, 2), 'y_state': (1, 1), 's_state': (1, 1),
}
SCAN_GROUP = 8
SCAN_SEQS = 2


def _parts(x, n):
    return (x.astype(BF16),) if n == 1 else _split2(x)


def _mmp(ap, bp, dims=_NN):
    out = _dg(ap[0], bp[0], dims)
    if len(bp) > 1:
        out = out + _dg(ap[0], bp[1], dims)
    if len(ap) > 1:
        out = out + _dg(ap[1], bp[0], dims)
    return out


def _mms(site, a, b, dims=_NN):
    na, nb = SCAN_SPLITS[site]
    return _mmp(_parts(a, na), _parts(b, nb), dims)


def _scan_kernel(*refs, has_s0, want_state, group, nseq):
    if has_s0:
        (r_ref, v_ref, kk_ref, lw_ref, kd_ref, b_ref, s0_ref), rest = refs[:7], refs[7:]
    else:
        (r_ref, v_ref, kk_ref, lw_ref, kd_ref, b_ref), rest = refs[:6], refs[6:]
    if want_state:
        y_ref, sout_ref, s_scr = rest
    else:
        y_ref, s_scr = rest
    d = pl.program_id(0)
    c = pl.program_id(3)
    C = CHUNK

    chains = [(bi, p) for bi in range(nseq) for p in range(group)]
    keys = [(bi, slice(p * PAIR, (p + 1) * PAIR)) for bi, p in chains]

    @pl.when(c == 0)
    def _():
        if has_s0:
            z = jnp.zeros((HEAD_DIM, HEAD_DIM), F32)
            for j, (bi, p) in enumerate(chains):
                top = jnp.concatenate([s0_ref[bi, 2 * p], z], axis=1)
                bot = jnp.concatenate([z, s0_ref[bi, 2 * p + 1]], axis=1)
                s_scr[j] = jnp.concatenate([top, bot], axis=0)
        else:
            s_scr[...] = jnp.zeros_like(s_scr)

    row = lax.broadcasted_iota(jnp.int32, (PAIR, PAIR), 0)
    col = lax.broadcasted_iota(jnp.int32, (PAIR, PAIR), 1)
    same_head = _div_p2(row, HEAD_DIM) == _div_p2(col, HEAD_DIM)
    sign = 1 - 2 * d
    order = (_mod_p2(row, C) - _mod_p2(col, C)) * sign
    strict = (order > 0) & same_head
    incl = (order >= 0)[:C, :]
    tri = jnp.where(incl[:, :C], 1.0, 0.0).astype(BF16)
    head0 = lax.broadcasted_iota(jnp.int32, (C, PAIR), 1) < HEAD_DIM

    def stack(z):
        return jnp.concatenate([jnp.where(head0, z, 0.0), jnp.where(head0, 0.0, z)], axis=0)

    def each(fn, *lists):
        return [fn(*args) for args in zip(*lists)]

    def cumsum(lw):
        l1, l2, l3 = _split3(lw)
        return (jnp.dot(tri, l1, preferred_element_type=F32)
                + (jnp.dot(tri, l2, preferred_element_type=F32) + jnp.dot(tri, l3, preferred_element_type=F32)))

    def rd(ref, key):
        return ref[key[0], :, key[1]]

    lw = [rd(lw_ref, k) for k in keys]
    cs = each(cumsum, lw)
    tot = [jnp.sum(z, axis=0, keepdims=True) for z in lw]
    w_inv = [jnp.exp(-z) for z in cs]
    w_rest = each(lambda t, z: jnp.exp(t - z), tot, cs)
    r_t = each(lambda k, z: rd(r_ref, k) * jnp.exp(z), keys, cs)
    kk_st = each(lambda k, z, l: stack(rd(kk_ref, k) * jnp.exp(z - l)), keys, cs, lw)
    v_st = [stack(rd(v_ref, k)) for k in keys]
    kb_st = each(lambda k, wi: jnp.concatenate([stack(rd(kd_ref, k) * wi), stack(rd(b_ref, k) * wi)], axis=0),
                 keys, w_inv)
    m = each(lambda a, rt, kb: _mms('gram', jnp.concatenate([a, rt], axis=0), kb, _NT), kk_st, r_t, kb_st)
    a_k = [jnp.where(strict, z[:2 * C, :2 * C], 0.0) for z in m]
    x = [jnp.where(strict, -z[:2 * C, 2 * C:], 0.0) for z in m]
    p_k = [jnp.where(incl, z[2 * C:, :2 * C], 0.0) for z in m]
    p_b = [jnp.where(incl, z[2 * C:, 2 * C:], 0.0) for z in m]

    ti, tj = _mod_p2(row, C), _mod_p2(col, C)
    eye = jnp.where(row == col, 1.0, 0.0)

    def off_block(n):
        return (_div_p2(ti, 2 * n) == _div_p2(tj, 2 * n)) & (_div_p2(ti, n) != _div_p2(tj, n))

    t_inv = [eye + jnp.where(off_block(1), z, 0.0) for z in x]
    n = 2
    while n < C:
        keep = off_block(n)
        mid = each(lambda z, t: _mms('inv_a', jnp.where(keep, z, 0.0), t), x, t_inv)
        t_inv = each(lambda t, mm: t + _mms('inv_b', t, mm), t_inv, mid)
        n *= 2
    q = each(lambda t, a, ak, vs: _mms('apply', t, jnp.concatenate([a, _mms('akv', ak, vs)], axis=1)),
             t_inv, kk_st, a_k, v_st)

    ry = each(lambda rt, pk, vs, pb, qi: jnp.concatenate([rt, _mms('pkv', pk, vs)], axis=1) - _mms('pbq', pb, qi),
              r_t, p_k, v_st, p_b, q)
    gq = each(lambda qi, k, wr: _mms('trans', qi, stack(rd(b_ref, k) * wr), _TN), q, keys, w_rest)
    s_add = each(lambda vs, k, wr, g: _mms('sadd', vs, stack(rd(kd_ref, k) * wr), _TN) - g[PAIR:],
                 v_st, keys, w_rest, gq)

    s_old = [s_scr[j] for j in range(len(chains))]
    for j, (bi, sl) in enumerate(keys):
        y_ref[bi, :, sl] = _mms('y_state', ry[j][:, :PAIR], s_old[j], _NT) + ry[j][:, PAIR:]
    s_new = each(lambda s, t, g, sa: s * jnp.exp(t) - _mms('s_state', s, g[:PAIR]) + sa, s_old, tot, gq, s_add)
    for j in range(len(chains)):
        s_scr[j] = s_new[j]

    if want_state:
        @pl.when(c == pl.num_programs(3) - 1)
        def _():
            for j, (bi, p) in enumerate(chains):
                sout_ref[bi, 2 * p] = s_new[j][:HEAD_DIM, :HEAD_DIM]
                sout_ref[bi, 2 * p + 1] = s_new[j][HEAD_DIM:, HEAD_DIM:]


def _scan(r, v, kk, lw, kd, b, s0, batch, seq_len, want_state):
    rows = r.shape[0]
    nc = seq_len // CHUNK
    group = SCAN_GROUP
    nseq = SCAN_SEQS
    n_groups = D_A // (PAIR * group)
    gw = group * PAIR

    def chunk_idx(d, c):
        return c + d * (nc - 1 - 2 * c)

    shared = pl.BlockSpec((nseq, CHUNK, gw), lambda d, bi, p, c: (bi, chunk_idx(d, c), p))
    per_dir = pl.BlockSpec((None, nseq, CHUNK, gw), lambda d, bi, p, c: (d, bi, chunk_idx(d, c), p))
    state = pl.BlockSpec((nseq, None, 2 * group, HEAD_DIM, HEAD_DIM), lambda d, bi, p, c: (bi, d, p, 0, 0))
    in_specs = [shared, shared, shared, per_dir, per_dir, per_dir]
    seqs = lambda a: a.reshape(a.shape[:-2] + (batch, seq_len, D_A))
    args = [seqs(a) for a in (r, v, kk, lw, kd, b)]
    if s0 is not None:
        in_specs.append(state)
        args.append(s0)
    out_specs = [per_dir]
    out_shape = [jax.ShapeDtypeStruct((2, batch, seq_len, D_A), F32)]
    if want_state:
        out_specs.append(state)
        out_shape.append(jax.ShapeDtypeStruct((batch, 2, N_HEADS, HEAD_DIM, HEAD_DIM), F32))
    outs = pl.pallas_call(
        functools.partial(_scan_kernel, has_s0=s0 is not None, want_state=want_state, group=group, nseq=nseq),
        grid=(2, batch // nseq, n_groups, nc),
        in_specs=in_specs,
        out_specs=out_specs,
        out_shape=out_shape,
        scratch_shapes=[pltpu.VMEM((nseq * group, PAIR, PAIR), F32)],
        compiler_params=_cparams(("parallel", "parallel", "parallel", "arbitrary")),
        name="rwkv7_scan",
    )(*args)
    return [outs[0].reshape(2, rows, D_A)] + list(outs[1:])


def _post_kernel(y_ref, bonus_ref, g_ref, pool_ref, pprev_ref, pnext_ref, gates_ref, x_ref, mod_ref,
                 lnxg_ref, lnxb_ref, worwkv_ref, wpool_ref, pscale_ref, wopool_ref, wout_ref,
                 lng_ref, lnb_ref, ones_ref, o_ref, ext_scr, *, seq_len, tm):
    i = pl.program_id(0)
    ones_bd = ones_ref[...]
    y = y_ref[0] + y_ref[1]
    mu = _head_sum(y, ones_bd) * (1.0 / HEAD_DIM)
    dlt = y - mu
    var = _head_sum(dlt * dlt, ones_bd) * (1.0 / HEAD_DIM)
    yn = dlt * lax.rsqrt(var + GN_EPS) * lnxg_ref[...] + lnxb_ref[...]
    ya = _dot((yn + bonus_ref[...]) * g_ref[...], worwkv_ref[...])

    ext_scr[0:POOL_HALO, :] = pprev_ref[...]
    ext_scr[POOL_HALO:POOL_HALO + tm, :] = pool_ref[...]
    ext_scr[POOL_HALO + tm:POOL_HALO + tm + POOL_HALO, :] = pnext_ref[...]
    t = _mod_p2(i * tm + lax.broadcasted_iota(jnp.int32, (tm, LANES), 0), seq_len)
    us = []
    for gi, w in enumerate(POOL_WINDOWS):
        sl = slice(gi * POOL_GROUP_DIM, (gi + 1) * POOL_GROUP_DIM)
        acc = jnp.zeros((tm, POOL_GROUP_DIM), F32)
        for off in range(-(w // 2), w - w // 2):
            rows = ext_scr[POOL_HALO + off:POOL_HALO + off + tm, sl]
            valid = (t + off >= 0) & (t + off < seq_len)
            acc = acc + jnp.where(valid, rows, 0.0)
        cnt = jnp.minimum(t + (w - w // 2), seq_len) - jnp.maximum(t - w // 2, 0)
        resid = acc / cnt.astype(F32) - ext_scr[POOL_HALO:POOL_HALO + tm, sl]
        us.append(_dot(resid, wpool_ref[gi]))
    u = jnp.concatenate(us, axis=1) * pscale_ref[...]
    yb = _dot(u, wopool_ref[...])

    gates = gates_ref[...]
    merged = _sigmoid(gates[:, :D_MODEL]) * ya + _sigmoid(gates[:, D_MODEL:]) * yb
    mix = _dot(merged, wout_ref[...])
    z = ALPHA * x_ref[...] + mod_ref[5:6, :] * mix
    o_ref[...] = _layer_norm(z, lng_ref[...], lnb_ref[...])


def _post(y, bonus, g, pool_in, gates, x, mod, mod_of_tile, p, seq_len, tm):
    rows = x.shape[0]
    hb = tm // POOL_HALO
    n_halo_blocks = rows // POOL_HALO
    row1 = lambda a: a.reshape(1, -1)
    full = lambda shape: pl.BlockSpec(shape, lambda i: (0,) * len(shape))
    tile = lambda width: pl.BlockSpec((tm, width), lambda i: (i, 0))
    return pl.pallas_call(
        functools.partial(_post_kernel, seq_len=seq_len, tm=tm),
        grid=(rows // tm,),
        in_specs=[pl.BlockSpec((2, tm, D_A), lambda i: (0, i, 0)), tile(D_A), tile(D_A), tile(D_B),
                  pl.BlockSpec((POOL_HALO, D_B), lambda i: (jnp.maximum(i * hb - 1, 0), 0)),
                  pl.BlockSpec((POOL_HALO, D_B), lambda i: (jnp.minimum((i + 1) * hb, n_halo_blocks - 1), 0)),
                  tile(2 * D_MODEL), tile(D_MODEL),
                  pl.BlockSpec((None, N_MOD, D_MODEL), lambda i: (mod_of_tile(i, tm), 0, 0)),
                  full((1, D_A)), full((1, D_A)), full((D_A, D_MODEL)),
                  full((len(POOL_WINDOWS), POOL_GROUP_DIM, POOL_GROUP_DIM)), full((1, D_B)),
                  full((D_B, D_MODEL)), full((D_MODEL, D_MODEL)), full((1, D_MODEL)), full((1, D_MODEL)),
                  full((LANES, LANES))],
        out_specs=tile(D_MODEL),
        out_shape=jax.ShapeDtypeStruct((rows, D_MODEL), F32),
        scratch_shapes=[pltpu.VMEM((tm + 2 * POOL_HALO, D_B), F32)],
        compiler_params=_cparams(("parallel",)),
        name="mixer_out",
    )(y, bonus, g, pool_in, pool_in, pool_in, gates, x, mod, row1(p['lnx_g']), row1(p['lnx_b']),
      p['w_o_rwkv'], p['w_pool'], row1(p['pool_scale']), p['w_o_pool'], p['w_out'],
      row1(p['ln_g'][1]), row1(p['ln_b'][1]), p['ones_bd'])


def _trunk_layer(x, mod, mod_of_tile, s0, batch, seq_len, latent, want_state, p):
    tm = 256
    x1 = _ffn(x, mod, mod_of_tile, p['ffn_in'][0], p['ffn_out'][0], p['ln_g'][0], p['ln_b'][0], 0, 512)
    ps = _proj(x1, mod, mod_of_tile, p['w_mix_shift'], 3, 512, 1152)
    pool_in = _proj(x1, mod, mod_of_tile, p['w_mix_pool'], 3, 512, D_B)
    gates = _proj(x1, mod, mod_of_tile, p['w_mix_gate'], 3, 512, 1024)
    r, v, kk, lw, kd, b, bonus, g = _prep(ps, p, seq_len, latent, tm)
    outs = _scan(r, v, kk, lw, kd, b, s0, batch, seq_len, want_state)
    y = outs[0]
    x2 = _post(y, bonus, g, pool_in, gates, x1, mod, mod_of_tile, p, seq_len, tm)
    x3 = _ffn(x2, mod, mod_of_tile, p['ffn_in'][1], p['ffn_out'][1], p['ln_g'][2], p['ln_b'][2], 6, 512)
    return x3, (outs[1] if want_state else None)


def _block_diag2(w):
    z = jnp.zeros_like(w[0])
    return jnp.concatenate([jnp.concatenate([w[0], z], axis=1), jnp.concatenate([z, w[1]], axis=1)], axis=0)


def kernel(x_prompt, x_sample, c, state_rwkv, c_ctx, w_mod, b_mod, ln_g, ln_b, ffn_in, ffn_out, w_mix_in,
           mu_shift, w0, w_up, a0, a_up, g_up, k_k, k_a, r_k, lnx_g, lnx_b, w_o_rwkv, w_pool, pool_scale,
           w_o_pool, w_out):
    batch, seq, _ = x_prompt.shape
    dec_batch, dec_seq, _ = x_sample.shape
    y_p = x_prompt.reshape(batch * seq, D_MODEL)
    y_s = x_sample.reshape(dec_batch * dec_seq, D_MODEL)
    cvecs = jnp.concatenate([c_ctx[None, :], c, jnp.zeros((8 - 1 - dec_batch, D_MODEL), F32)], axis=0)
    ii = lax.broadcasted_iota(jnp.int32, (LANES, LANES), 0) // HEAD_DIM
    jj = lax.broadcasted_iota(jnp.int32, (LANES, LANES), 1) // HEAD_DIM
    ones_bd = (ii == jj).astype(BF16)
    ctx_states = []
    for l in range(DEPTH):
        wm = w_mix_in[l]
        p = {
            'ln_g': ln_g[l], 'ln_b': ln_b[l],
            'ffn_in': ffn_in[l].astype(BF16), 'ffn_out': ffn_out[l].astype(BF16),
            'w_mix_shift': wm[:, :SHIFT_W].astype(BF16),
            'w_mix_pool': wm[:, SHIFT_W:SHIFT_W + D_B].astype(BF16),
            'w_mix_gate': wm[:, SHIFT_W + D_B:].astype(BF16),
            'mu_shift': mu_shift[l], 'w0': w0[l], 'a0': a0[l],
            'wup_bd': _block_diag2(w_up[l]).astype(BF16), 'aup_bd': _block_diag2(a_up[l]).astype(BF16),
            'g_up': g_up[l].astype(BF16), 'k_k': k_k[l], 'k_a': k_a[l], 'r_k': r_k[l],
            'lnx_g': lnx_g[l], 'lnx_b': lnx_b[l], 'w_o_rwkv': w_o_rwkv[l].astype(BF16),
            'w_pool': w_pool[l].astype(BF16), 'pool_scale': pool_scale[l],
            'w_o_pool': w_o_pool[l].astype(BF16), 'w_out': w_out[l].astype(BF16), 'ones_bd': ones_bd,
        }
        mod = _modulation(cvecs, w_mod[l], b_mod[l]).reshape(8, N_MOD, D_MODEL)
        y_p, s_ctx = _trunk_layer(y_p, mod, lambda i, tm: 0, None, batch, seq, False, True, p)
        ctx_states.append(s_ctx)
        y_s, _ = _trunk_layer(y_s, mod, lambda i, tm: 1 + (i * tm) // dec_seq, state_rwkv[:, l],
                              dec_batch, dec_seq, True, False, p)
    new_state = jnp.stack(ctx_states, axis=1).astype(x_prompt.dtype)
    return (y_p.reshape(batch, seq, D_MODEL), y_s.reshape(dec_batch, dec_seq, D_MODEL), new_state)
```

```python
import functools
import math

import jax
import jax.numpy as jnp
from jax import lax
from jax.experimental import pallas as pl
from jax.experimental.pallas import tpu as pltpu

F32 = jnp.float32
BF16 = jnp.bfloat16

D_MODEL = 1024
DEPTH = 1
GRID_W = 64
HEAD_DIM = 64
D_A = D_MODEL
N_HEADS = D_A // HEAD_DIM
D_B = D_MODEL // 2
POOL_WINDOWS = (2, 4, 8, 16)
POOL_GROUP_DIM = D_B // len(POOL_WINDOWS)
LORA_W = 64
LORA_A = 64
LORA_G = 128
D_FF = 2816
N_MOD = 9
SHIFT_W = 3 * D_A + 2 * LORA_W + 2 * LORA_A + LORA_G
ALPHA = (2 * DEPTH) ** 0.25
LN_EPS = 1e-5
GN_EPS = 64e-5
DECAY_SCALE = math.exp(-0.5)

LANES = 128
HALO = 64
POOL_HALO = 8
CHUNK = 64
PAIR = 2 * HEAD_DIM
VMEM_LIMIT = 56 * 1024 * 1024


def _cparams(sem):
    return pltpu.CompilerParams(dimension_semantics=sem, vmem_limit_bytes=VMEM_LIMIT)


def _mod_p2(x, n):
    assert n & (n - 1) == 0
    return x & (n - 1)


def _div_p2(x, n):
    assert n & (n - 1) == 0
    return x >> (n.bit_length() - 1)


def _sigmoid(x):
    return 1.0 / (1.0 + jnp.exp(-x))


def _silu(x):
    return x / (1.0 + jnp.exp(-x))


def _dot(a, b):
    return jnp.dot(a.astype(BF16), b.astype(BF16), preferred_element_type=F32)


def _split2(x):
    hi = x.astype(BF16)
    lo = (x - hi.astype(F32)).astype(BF16)
    return hi, lo


def _split3(x):
    hi = x.astype(BF16)
    r1 = x - hi.astype(F32)
    mid = r1.astype(BF16)
    lo = (r1 - mid.astype(F32)).astype(BF16)
    return hi, mid, lo


_NN = (((1,), (0,)), ((), ()))
_NT = (((1,), (1,)), ((), ()))
_TN = (((0,), (0,)), ((), ()))


def _dg(a, b, dims):
    return lax.dot_general(a, b, dims, preferred_element_type=F32)


def _mm3(a, b, dims=_NN):
    ah, al = _split2(a)
    bh, bl = _split2(b)
    return _dg(ah, bh, dims) + (_dg(ah, bl, dims) + _dg(al, bh, dims))


def _layer_norm(z, g, b):
    mu = jnp.mean(z, axis=-1, keepdims=True)
    d = z - mu
    var = jnp.mean(d * d, axis=-1, keepdims=True)
    return d * lax.rsqrt(var + LN_EPS) * g + b


def _head_sum(x, ones_bd):
    hi, lo = _split2(x)
    outs = []
    for c in range(x.shape[1] // LANES):
        sl = slice(c * LANES, (c + 1) * LANES)
        outs.append(jnp.dot(hi[:, sl], ones_bd, preferred_element_type=F32)
                    + jnp.dot(lo[:, sl], ones_bd, preferred_element_type=F32))
    return jnp.concatenate(outs, axis=1)


def _mod_kernel(c_ref, w_ref, b_ref, o_ref):
    o_ref[...] = _dot(_silu(c_ref[...]), w_ref[...]) + b_ref[...]


def _modulation(cvecs, w_mod, b_mod):
    rows = cvecs.shape[0]
    n = w_mod.shape[1]
    tn = 1024
    return pl.pallas_call(
        _mod_kernel,
        grid=(n // tn,),
        in_specs=[pl.BlockSpec((rows, D_MODEL), lambda j: (0, 0)),
                  pl.BlockSpec((D_MODEL, tn), lambda j: (0, j)),
                  pl.BlockSpec((1, tn), lambda j: (0, j))],
        out_specs=pl.BlockSpec((rows, tn), lambda j: (0, j)),
        out_shape=jax.ShapeDtypeStruct((rows, n), F32),
        compiler_params=_cparams(("arbitrary",)),
        name="modulation",
    )(cvecs, w_mod, b_mod.reshape(1, n))


def _ffn_kernel(x_ref, mod_ref, wi_ref, wo_ref, lng_ref, lnb_ref, o_ref, *, mi):
    x = x_ref[...]
    h = (x * (1.0 + mod_ref[mi + 1:mi + 2, :]) + mod_ref[mi:mi + 1, :]).astype(BF16)
    gate = jnp.dot(h, wi_ref[:, :D_FF], preferred_element_type=F32)
    up = jnp.dot(h, wi_ref[:, D_FF:], preferred_element_type=F32)
    ff = _dot(_silu(gate) * up, wo_ref[...])
    z = ALPHA * x + 0.5 * mod_ref[mi + 2:mi + 3, :] * ff
    o_ref[...] = _layer_norm(z, lng_ref[...], lnb_ref[...])


def _ffn(x, mod, mod_of_tile, w_in, w_out, ln_g, ln_b, mi, tm):
    rows = x.shape[0]
    resident = lambda shape: pl.BlockSpec(shape, lambda i: (0,) * len(shape), pipeline_mode=pl.Buffered(1))
    return pl.pallas_call(
        functools.partial(_ffn_kernel, mi=mi),
        grid=(rows // tm,),
        in_specs=[pl.BlockSpec((tm, D_MODEL), lambda i: (i, 0)),
                  pl.BlockSpec((None, N_MOD, D_MODEL), lambda i: (mod_of_tile(i, tm), 0, 0)),
                  resident((D_MODEL, 2 * D_FF)), resident((D_FF, D_MODEL)),
                  resident((1, D_MODEL)), resident((1, D_MODEL))],
        out_specs=pl.BlockSpec((tm, D_MODEL), lambda i: (i, 0)),
        out_shape=jax.ShapeDtypeStruct((rows, D_MODEL), F32),
        compiler_params=_cparams(("parallel",)),
        name="ffn",
    )(x, mod, w_in, w_out, ln_g.reshape(1, -1), ln_b.reshape(1, -1))


def _proj_kernel(x_ref, mod_ref, w_ref, ps_o, pool_o, gate_o, *, mi):
    shift = mod_ref[mi:mi + 1, :]
    scale = mod_ref[mi + 1:mi + 2, :]
    h = (x_ref[...] * (1.0 + scale) + shift).astype(BF16)
    ps_o[...] = jnp.dot(h, w_ref[:, :SHIFT_W], preferred_element_type=F32)
    pool_o[...] = jnp.dot(h, w_ref[:, SHIFT_W:SHIFT_W + D_B], preferred_element_type=F32)
    gate_o[...] = jnp.dot(h, w_ref[:, SHIFT_W + D_B:], preferred_element_type=F32)


def _proj(x, mod, mod_of_tile, w, mi, tm):
    rows = x.shape[0]
    n = w.shape[1]
    widths = (SHIFT_W, D_B, n - SHIFT_W - D_B)
    return pl.pallas_call(
        functools.partial(_proj_kernel, mi=mi),
        grid=(rows // tm,),
        in_specs=[pl.BlockSpec((tm, D_MODEL), lambda i: (i, 0)),
                  pl.BlockSpec((None, N_MOD, D_MODEL), lambda i: (mod_of_tile(i, tm), 0, 0)),
                  pl.BlockSpec((D_MODEL, n), lambda i: (0, 0), pipeline_mode=pl.Buffered(1))],
        out_specs=[pl.BlockSpec((tm, wd), lambda i: (i, 0)) for wd in widths],
        out_shape=[jax.ShapeDtypeStruct((rows, wd), F32) for wd in widths],
        compiler_params=_cparams(("parallel",)),
        name="mix_in_proj",
    )(x, mod, w)


def _prep_kernel(cur_ref, prev_ref, next_ref, mu_ref, wup_ref, aup_ref, gup_ref, w0_ref, a0_ref,
                 kk_ref, ka_ref, rk_ref, ones_ref,
                 r_o, v_o, kk_o, lw_o, kd_o, b_o, bonus_o, g_o, ext_scr, *, seq_len, latent, tm):
    i = pl.program_id(0)
    ext_scr[0:HALO, :] = prev_ref[...]
    ext_scr[HALO:HALO + tm, :] = cur_ref[...]
    ext_scr[HALO + tm:HALO + tm + HALO, :] = next_ref[...]

    t = _mod_p2(i * tm + lax.broadcasted_iota(jnp.int32, (tm, LANES), 0), seq_len)
    lane = lax.broadcasted_iota(jnp.int32, (tm, LANES), 1)

    def mixed_cols(c):
        sl = slice(c * LANES, (c + 1) * LANES)
        cur = ext_scr[HALO:HALO + tm, sl]
        before = ext_scr[HALO - 1:HALO - 1 + tm, sl]
        after = ext_scr[HALO + 1:HALO + 1 + tm, sl]
        if latent:
            col = _mod_p2(t, GRID_W)
            up = ext_scr[0:tm, sl]
            down = ext_scr[2 * HALO:2 * HALO + tm, sl]
            which = _mod_p2(lane, 4)
            shifted = jnp.where(
                which == 0, jnp.where(col > 0, before, 0.0),
                jnp.where(which == 1, jnp.where(col < GRID_W - 1, after, 0.0),
                          jnp.where(which == 2, jnp.where(t >= GRID_W, up, 0.0),
                                    jnp.where(t < seq_len - GRID_W, down, 0.0))))
        else:
            shifted = jnp.where(_mod_p2(lane, 2) == 0, jnp.where(t > 0, before, 0.0),
                                jnp.where(t < seq_len - 1, after, 0.0))
        return cur + mu_ref[:, sl] * (shifted - cur)

    def mixed(c0, c1):
        return jnp.concatenate([mixed_cols(c) for c in range(c0, c1)], axis=1)

    nb = D_A // LANES
    r = mixed(0, nb)
    k = mixed(nb, 2 * nb)
    v = mixed(2 * nb, 3 * nb)
    w_down = mixed_cols(3 * nb)
    a_down = mixed_cols(3 * nb + 1)
    g_down = mixed_cols(3 * nb + 2)

    ones_bd = ones_ref[...]
    r_o[...] = r
    v_o[...] = v
    kkraw = k * kk_ref[...]
    ss = _head_sum(kkraw * kkraw, ones_bd)
    kk = kkraw / jnp.maximum(jnp.sqrt(ss), 1e-12)
    kk_o[...] = kk
    bonus_o[...] = _head_sum(r * k * rk_ref[...], ones_bd) * v
    g_o[...] = _dot(_sigmoid(g_down), gup_ref[...])

    w_raw = _dot(jnp.tanh(w_down), wup_ref[...])
    a_raw = _dot(a_down, aup_ref[...])
    for d in range(2):
        sl = slice(d * D_A, (d + 1) * D_A)
        lw_o[d] = -DECAY_SCALE * _sigmoid(w0_ref[d:d + 1, :] + w_raw[:, sl])
        a = _sigmoid(a0_ref[d:d + 1, :] + a_raw[:, sl])
        kd_o[d] = k * (1.0 + (a - 1.0) * ka_ref[...])
        b_o[d] = kk * a


def _prep(ps, p, seq_len, latent, tm):
    rows = ps.shape[0]
    hb = tm // HALO
    n_halo_blocks = rows // HALO
    row1 = lambda a: a.reshape(1, -1)
    full = lambda shape: pl.BlockSpec(shape, lambda i: (0,) * len(shape))
    out_rows = pl.BlockSpec((tm, D_A), lambda i: (i, 0))
    out_dir = pl.BlockSpec((2, tm, D_A), lambda i: (0, i, 0))
    sds = jax.ShapeDtypeStruct
    return pl.pallas_call(
        functools.partial(_prep_kernel, seq_len=seq_len, latent=latent, tm=tm),
        grid=(rows // tm,),
        in_specs=[pl.BlockSpec((tm, SHIFT_W), lambda i: (i, 0)),
                  pl.BlockSpec((HALO, SHIFT_W), lambda i: (jnp.maximum(i * hb - 1, 0), 0)),
                  pl.BlockSpec((HALO, SHIFT_W), lambda i: (jnp.minimum((i + 1) * hb, n_halo_blocks - 1), 0)),
                  full((1, SHIFT_W)), full((LANES, 2 * D_A)), full((LANES, 2 * D_A)), full((LORA_G, D_A)),
                  full((2, D_A)), full((2, D_A)), full((1, D_A)), full((1, D_A)), full((1, D_A)),
                  full((LANES, LANES))],
        out_specs=[out_rows, out_rows, out_rows, out_dir, out_dir, out_dir, out_rows, out_rows],
        out_shape=[sds((rows, D_A), F32), sds((rows, D_A), F32), sds((rows, D_A), F32),
                   sds((2, rows, D_A), F32), sds((2, rows, D_A), F32), sds((2, rows, D_A), F32),
                   sds((rows, D_A), F32), sds((rows, D_A), F32)],
        scratch_shapes=[pltpu.VMEM((tm + 2 * HALO, SHIFT_W), F32)],
        compiler_params=_cparams(("parallel",)),
        name="shift_prep",
    )(ps, ps, ps, row1(p['mu_shift']), p['wup_bd'], p['aup_bd'], p['g_up'], p['w0'], p['a0'],
      row1(p['k_k']), row1(p['k_a']), row1(p['r_k']), p['ones_bd'])


SCAN_SPLITS = {
    'gram': (1, 1), 'akv': (1, 1), 'apply': (1, 1), 'pbq': (1, 1),
    'inv_a': (1, 1), 'inv_b': (1, 1),
    'trans': (1, 1), 'sadd': (2, 2), 'y_state': (1, 1), 's_state': (1, 1),
}
SCAN_GROUP = 8
SCAN_SEQS = 2


def _parts(x, n):
    return (x.astype(BF16),) if n == 1 else _split2(x)


def _mmp(ap, bp, dims=_NN):
    out = _dg(ap[0], bp[0], dims)
    if len(bp) > 1:
        out = out + _dg(ap[0], bp[1], dims)
    if len(ap) > 1:
        out = out + _dg(ap[1], bp[0], dims)
    return out


def _mms(site, a, b, dims=_NN):
    na, nb = SCAN_SPLITS[site]
    return _mmp(_parts(a, na), _parts(b, nb), dims)


def _scan_kernel(*refs, has_s0, want_state, group, nseq):
    if has_s0:
        (r_ref, v_ref, kk_ref, lw_ref, kd_ref, b_ref, s0_ref), rest = refs[:7], refs[7:]
    else:
        (r_ref, v_ref, kk_ref, lw_ref, kd_ref, b_ref), rest = refs[:6], refs[6:]
    if want_state:
        y_ref, sout_ref, s_scr = rest
    else:
        y_ref, s_scr = rest
    d = pl.program_id(0)
    c = pl.program_id(3)
    C = CHUNK

    chains = [(bi, p) for bi in range(nseq) for p in range(group)]
    keys = [(bi, slice(p * PAIR, (p + 1) * PAIR)) for bi, p in chains]

    @pl.when(c == 0)
    def _():
        if has_s0:
            z = jnp.zeros((HEAD_DIM, HEAD_DIM), F32)
            for j, (bi, p) in enumerate(chains):
                top = jnp.concatenate([s0_ref[bi, 2 * p], z], axis=1)
                bot = jnp.concatenate([z, s0_ref[bi, 2 * p + 1]], axis=1)
                s_scr[j] = jnp.concatenate([top, bot], axis=0)
        else:
            s_scr[...] = jnp.zeros_like(s_scr)

    row = lax.broadcasted_iota(jnp.int32, (PAIR, PAIR), 0)
    col = lax.broadcasted_iota(jnp.int32, (PAIR, PAIR), 1)
    same_head = _div_p2(row, HEAD_DIM) == _div_p2(col, HEAD_DIM)
    sign = 1 - 2 * d
    order = (_mod_p2(row, C) - _mod_p2(col, C)) * sign
    strict = (order > 0) & same_head
    incl = (order >= 0)[:C, :]
    head0 = lax.broadcasted_iota(jnp.int32, (C, PAIR), 1) < HEAD_DIM

    def stack(z):
        return jnp.concatenate([jnp.where(head0, z, 0.0), jnp.where(head0, 0.0, z)], axis=0)

    def each(fn, *lists):
        return [fn(*args) for args in zip(*lists)]

    trow = lax.broadcasted_iota(jnp.int32, (C, PAIR), 0)
    rev = d.astype(F32)

    def cumsum(lw, tot):
        acc = lw
        s = 1
        while s < C:
            acc = acc + jnp.where(trow >= s, pltpu.roll(acc, s, axis=0), 0.0)
            s *= 2
        return rev * (tot + lw) + (1.0 - 2.0 * rev) * acc

    def rd(ref, key):
        return ref[key[0], :, key[1]]

    lw = [rd(lw_ref, k) for k in keys]
    tot = [jnp.sum(z, axis=0, keepdims=True) for z in lw]
    cs = each(cumsum, lw, tot)
    w_inv = [jnp.exp(-z) for z in cs]
    w_rest = each(lambda t, z: jnp.exp(t - z), tot, cs)
    r_t = each(lambda k, z: rd(r_ref, k) * jnp.exp(z), keys, cs)
    kk_st = each(lambda k, z, l: stack(rd(kk_ref, k) * jnp.exp(z - l)), keys, cs, lw)
    v_st = [stack(rd(v_ref, k)) for k in keys]
    kb_st = each(lambda k, wi: jnp.concatenate([stack(rd(kd_ref, k) * wi), stack(rd(b_ref, k) * wi)], axis=0),
                 keys, w_inv)
    m = each(lambda a, rt, kb: _mms('gram', jnp.concatenate([a, rt], axis=0), kb, _NT), kk_st, r_t, kb_st)
    x = [jnp.where(strict, -z[:2 * C, 2 * C:], 0.0) for z in m]
    p_b = [jnp.where(incl, z[2 * C:, 2 * C:], 0.0) for z in m]
    apk_mask = jnp.concatenate([strict, incl], axis=0)
    apk_v = each(lambda z, vs: _mms('akv', jnp.where(apk_mask, z[:, :2 * C], 0.0), vs), m, v_st)

    ti, tj = _mod_p2(row, C), _mod_p2(col, C)
    eye = jnp.where(row == col, 1.0, 0.0)

    def off_block(n):
        return (_div_p2(ti, 2 * n) == _div_p2(tj, 2 * n)) & (_div_p2(ti, n) != _div_p2(tj, n))

    t_inv = [eye + jnp.where(off_block(1), z, 0.0) for z in x]
    n = 2
    while n < C:
        keep = off_block(n)
        mid = each(lambda z, t: _mms('inv_a', jnp.where(keep, z, 0.0), t), x, t_inv)
        t_inv = each(lambda t, mm: t + _mms('inv_b', t, mm), t_inv, mid)
        n *= 2
    q = each(lambda t, a, av: _mms('apply', t, jnp.concatenate([a, av[:2 * C]], axis=1)),
             t_inv, kk_st, apk_v)

    ry = each(lambda rt, av, pb, qi: jnp.concatenate([rt, av[2 * C:]], axis=1) - _mms('pbq', pb, qi),
              r_t, apk_v, p_b, q)
    gq = each(lambda qi, k, wr: _mms('trans', qi, stack(rd(b_ref, k) * wr), _TN), q, keys, w_rest)
    s_add = each(lambda vs, k, wr, g: _mms('sadd', vs, stack(rd(kd_ref, k) * wr), _TN) - g[PAIR:],
                 v_st, keys, w_rest, gq)

    s_old = [s_scr[j] for j in range(len(chains))]
    for j, (bi, sl) in enumerate(keys):
        y_ref[bi, :, sl] = _mms('y_state', ry[j][:, :PAIR], s_old[j], _NT) + ry[j][:, PAIR:]
    s_new = each(lambda s, t, g, sa: s * jnp.exp(t) - _mms('s_state', s, g[:PAIR]) + sa, s_old, tot, gq, s_add)
    for j in range(len(chains)):
        s_scr[j] = s_new[j]

    if want_state:
        @pl.when(c == pl.num_programs(3) - 1)
        def _():
            for j, (bi, p) in enumerate(chains):
                sout_ref[bi, 2 * p] = s_new[j][:HEAD_DIM, :HEAD_DIM]
                sout_ref[bi, 2 * p + 1] = s_new[j][HEAD_DIM:, HEAD_DIM:]


def _scan(r, v, kk, lw, kd, b, s0, batch, seq_len, want_state):
    rows = r.shape[0]
    nc = seq_len // CHUNK
    group = SCAN_GROUP
    nseq = min(SCAN_SEQS, batch)
    n_groups = D_A // (PAIR * group)
    gw = group * PAIR

    def chunk_idx(d, c):
        return c + d * (nc - 1 - 2 * c)

    shared = pl.BlockSpec((nseq, CHUNK, gw), lambda d, bi, p, c: (bi, chunk_idx(d, c), p))
    per_dir = pl.BlockSpec((None, nseq, CHUNK, gw), lambda d, bi, p, c: (d, bi, chunk_idx(d, c), p))
    state = pl.BlockSpec((nseq, None, 2 * group, HEAD_DIM, HEAD_DIM), lambda d, bi, p, c: (bi, d, p, 0, 0))
    in_specs = [shared, shared, shared, per_dir, per_dir, per_dir]
    seqs = lambda a: a.reshape(a.shape[:-2] + (batch, seq_len, D_A))
    args = [seqs(a) for a in (r, v, kk, lw, kd, b)]
    if s0 is not None:
        in_specs.append(state)
        args.append(s0)
    out_specs = [per_dir]
    out_shape = [jax.ShapeDtypeStruct((2, batch, seq_len, D_A), F32)]
    if want_state:
        out_specs.append(state)
        out_shape.append(jax.ShapeDtypeStruct((batch, 2, N_HEADS, HEAD_DIM, HEAD_DIM), F32))
    outs = pl.pallas_call(
        functools.partial(_scan_kernel, has_s0=s0 is not None, want_state=want_state, group=group, nseq=nseq),
        grid=(2, batch // nseq, n_groups, nc),
        in_specs=in_specs,
        out_specs=out_specs,
        out_shape=out_shape,
        scratch_shapes=[pltpu.VMEM((nseq * group, PAIR, PAIR), F32)],
        compiler_params=_cparams(("parallel", "parallel", "parallel", "arbitrary")),
        name="rwkv7_scan",
    )(*args)
    return [outs[0].reshape(2, rows, D_A)] + list(outs[1:])


def _post_kernel(y_ref, bonus_ref, g_ref, pool_ref, pprev_ref, pnext_ref, gates_ref, x_ref, mod_ref,
                 lnxg_ref, lnxb_ref, worwkv_ref, wpool_ref, pscale_ref, wopool_ref, wout_ref,
                 lng_ref, lnb_ref, ones_ref, o_ref, ext_scr, *, seq_len, tm):
    i = pl.program_id(0)
    ones_bd = ones_ref[...]
    y = y_ref[0] + y_ref[1]
    mu = _head_sum(y, ones_bd) * (1.0 / HEAD_DIM)
    dlt = y - mu
    var = _head_sum(dlt * dlt, ones_bd) * (1.0 / HEAD_DIM)
    yn = dlt * lax.rsqrt(var + GN_EPS) * lnxg_ref[...] + lnxb_ref[...]
    ya = _dot((yn + bonus_ref[...]) * g_ref[...], worwkv_ref[...])

    ext_scr[0:POOL_HALO, :] = pprev_ref[...]
    ext_scr[POOL_HALO:POOL_HALO + tm, :] = pool_ref[...]
    ext_scr[POOL_HALO + tm:POOL_HALO + tm + POOL_HALO, :] = pnext_ref[...]
    t = _mod_p2(i * tm + lax.broadcasted_iota(jnp.int32, (tm, LANES), 0), seq_len)
    us = []
    for gi, w in enumerate(POOL_WINDOWS):
        sl = slice(gi * POOL_GROUP_DIM, (gi + 1) * POOL_GROUP_DIM)
        acc = jnp.zeros((tm, POOL_GROUP_DIM), F32)
        for off in range(-(w // 2), w - w // 2):
            rows = ext_scr[POOL_HALO + off:POOL_HALO + off + tm, sl]
            valid = (t + off >= 0) & (t + off < seq_len)
            acc = acc + jnp.where(valid, rows, 0.0)
        cnt = jnp.minimum(t + (w - w // 2), seq_len) - jnp.maximum(t - w // 2, 0)
        resid = acc / cnt.astype(F32) - ext_scr[POOL_HALO:POOL_HALO + tm, sl]
        us.append(_dot(resid, wpool_ref[gi]))
    u = jnp.concatenate(us, axis=1) * pscale_ref[...]
    yb = _dot(u, wopool_ref[...])

    gates = gates_ref[...]
    merged = _sigmoid(gates[:, :D_MODEL]) * ya + _sigmoid(gates[:, D_MODEL:]) * yb
    mix = _dot(merged, wout_ref[...])
    z = ALPHA * x_ref[...] + mod_ref[5:6, :] * mix
    o_ref[...] = _layer_norm(z, lng_ref[...], lnb_ref[...])


def _post(y, bonus, g, pool_in, gates, x, mod, mod_of_tile, p, seq_len, tm):
    rows = x.shape[0]
    hb = tm // POOL_HALO
    n_halo_blocks = rows // POOL_HALO
    row1 = lambda a: a.reshape(1, -1)
    full = lambda shape: pl.BlockSpec(shape, lambda i: (0,) * len(shape))
    tile = lambda width: pl.BlockSpec((tm, width), lambda i: (i, 0))
    return pl.pallas_call(
        functools.partial(_post_kernel, seq_len=seq_len, tm=tm),
        grid=(rows // tm,),
        in_specs=[pl.BlockSpec((2, tm, D_A), lambda i: (0, i, 0)), tile(D_A), tile(D_A), tile(D_B),
                  pl.BlockSpec((POOL_HALO, D_B), lambda i: (jnp.maximum(i * hb - 1, 0), 0)),
                  pl.BlockSpec((POOL_HALO, D_B), lambda i: (jnp.minimum((i + 1) * hb, n_halo_blocks - 1), 0)),
                  tile(2 * D_MODEL), tile(D_MODEL),
                  pl.BlockSpec((None, N_MOD, D_MODEL), lambda i: (mod_of_tile(i, tm), 0, 0)),
                  full((1, D_A)), full((1, D_A)), full((D_A, D_MODEL)),
                  full((len(POOL_WINDOWS), POOL_GROUP_DIM, POOL_GROUP_DIM)), full((1, D_B)),
                  full((D_B, D_MODEL)), full((D_MODEL, D_MODEL)), full((1, D_MODEL)), full((1, D_MODEL)),
                  full((LANES, LANES))],
        out_specs=tile(D_MODEL),
        out_shape=jax.ShapeDtypeStruct((rows, D_MODEL), F32),
        scratch_shapes=[pltpu.VMEM((tm + 2 * POOL_HALO, D_B), F32)],
        compiler_params=_cparams(("parallel",)),
        name="mixer_out",
    )(y, bonus, g, pool_in, pool_in, pool_in, gates, x, mod, row1(p['lnx_g']), row1(p['lnx_b']),
      p['w_o_rwkv'], p['w_pool'], row1(p['pool_scale']), p['w_o_pool'], p['w_out'],
      row1(p['ln_g'][1]), row1(p['ln_b'][1]), p['ones_bd'])


def _trunk_layer(x, mod, mod_of_tile, s0, batch, seq_len, latent, want_state, p):
    tm = 256
    x1 = _ffn(x, mod, mod_of_tile, p['ffn_in'][0], p['ffn_out'][0], p['ln_g'][0], p['ln_b'][0], 0, 512)
    ps, pool_in, gates = _proj(x1, mod, mod_of_tile, p['w_mix_in'], 3, 512)
    r, v, kk, lw, kd, b, bonus, g = _prep(ps, p, seq_len, latent, tm)
    outs = _scan(r, v, kk, lw, kd, b, s0, batch, seq_len, want_state)
    y = outs[0]
    x2 = _post(y, bonus, g, pool_in, gates, x1, mod, mod_of_tile, p, seq_len, tm)
    x3 = _ffn(x2, mod, mod_of_tile, p['ffn_in'][1], p['ffn_out'][1], p['ln_g'][2], p['ln_b'][2], 6, 512)
    return x3, (outs[1] if want_state else None)


def _block_diag2(w):
    z = jnp.zeros_like(w[0])
    return jnp.concatenate([jnp.concatenate([w[0], z], axis=1), jnp.concatenate([z, w[1]], axis=1)], axis=0)


def kernel(x_prompt, x_sample, c, state_rwkv, c_ctx, w_mod, b_mod, ln_g, ln_b, ffn_in, ffn_out, w_mix_in,
           mu_shift, w0, w_up, a0, a_up, g_up, k_k, k_a, r_k, lnx_g, lnx_b, w_o_rwkv, w_pool, pool_scale,
           w_o_pool, w_out):
    batch, seq, _ = x_prompt.shape
    dec_batch, dec_seq, _ = x_sample.shape
    y_p = x_prompt.reshape(batch * seq, D_MODEL)
    y_s = x_sample.reshape(dec_batch * dec_seq, D_MODEL)
    cvecs = jnp.concatenate([c_ctx[None, :], c, jnp.zeros((8 - 1 - dec_batch, D_MODEL), F32)], axis=0)
    ii = lax.broadcasted_iota(jnp.int32, (LANES, LANES), 0) // HEAD_DIM
    jj = lax.broadcasted_iota(jnp.int32, (LANES, LANES), 1) // HEAD_DIM
    ones_bd = (ii == jj).astype(BF16)
    ctx_states = []
    for l in range(DEPTH):
        p = {
            'ln_g': ln_g[l], 'ln_b': ln_b[l],
            'ffn_in': ffn_in[l].astype(BF16), 'ffn_out': ffn_out[l].astype(BF16),
            'w_mix_in': w_mix_in[l].astype(BF16),
            'mu_shift': mu_shift[l], 'w0': w0[l], 'a0': a0[l],
            'wup_bd': _block_diag2(w_up[l]).astype(BF16), 'aup_bd': _block_diag2(a_up[l]).astype(BF16),
            'g_up': g_up[l].astype(BF16), 'k_k': k_k[l], 'k_a': k_a[l], 'r_k': r_k[l],
            'lnx_g': lnx_g[l], 'lnx_b': lnx_b[l], 'w_o_rwkv': w_o_rwkv[l].astype(BF16),
            'w_pool': w_pool[l].astype(BF16), 'pool_scale': pool_scale[l],
            'w_o_pool': w_o_pool[l].astype(BF16), 'w_out': w_out[l].astype(BF16), 'ones_bd': ones_bd,
        }
        mod = _modulation(cvecs, w_mod[l], b_mod[l]).reshape(8, N_MOD, D_MODEL)
        y_p, s_ctx = _trunk_layer(y_p, mod, lambda i, tm: 0, None, batch, seq, False, True, p)
        ctx_states.append(s_ctx)
        y_s, _ = _trunk_layer(y_s, mod, lambda i, tm: 1 + (i * tm) // dec_seq, state_rwkv[:, l],
                              dec_batch, dec_seq, True, False, p)
    new_state = jnp.stack(ctx_states, axis=1).astype(x_prompt.dtype)
    return (y_p.reshape(batch, seq, D_MODEL), y_s.reshape(dec_batch, dec_seq, D_MODEL), new_state)
```

```python
import functools
import math

import jax
import jax.numpy as jnp
from jax import lax
from jax.experimental import pallas as pl
from jax.experimental.pallas import tpu as pltpu

F32 = jnp.float32
BF16 = jnp.bfloat16

D_MODEL = 1024
DEPTH = 1
GRID_W = 64
HEAD_DIM = 64
D_A = D_MODEL
N_HEADS = D_A // HEAD_DIM
D_B = D_MODEL // 2
POOL_WINDOWS = (2, 4, 8, 16)
POOL_GROUP_DIM = D_B // len(POOL_WINDOWS)
LORA_W = 64
LORA_A = 64
LORA_G = 128
D_FF = 2816
N_MOD = 9
SHIFT_W = 3 * D_A + 2 * LORA_W + 2 * LORA_A + LORA_G
ALPHA = (2 * DEPTH) ** 0.25
LN_EPS = 1e-5
GN_EPS = 64e-5
DECAY_SCALE = math.exp(-0.5)

LANES = 128
HALO = 64
POOL_HALO = 8
POOL_PAD = 64
CHUNK = 64
PAIR = 2 * HEAD_DIM
VMEM_LIMIT = 56 * 1024 * 1024


def _cparams(sem):
    return pltpu.CompilerParams(dimension_semantics=sem, vmem_limit_bytes=VMEM_LIMIT)


def _mod_p2(x, n):
    assert n & (n - 1) == 0
    return x & (n - 1)


def _div_p2(x, n):
    assert n & (n - 1) == 0
    return x >> (n.bit_length() - 1)


def _sigmoid(x):
    return 0.5 * jnp.tanh(0.5 * x) + 0.5


def _silu(x):
    return x / (1.0 + jnp.exp(-x))


def _dot(a, b):
    return jnp.dot(a.astype(BF16), b.astype(BF16), preferred_element_type=F32)


def _split2(x):
    hi = x.astype(BF16)
    lo = (x - hi.astype(F32)).astype(BF16)
    return hi, lo


def _split3(x):
    hi = x.astype(BF16)
    r1 = x - hi.astype(F32)
    mid = r1.astype(BF16)
    lo = (r1 - mid.astype(F32)).astype(BF16)
    return hi, mid, lo


_NN = (((1,), (0,)), ((), ()))
_NT = (((1,), (1,)), ((), ()))
_TN = (((0,), (0,)), ((), ()))


def _dg(a, b, dims):
    return lax.dot_general(a, b, dims, preferred_element_type=F32)


def _mm3(a, b, dims=_NN):
    ah, al = _split2(a)
    bh, bl = _split2(b)
    return _dg(ah, bh, dims) + (_dg(ah, bl, dims) + _dg(al, bh, dims))


def _layer_norm(z, g, b):
    mu = jnp.mean(z, axis=-1, keepdims=True)
    d = z - mu
    var = jnp.mean(d * d, axis=-1, keepdims=True)
    return d * lax.rsqrt(var + LN_EPS) * g + b


def _head_sum(x, ones_bd):
    hi, lo = _split2(x)
    outs = []
    for c in range(x.shape[1] // LANES):
        sl = slice(c * LANES, (c + 1) * LANES)
        outs.append(jnp.dot(hi[:, sl], ones_bd, preferred_element_type=F32)
                    + jnp.dot(lo[:, sl], ones_bd, preferred_element_type=F32))
    return jnp.concatenate(outs, axis=1)


def _mod_kernel(c_ref, w_ref, b_ref, o_ref):
    o_ref[...] = _dot(_silu(c_ref[...]), w_ref[...]) + b_ref[...]


def _modulation(cvecs, w_mod, b_mod):
    rows = cvecs.shape[0]
    n = w_mod.shape[1]
    tn = 1024
    return pl.pallas_call(
        _mod_kernel,
        grid=(n // tn,),
        in_specs=[pl.BlockSpec((rows, D_MODEL), lambda j: (0, 0)),
                  pl.BlockSpec((D_MODEL, tn), lambda j: (0, j)),
                  pl.BlockSpec((1, tn), lambda j: (0, j))],
        out_specs=pl.BlockSpec((rows, tn), lambda j: (0, j)),
        out_shape=jax.ShapeDtypeStruct((rows, n), F32),
        compiler_params=_cparams(("arbitrary",)),
        name="modulation",
    )(cvecs, w_mod, b_mod.reshape(1, n))


def _ffn_kernel(x_ref, mod_ref, wi_ref, wo_ref, lng_ref, lnb_ref, o_ref, *, mi):
    x = x_ref[...]
    h = (x * (1.0 + mod_ref[mi + 1:mi + 2, :]) + mod_ref[mi:mi + 1, :]).astype(BF16)
    gate = jnp.dot(h, wi_ref[:, :D_FF], preferred_element_type=F32)
    up = jnp.dot(h, wi_ref[:, D_FF:], preferred_element_type=F32)
    ff = _dot(_silu(gate) * up, wo_ref[...])
    z = ALPHA * x + 0.5 * mod_ref[mi + 2:mi + 3, :] * ff
    o_ref[...] = _layer_norm(z, lng_ref[...], lnb_ref[...])


def _ffn(x, mod, mod_of_tile, w_in, w_out, ln_g, ln_b, which, mi, tm):
    rows = x.shape[0]
    resident = lambda shape: pl.BlockSpec(shape, lambda i: (0,) * len(shape), pipeline_mode=pl.Buffered(1))
    picked = lambda shape: pl.BlockSpec((None,) + shape, lambda i: (which, 0, 0), pipeline_mode=pl.Buffered(1))
    return pl.pallas_call(
        functools.partial(_ffn_kernel, mi=mi),
        grid=(rows // tm,),
        in_specs=[pl.BlockSpec((tm, D_MODEL), lambda i: (i, 0)),
                  pl.BlockSpec((None, N_MOD, D_MODEL), lambda i: (mod_of_tile(i, tm), 0, 0)),
                  picked((D_MODEL, 2 * D_FF)), picked((D_FF, D_MODEL)),
                  resident((1, D_MODEL)), resident((1, D_MODEL))],
        out_specs=pl.BlockSpec((tm, D_MODEL), lambda i: (i, 0)),
        out_shape=jax.ShapeDtypeStruct((rows, D_MODEL), F32),
        compiler_params=_cparams(("parallel",)),
        name="ffn",
    )(x, mod, w_in, w_out, ln_g.reshape(1, -1), ln_b.reshape(1, -1))


def _proj_kernel(x_ref, mod_ref, w_ref, ps_o, pool_o, gate_o, *, mi):
    shift = mod_ref[mi:mi + 1, :]
    scale = mod_ref[mi + 1:mi + 2, :]
    h = (x_ref[...] * (1.0 + scale) + shift).astype(BF16)
    ps_o[...] = jnp.dot(h, w_ref[:, :SHIFT_W], preferred_element_type=F32)
    pool_o[...] = jnp.dot(h, w_ref[:, SHIFT_W:SHIFT_W + D_B], preferred_element_type=F32)
    gate_o[...] = jnp.dot(h, w_ref[:, SHIFT_W + D_B:], preferred_element_type=F32)


def _proj(x, mod, mod_of_tile, w, mi, tm):
    rows = x.shape[0]
    n = w.shape[1]
    widths = (SHIFT_W, D_B, n - SHIFT_W - D_B)
    return pl.pallas_call(
        functools.partial(_proj_kernel, mi=mi),
        grid=(rows // tm,),
        in_specs=[pl.BlockSpec((tm, D_MODEL), lambda i: (i, 0)),
                  pl.BlockSpec((None, N_MOD, D_MODEL), lambda i: (mod_of_tile(i, tm), 0, 0)),
                  pl.BlockSpec((D_MODEL, n), lambda i: (0, 0), pipeline_mode=pl.Buffered(1))],
        out_specs=[pl.BlockSpec((tm, wd), lambda i: (i, 0)) for wd in widths],
        out_shape=[jax.ShapeDtypeStruct((rows, wd), F32) for wd in widths],
        compiler_params=_cparams(("parallel",)),
        name="mix_in_proj",
    )(x, mod, w)


def _prep_kernel(cur_ref, prev_ref, next_ref, mu_ref, wup_ref, aup_ref, gup_ref, w0_ref, a0_ref,
                 kk_ref, ka_ref, rk_ref, ones_ref,
                 r_o, v_o, kk_o, lw_o, kd_o, b_o, bonus_o, g_o, ext_scr, *, seq_len, latent, tm):
    i = pl.program_id(0)
    t0 = _mod_p2(i * tm, seq_len)
    ext_scr[0:HALO, :] = jnp.where(t0 == 0, 0.0, prev_ref[...])
    ext_scr[HALO:HALO + tm, :] = cur_ref[...]
    ext_scr[HALO + tm:HALO + tm + HALO, :] = jnp.where(t0 + tm == seq_len, 0.0, next_ref[...])

    t = t0 + lax.broadcasted_iota(jnp.int32, (tm, LANES), 0)
    lane = lax.broadcasted_iota(jnp.int32, (1, LANES), 1)
    n_src = 4 if latent else 2
    if latent:
        col = _mod_p2(t, GRID_W)
        has_before = jnp.where(col > 0, 1.0, 0.0)
        has_after = jnp.where(col < GRID_W - 1, 1.0, 0.0)

    def mixed_cols(c):
        sl = slice(c * LANES, (c + 1) * LANES)
        mu = mu_ref[:, sl]
        coef = [jnp.where(_mod_p2(lane, n_src) == s, mu, 0.0) for s in range(n_src)]
        before = ext_scr[HALO - 1:HALO - 1 + tm, sl]
        after = ext_scr[HALO + 1:HALO + 1 + tm, sl]
        if latent:
            return (ext_scr[HALO:HALO + tm, sl] * (1.0 - mu) + (before * has_before) * coef[0]
                    + (after * has_after) * coef[1]
                    + ext_scr[0:tm, sl] * coef[2] + ext_scr[2 * HALO:2 * HALO + tm, sl] * coef[3])
        return ext_scr[HALO:HALO + tm, sl] * (1.0 - mu) + before * coef[0] + after * coef[1]

    def mixed(c0, c1):
        return jnp.concatenate([mixed_cols(c) for c in range(c0, c1)], axis=1)

    nb = D_A // LANES
    r = mixed(0, nb)
    k = mixed(nb, 2 * nb)
    v = mixed(2 * nb, 3 * nb)
    w_down = mixed_cols(3 * nb)
    a_down = mixed_cols(3 * nb + 1)
    g_down = mixed_cols(3 * nb + 2)

    ones_bd = ones_ref[...]
    r_o[...] = r
    v_o[...] = v
    kkraw = k * kk_ref[...]
    ss = _head_sum(kkraw * kkraw, ones_bd)
    kk = kkraw / jnp.maximum(jnp.sqrt(ss), 1e-12)
    kk_o[...] = kk
    bonus_o[...] = _head_sum(r * k * rk_ref[...], ones_bd) * v
    g_o[...] = _dot(_sigmoid(g_down), gup_ref[...])

    w_raw = _dot(jnp.tanh(w_down), wup_ref[...])
    a_raw = _dot(a_down, aup_ref[...])
    for d in range(2):
        sl = slice(d * D_A, (d + 1) * D_A)
        lw_o[d] = -DECAY_SCALE * _sigmoid(w0_ref[d:d + 1, :] + w_raw[:, sl])
        a = _sigmoid(a0_ref[d:d + 1, :] + a_raw[:, sl])
        kd_o[d] = k * (1.0 + (a - 1.0) * ka_ref[...])
        b_o[d] = kk * a


def _prep(ps, p, seq_len, latent, tm):
    rows = ps.shape[0]
    assert seq_len % tm == 0 and tm % HALO == 0
    hb = tm // HALO
    n_halo_blocks = rows // HALO
    row1 = lambda a: a.reshape(1, -1)
    full = lambda shape: pl.BlockSpec(shape, lambda i: (0,) * len(shape))
    out_rows = pl.BlockSpec((tm, D_A), lambda i: (i, 0))
    out_dir = pl.BlockSpec((2, tm, D_A), lambda i: (0, i, 0))
    sds = jax.ShapeDtypeStruct
    return pl.pallas_call(
        functools.partial(_prep_kernel, seq_len=seq_len, latent=latent, tm=tm),
        grid=(rows // tm,),
        in_specs=[pl.BlockSpec((tm, SHIFT_W), lambda i: (i, 0)),
                  pl.BlockSpec((HALO, SHIFT_W), lambda i: (jnp.maximum(i * hb - 1, 0), 0)),
                  pl.BlockSpec((HALO, SHIFT_W), lambda i: (jnp.minimum((i + 1) * hb, n_halo_blocks - 1), 0)),
                  full((1, SHIFT_W)), full((LANES, 2 * D_A)), full((LANES, 2 * D_A)), full((LORA_G, D_A)),
                  full((2, D_A)), full((2, D_A)), full((1, D_A)), full((1, D_A)), full((1, D_A)),
                  full((LANES, LANES))],
        out_specs=[out_rows, out_rows, out_rows, out_dir, out_dir, out_dir, out_rows, out_rows],
        out_shape=[sds((rows, D_A), F32), sds((rows, D_A), F32), sds((rows, D_A), F32),
                   sds((2, rows, D_A), F32), sds((2, rows, D_A), F32), sds((2, rows, D_A), F32),
                   sds((rows, D_A), F32), sds((rows, D_A), F32)],
        scratch_shapes=[pltpu.VMEM((tm + 2 * HALO, SHIFT_W), F32)],
        compiler_params=_cparams(("parallel",)),
        name="shift_prep",
    )(ps, ps, ps, row1(p['mu_shift']), p['wup_bd'], p['aup_bd'], p['g_up'], p['w0'], p['a0'],
      row1(p['k_k']), row1(p['k_a']), row1(p['r_k']), p['ones_bd'])


SCAN_SPLITS = {
    'gram': (1, 1), 'akv': (1, 1), 'apply': (1, 1), 'pbt': (1, 1),
    'inv_a': (1, 1), 'inv_b': (1, 1),
    'trans': (1, 1), 'sadd': (1, 1), 'y_state': (1, 1), 's_state': (1, 1),
}
SCAN_GROUP = 8
SCAN_SEQS = 2


def _parts(x, n):
    return (x.astype(BF16),) if n == 1 else _split2(x)


def _mmp(ap, bp, dims=_NN):
    out = _dg(ap[0], bp[0], dims)
    if len(bp) > 1:
        out = out + _dg(ap[0], bp[1], dims)
    if len(ap) > 1:
        out = out + _dg(ap[1], bp[0], dims)
    return out


def _mms(site, a, b, dims=_NN):
    na, nb = SCAN_SPLITS[site]
    return _mmp(_parts(a, na), _parts(b, nb), dims)


def _scan_kernel(*refs, has_s0, want_state, group, nseq):
    if has_s0:
        (r_ref, v_ref, kk_ref, lw_ref, kd_ref, b_ref, s0_ref), rest = refs[:7], refs[7:]
    else:
        (r_ref, v_ref, kk_ref, lw_ref, kd_ref, b_ref), rest = refs[:6], refs[6:]
    if want_state:
        y_ref, sout_ref, s_scr = rest
    else:
        y_ref, s_scr = rest
    d = pl.program_id(0)
    c = pl.program_id(3)
    C = CHUNK

    chains = [(bi, p) for bi in range(nseq) for p in range(group)]
    keys = [(bi, slice(p * PAIR, (p + 1) * PAIR)) for bi, p in chains]

    @pl.when(c == 0)
    def _():
        if has_s0:
            z = jnp.zeros((HEAD_DIM, HEAD_DIM), F32)
            for j, (bi, p) in enumerate(chains):
                top = jnp.concatenate([s0_ref[bi, 2 * p], z], axis=1)
                bot = jnp.concatenate([z, s0_ref[bi, 2 * p + 1]], axis=1)
                s_scr[j] = jnp.concatenate([top, bot], axis=0)
        else:
            s_scr[...] = jnp.zeros_like(s_scr)

    row = lax.broadcasted_iota(jnp.int32, (PAIR, PAIR), 0)
    col = lax.broadcasted_iota(jnp.int32, (PAIR, PAIR), 1)
    same_head = _div_p2(row, HEAD_DIM) == _div_p2(col, HEAD_DIM)
    sign = 1 - 2 * d
    order = (_mod_p2(row, C) - _mod_p2(col, C)) * sign
    strict = (order > 0) & same_head
    incl = (order >= 0)[:C, :]
    head0 = lax.broadcasted_iota(jnp.int32, (C, PAIR), 1) < HEAD_DIM

    def stack(z):
        return jnp.concatenate([jnp.where(head0, z, 0.0), jnp.where(head0, 0.0, z)], axis=0)

    def each(fn, *lists):
        return [fn(*args) for args in zip(*lists)]

    trow = lax.broadcasted_iota(jnp.int32, (C, PAIR), 0)
    rev = d.astype(F32)

    def cumsum(lw, tot):
        acc = lw
        s = 1
        while s < C:
            acc = acc + jnp.where(trow >= s, pltpu.roll(acc, s, axis=0), 0.0)
            s *= 2
        return rev * (tot + lw) + (1.0 - 2.0 * rev) * acc

    def rd(ref, key):
        return ref[key[0], :, key[1]]

    lw = [rd(lw_ref, k) for k in keys]
    tot = [jnp.sum(z, axis=0, keepdims=True) for z in lw]
    cs = each(cumsum, lw, tot)
    w_inv = [jnp.exp(-z) for z in cs]
    w_rest = each(lambda t, z: jnp.exp(t - z), tot, cs)
    r_t = each(lambda k, z: rd(r_ref, k) * jnp.exp(z), keys, cs)
    kk_st = each(lambda k, z, l: stack(rd(kk_ref, k) * jnp.exp(z - l)), keys, cs, lw)
    v_st = [stack(rd(v_ref, k)) for k in keys]
    kb_st = each(lambda k, wi: jnp.concatenate([stack(rd(kd_ref, k) * wi), stack(rd(b_ref, k) * wi)], axis=0),
                 keys, w_inv)
    m = each(lambda a, rt, kb: _mms('gram', jnp.concatenate([a, rt], axis=0), kb, _NT), kk_st, r_t, kb_st)
    x = [jnp.where(strict, -z[:2 * C, 2 * C:], 0.0) for z in m]
    p_b = [jnp.where(incl, z[2 * C:, 2 * C:], 0.0) for z in m]
    apk_mask = jnp.concatenate([strict, incl], axis=0)
    apk_v = each(lambda z, vs: _mms('akv', jnp.where(apk_mask, z[:, :2 * C], 0.0), vs), m, v_st)

    ti, tj = _mod_p2(row, C), _mod_p2(col, C)
    eye = jnp.where(row == col, 1.0, 0.0)

    def off_block(n):
        return (_div_p2(ti, 2 * n) == _div_p2(tj, 2 * n)) & (_div_p2(ti, n) != _div_p2(tj, n))

    t_inv = [eye + jnp.where(off_block(1), z, 0.0) for z in x]
    n = 2
    while n < C:
        keep = off_block(n)
        mid = each(lambda z, t: _mms('inv_a', jnp.where(keep, z, 0.0), t), x, t_inv)
        t_inv = each(lambda t, mm: t + _mms('inv_b', t, mm), t_inv, mid)
        n *= 2
    pbt = each(lambda pb, t: _mms('pbt', pb, t), p_b, t_inv)
    qq = each(lambda t, pt, a, av: _mms('apply', jnp.concatenate([t, pt], axis=0),
                                        jnp.concatenate([a, av[:2 * C]], axis=1)),
              t_inv, pbt, kk_st, apk_v)
    q = [z[:2 * C] for z in qq]
    ry = each(lambda rt, av, z: jnp.concatenate([rt, av[2 * C:]], axis=1) - z[2 * C:], r_t, apk_v, qq)
    gq = each(lambda qi, k, wr: _mms('trans', qi, stack(rd(b_ref, k) * wr), _TN), q, keys, w_rest)
    s_add = each(lambda vs, k, wr, g: _mms('sadd', vs, stack(rd(kd_ref, k) * wr), _TN) - g[PAIR:],
                 v_st, keys, w_rest, gq)

    s_old = [s_scr[j] for j in range(len(chains))]
    for j, (bi, sl) in enumerate(keys):
        y_ref[bi, :, sl] = _mms('y_state', ry[j][:, :PAIR], s_old[j], _NT) + ry[j][:, PAIR:]
    s_new = each(lambda s, t, g, sa: s * jnp.exp(t) - _mms('s_state', s, g[:PAIR]) + sa, s_old, tot, gq, s_add)
    for j in range(len(chains)):
        s_scr[j] = s_new[j]

    if want_state:
        @pl.when(c == pl.num_programs(3) - 1)
        def _():
            for j, (bi, p) in enumerate(chains):
                sout_ref[bi, 2 * p] = s_new[j][:HEAD_DIM, :HEAD_DIM]
                sout_ref[bi, 2 * p + 1] = s_new[j][HEAD_DIM:, HEAD_DIM:]


def _scan(r, v, kk, lw, kd, b, s0, batch, seq_len, want_state):
    rows = r.shape[0]
    nc = seq_len // CHUNK
    group = SCAN_GROUP
    nseq = min(SCAN_SEQS, batch)
    n_groups = D_A // (PAIR * group)
    gw = group * PAIR

    def chunk_idx(d, c):
        return c + d * (nc - 1 - 2 * c)

    shared = pl.BlockSpec((nseq, CHUNK, gw), lambda d, bi, p, c: (bi, chunk_idx(d, c), p))
    per_dir = pl.BlockSpec((None, nseq, CHUNK, gw), lambda d, bi, p, c: (d, bi, chunk_idx(d, c), p))
    state = pl.BlockSpec((nseq, None, 2 * group, HEAD_DIM, HEAD_DIM), lambda d, bi, p, c: (bi, d, p, 0, 0))
    in_specs = [shared, shared, shared, per_dir, per_dir, per_dir]
    seqs = lambda a: a.reshape(a.shape[:-2] + (batch, seq_len, D_A))
    args = [seqs(a) for a in (r, v, kk, lw, kd, b)]
    if s0 is not None:
        in_specs.append(state)
        args.append(s0)
    out_specs = [per_dir]
    out_shape = [jax.ShapeDtypeStruct((2, batch, seq_len, D_A), F32)]
    if want_state:
        out_specs.append(state)
        out_shape.append(jax.ShapeDtypeStruct((batch, 2, N_HEADS, HEAD_DIM, HEAD_DIM), F32))
    outs = pl.pallas_call(
        functools.partial(_scan_kernel, has_s0=s0 is not None, want_state=want_state, group=group, nseq=nseq),
        grid=(2, batch // nseq, n_groups, nc),
        in_specs=in_specs,
        out_specs=out_specs,
        out_shape=out_shape,
        scratch_shapes=[pltpu.VMEM((nseq * group, PAIR, PAIR), F32)],
        compiler_params=_cparams(("parallel", "parallel", "parallel", "arbitrary")),
        name="rwkv7_scan",
    )(*args)
    return [outs[0].reshape(2, rows, D_A)] + list(outs[1:])


def _post_kernel(y_ref, bonus_ref, g_ref, pool_ref, pprev_ref, pnext_ref, gates_ref, x_ref, mod_ref,
                 lnxg_ref, lnxb_ref, worwkv_ref, wpool_ref, pscale_ref, wopool_ref, wout_ref,
                 lng_ref, lnb_ref, ones_ref, band_ref, o_ref, ext_scr, *, seq_len, tm):
    i = pl.program_id(0)
    ones_bd = ones_ref[...]
    y = y_ref[0] + y_ref[1]
    mu = _head_sum(y, ones_bd) * (1.0 / HEAD_DIM)

    t0 = _mod_p2(i * tm, seq_len)
    lo_edge, hi_edge = POOL_PAD - POOL_HALO, POOL_PAD + tm + POOL_HALO
    ext_scr[0:lo_edge, :] = jnp.zeros((lo_edge, D_B), F32)
    ext_scr[lo_edge:POOL_PAD, :] = jnp.where(t0 == 0, 0.0, pprev_ref[...])
    ext_scr[POOL_PAD:POOL_PAD + tm, :] = pool_ref[...]
    ext_scr[POOL_PAD + tm:hi_edge, :] = jnp.where(t0 + tm == seq_len, 0.0, pnext_ref[...])
    ext_scr[hi_edge:, :] = jnp.zeros((lo_edge, D_B), F32)
    ext_hi, ext_lo = _split2(ext_scr[...])
    groups = [slice(gi * POOL_GROUP_DIM, (gi + 1) * POOL_GROUP_DIM) for gi in range(len(POOL_WINDOWS))]
    win = [jnp.dot(band_ref[gi], ext_hi[:, sl], preferred_element_type=F32)
           + jnp.dot(band_ref[gi], ext_lo[:, sl], preferred_element_type=F32) for gi, sl in enumerate(groups)]

    dlt = y - mu
    var = _head_sum(dlt * dlt, ones_bd) * (1.0 / HEAD_DIM)

    t = t0 + lax.broadcasted_iota(jnp.int32, (tm, LANES), 0)
    us = []
    for gi, w in enumerate(POOL_WINDOWS):
        cnt = jnp.minimum(t + (w - w // 2), seq_len) - jnp.maximum(t - w // 2, 0)
        resid = win[gi] / cnt.astype(F32) - pool_ref[:, groups[gi]]
        us.append(_dot(resid, wpool_ref[gi]))

    yn = dlt * lax.rsqrt(var + GN_EPS) * lnxg_ref[...] + lnxb_ref[...]
    ya = _dot((yn + bonus_ref[...]) * g_ref[...], worwkv_ref[...])
    u = jnp.concatenate(us, axis=1) * pscale_ref[...]
    yb = _dot(u, wopool_ref[...])

    gates = gates_ref[...]
    merged = _sigmoid(gates[:, :D_MODEL]) * ya + _sigmoid(gates[:, D_MODEL:]) * yb
    mix = _dot(merged, wout_ref[...])
    z = ALPHA * x_ref[...] + mod_ref[5:6, :] * mix
    o_ref[...] = _layer_norm(z, lng_ref[...], lnb_ref[...])


def _pool_bands(tm):
    i = lax.broadcasted_iota(jnp.int32, (tm, tm + 2 * POOL_PAD), 0)
    j = lax.broadcasted_iota(jnp.int32, (tm, tm + 2 * POOL_PAD), 1) - POOL_PAD
    return jnp.stack([((j >= i - w // 2) & (j < i + w - w // 2)).astype(BF16) for w in POOL_WINDOWS])


def _post(y, bonus, g, pool_in, gates, x, mod, mod_of_tile, p, seq_len, tm):
    rows = x.shape[0]
    hb = tm // POOL_HALO
    n_halo_blocks = rows // POOL_HALO
    row1 = lambda a: a.reshape(1, -1)
    full = lambda shape: pl.BlockSpec(shape, lambda i: (0,) * len(shape))
    tile = lambda width: pl.BlockSpec((tm, width), lambda i: (i, 0))
    return pl.pallas_call(
        functools.partial(_post_kernel, seq_len=seq_len, tm=tm),
        grid=(rows // tm,),
        in_specs=[pl.BlockSpec((2, tm, D_A), lambda i: (0, i, 0)), tile(D_A), tile(D_A), tile(D_B),
                  pl.BlockSpec((POOL_HALO, D_B), lambda i: (jnp.maximum(i * hb - 1, 0), 0)),
                  pl.BlockSpec((POOL_HALO, D_B), lambda i: (jnp.minimum((i + 1) * hb, n_halo_blocks - 1), 0)),
                  tile(2 * D_MODEL), tile(D_MODEL),
                  pl.BlockSpec((None, N_MOD, D_MODEL), lambda i: (mod_of_tile(i, tm), 0, 0)),
                  full((1, D_A)), full((1, D_A)), full((D_A, D_MODEL)),
                  full((len(POOL_WINDOWS), POOL_GROUP_DIM, POOL_GROUP_DIM)), full((1, D_B)),
                  full((D_B, D_MODEL)), full((D_MODEL, D_MODEL)), full((1, D_MODEL)), full((1, D_MODEL)),
                  full((LANES, LANES)), full((len(POOL_WINDOWS), tm, tm + 2 * POOL_PAD))],
        out_specs=tile(D_MODEL),
        out_shape=jax.ShapeDtypeStruct((rows, D_MODEL), F32),
        scratch_shapes=[pltpu.VMEM((tm + 2 * POOL_PAD, D_B), F32)],
        compiler_params=_cparams(("parallel",)),
        name="mixer_out",
    )(y, bonus, g, pool_in, pool_in, pool_in, gates, x, mod, row1(p['lnx_g']), row1(p['lnx_b']),
      p['w_o_rwkv'], p['w_pool'], row1(p['pool_scale']), p['w_o_pool'], p['w_out'],
      row1(p['ln_g'][1]), row1(p['ln_b'][1]), p['ones_bd'], _pool_bands(tm))


def _trunk_layer(x, mod, mod_of_tile, s0, batch, seq_len, latent, want_state, p):
    tm = 256
    x1 = _ffn(x, mod, mod_of_tile, p['ffn_in'], p['ffn_out'], p['ln_g'][0], p['ln_b'][0], 0, 0, 512)
    ps, pool_in, gates = _proj(x1, mod, mod_of_tile, p['w_mix_in'], 3, 512)
    r, v, kk, lw, kd, b, bonus, g = _prep(ps, p, seq_len, latent, tm)
    outs = _scan(r, v, kk, lw, kd, b, s0, batch, seq_len, want_state)
    y = outs[0]
    x2 = _post(y, bonus, g, pool_in, gates, x1, mod, mod_of_tile, p, seq_len, tm)
    x3 = _ffn(x2, mod, mod_of_tile, p['ffn_in'], p['ffn_out'], p['ln_g'][2], p['ln_b'][2], 1, 6, 512)
    return x3, (outs[1] if want_state else None)


def _block_diag2(w):
    z = jnp.zeros_like(w[0])
    return jnp.concatenate([jnp.concatenate([w[0], z], axis=1), jnp.concatenate([z, w[1]], axis=1)], axis=0)


def kernel(x_prompt, x_sample, c, state_rwkv, c_ctx, w_mod, b_mod, ln_g, ln_b, ffn_in, ffn_out, w_mix_in,
           mu_shift, w0, w_up, a0, a_up, g_up, k_k, k_a, r_k, lnx_g, lnx_b, w_o_rwkv, w_pool, pool_scale,
           w_o_pool, w_out):
    batch, seq, _ = x_prompt.shape
    dec_batch, dec_seq, _ = x_sample.shape
    y_p = x_prompt.reshape(batch * seq, D_MODEL)
    y_s = x_sample.reshape(dec_batch * dec_seq, D_MODEL)
    cvecs = jnp.concatenate([c_ctx[None, :], c, jnp.zeros((8 - 1 - dec_batch, D_MODEL), F32)], axis=0)
    ii = lax.broadcasted_iota(jnp.int32, (LANES, LANES), 0) // HEAD_DIM
    jj = lax.broadcasted_iota(jnp.int32, (LANES, LANES), 1) // HEAD_DIM
    ones_bd = (ii == jj).astype(BF16)
    ctx_states = []
    for l in range(DEPTH):
        p = {
            'ln_g': ln_g[l], 'ln_b': ln_b[l],
            'ffn_in': ffn_in[l].astype(BF16), 'ffn_out': ffn_out[l].astype(BF16),
            'w_mix_in': w_mix_in[l].astype(BF16),
            'mu_shift': mu_shift[l], 'w0': w0[l], 'a0': a0[l],
            'wup_bd': _block_diag2(w_up[l]).astype(BF16), 'aup_bd': _block_diag2(a_up[l]).astype(BF16),
            'g_up': g_up[l].astype(BF16), 'k_k': k_k[l], 'k_a': k_a[l], 'r_k': r_k[l],
            'lnx_g': lnx_g[l], 'lnx_b': lnx_b[l], 'w_o_rwkv': w_o_rwkv[l].astype(BF16),
            'w_pool': w_pool[l].astype(BF16), 'pool_scale': pool_scale[l],
            'w_o_pool': w_o_pool[l].astype(BF16), 'w_out': w_out[l].astype(BF16), 'ones_bd': ones_bd,
        }
        mod = _modulation(cvecs, w_mod[l], b_mod[l]).reshape(8, N_MOD, D_MODEL)
        y_p, s_ctx = _trunk_layer(y_p, mod, lambda i, tm: 0, None, batch, seq, False, True, p)
        ctx_states.append(s_ctx)
        y_s, _ = _trunk_layer(y_s, mod, lambda i, tm: 1 + (i * tm) // dec_seq, state_rwkv[:, l],
                              dec_batch, dec_seq, True, False, p)
    new_state = jnp.stack(ctx_states, axis=1).astype(x_prompt.dtype)
    return (y_p.reshape(batch, seq, D_MODEL), y_s.reshape(dec_batch, dec_seq, D_MODEL), new_state)
```

```python
import functools
import math

import jax
import jax.numpy as jnp
from jax import lax
from jax.experimental import pallas as pl
from jax.experimental.pallas import tpu as pltpu

F32 = jnp.float32
BF16 = jnp.bfloat16

D_MODEL = 1024
DEPTH = 1
GRID_W = 64
HEAD_DIM = 64
D_A = D_MODEL
N_HEADS = D_A // HEAD_DIM
D_B = D_MODEL // 2
POOL_WINDOWS = (2, 4, 8, 16)
POOL_GROUP_DIM = D_B // len(POOL_WINDOWS)
LORA_W = 64
LORA_A = 64
LORA_G = 128
D_FF = 2816
N_MOD = 9
SHIFT_W = 3 * D_A + 2 * LORA_W + 2 * LORA_A + LORA_G
ALPHA = (2 * DEPTH) ** 0.25
LN_EPS = 1e-5
GN_EPS = 64e-5
DECAY_SCALE = math.exp(-0.5)

LANES = 128
HALO = 64
POOL_HALO = 8
POOL_PAD = 64
CHUNK = 64
PAIR = 2 * HEAD_DIM
VMEM_LIMIT = 56 * 1024 * 1024


def _cparams(sem):
    return pltpu.CompilerParams(dimension_semantics=sem, vmem_limit_bytes=VMEM_LIMIT)


def _mod_p2(x, n):
    assert n & (n - 1) == 0
    return x & (n - 1)


def _div_p2(x, n):
    assert n & (n - 1) == 0
    return x >> (n.bit_length() - 1)


def _sigmoid(x):
    return 0.5 * jnp.tanh(0.5 * x) + 0.5


def _silu(x):
    return x / (1.0 + jnp.exp(-x))


def _dot(a, b):
    return jnp.dot(a.astype(BF16), b.astype(BF16), preferred_element_type=F32)


def _split2(x):
    hi = x.astype(BF16)
    lo = (x - hi.astype(F32)).astype(BF16)
    return hi, lo


def _split3(x):
    hi = x.astype(BF16)
    r1 = x - hi.astype(F32)
    mid = r1.astype(BF16)
    lo = (r1 - mid.astype(F32)).astype(BF16)
    return hi, mid, lo


_NN = (((1,), (0,)), ((), ()))
_NT = (((1,), (1,)), ((), ()))
_TN = (((0,), (0,)), ((), ()))


def _dg(a, b, dims):
    return lax.dot_general(a, b, dims, preferred_element_type=F32)


def _mm3(a, b, dims=_NN):
    ah, al = _split2(a)
    bh, bl = _split2(b)
    return _dg(ah, bh, dims) + (_dg(ah, bl, dims) + _dg(al, bh, dims))


def _layer_norm(z, g, b):
    mu = jnp.mean(z, axis=-1, keepdims=True)
    d = z - mu
    var = jnp.mean(d * d, axis=-1, keepdims=True)
    return d * lax.rsqrt(var + LN_EPS) * g + b


def _head_sum(x, ones_bd):
    hi, lo = _split2(x)
    outs = []
    for c in range(x.shape[1] // LANES):
        sl = slice(c * LANES, (c + 1) * LANES)
        outs.append(jnp.dot(hi[:, sl], ones_bd, preferred_element_type=F32)
                    + jnp.dot(lo[:, sl], ones_bd, preferred_element_type=F32))
    return jnp.concatenate(outs, axis=1)


def _mod_kernel(c_ref, w_ref, b_ref, o_ref):
    o_ref[...] = _dot(_silu(c_ref[...]), w_ref[...]) + b_ref[...]


def _modulation(cvecs, w_mod, b_mod):
    rows = cvecs.shape[0]
    n = w_mod.shape[1]
    tn = 1024
    return pl.pallas_call(
        _mod_kernel,
        grid=(n // tn,),
        in_specs=[pl.BlockSpec((rows, D_MODEL), lambda j: (0, 0)),
                  pl.BlockSpec((D_MODEL, tn), lambda j: (0, j)),
                  pl.BlockSpec((1, tn), lambda j: (0, j))],
        out_specs=pl.BlockSpec((rows, tn), lambda j: (0, j)),
        out_shape=jax.ShapeDtypeStruct((rows, n), F32),
        compiler_params=_cparams(("arbitrary",)),
        name="modulation",
    )(cvecs, w_mod, b_mod.reshape(1, n))


def _ffn_kernel(x_ref, mod_ref, wi_ref, wo_ref, lng_ref, lnb_ref, o_ref, *, mi):
    x = x_ref[...]
    h = (x * (1.0 + mod_ref[mi + 1:mi + 2, :]) + mod_ref[mi:mi + 1, :]).astype(BF16)
    gate = jnp.dot(h, wi_ref[:, :D_FF], preferred_element_type=F32)
    up = jnp.dot(h, wi_ref[:, D_FF:], preferred_element_type=F32)
    ff = _dot(_silu(gate) * up, wo_ref[...])
    z = ALPHA * x + 0.5 * mod_ref[mi + 2:mi + 3, :] * ff
    o_ref[...] = _layer_norm(z, lng_ref[...], lnb_ref[...])


def _ffn(x, mod, mod_of_tile, w_in, w_out, ln_g, ln_b, which, mi, tm):
    rows = x.shape[0]
    resident = lambda shape: pl.BlockSpec(shape, lambda i: (0,) * len(shape), pipeline_mode=pl.Buffered(1))
    picked = lambda shape: pl.BlockSpec((None,) + shape, lambda i: (which, 0, 0), pipeline_mode=pl.Buffered(1))
    return pl.pallas_call(
        functools.partial(_ffn_kernel, mi=mi),
        grid=(rows // tm,),
        in_specs=[pl.BlockSpec((tm, D_MODEL), lambda i: (i, 0)),
                  pl.BlockSpec((None, N_MOD, D_MODEL), lambda i: (mod_of_tile(i, tm), 0, 0)),
                  picked((D_MODEL, 2 * D_FF)), picked((D_FF, D_MODEL)),
                  resident((1, D_MODEL)), resident((1, D_MODEL))],
        out_specs=pl.BlockSpec((tm, D_MODEL), lambda i: (i, 0)),
        out_shape=jax.ShapeDtypeStruct((rows, D_MODEL), F32),
        compiler_params=_cparams(("parallel",)),
        name="ffn",
    )(x, mod, w_in, w_out, ln_g.reshape(1, -1), ln_b.reshape(1, -1))


def _prep_kernel(x_ref, xprev_ref, xnext_ref, mod_ref, w_ref, mu_ref, wup_ref, aup_ref, w0_ref, a0_ref,
                 kk_ref, rk_ref, ones_ref,
                 r_o, v_o, kk_o, k_o, lw_o, a_o, bonus_o, gs_o, pool_o, gate_o, ext_scr,
                 *, seq_len, latent, tm, mi):
    i = pl.program_id(0)
    t0 = _mod_p2(i * tm, seq_len)
    shift = mod_ref[mi:mi + 1, :]
    scale = mod_ref[mi + 1:mi + 2, :]

    def modulated(ref):
        return (ref[...] * (1.0 + scale) + shift).astype(BF16)

    h = modulated(x_ref)
    h_ext = jnp.concatenate([modulated(xprev_ref), h, modulated(xnext_ref)], axis=0) if latent else None

    def project(c0, c1):
        if latent:
            ext_scr[:, c0:c1] = jnp.dot(h_ext, w_ref[:, c0:c1], preferred_element_type=F32)
            ext_scr[0:HALO, c0:c1] = jnp.where(t0 == 0, 0.0, ext_scr[0:HALO, c0:c1])
            ext_scr[HALO + tm:, c0:c1] = jnp.where(t0 + tm == seq_len, 0.0, ext_scr[HALO + tm:, c0:c1])
        else:
            ext_scr[0:HALO, c0:c1] = jnp.zeros((HALO, c1 - c0), F32)
            ext_scr[HALO:HALO + tm, c0:c1] = jnp.dot(h, w_ref[:, c0:c1], preferred_element_type=F32)
            ext_scr[HALO + tm:, c0:c1] = jnp.zeros((HALO, c1 - c0), F32)

    t = t0 + lax.broadcasted_iota(jnp.int32, (tm, LANES), 0)
    lane = lax.broadcasted_iota(jnp.int32, (1, LANES), 1)
    n_src = 4 if latent else 2
    if latent:
        col = _mod_p2(t, GRID_W)
        has_before = jnp.where(col > 0, 1.0, 0.0)
        has_after = jnp.where(col < GRID_W - 1, 1.0, 0.0)

    def mixed_cols(c):
        sl = slice(c * LANES, (c + 1) * LANES)
        mu = mu_ref[:, sl]
        coef = [jnp.where(_mod_p2(lane, n_src) == s, mu, 0.0) for s in range(n_src)]
        before = ext_scr[HALO - 1:HALO - 1 + tm, sl]
        after = ext_scr[HALO + 1:HALO + 1 + tm, sl]
        if latent:
            return (ext_scr[HALO:HALO + tm, sl] * (1.0 - mu) + (before * has_before) * coef[0]
                    + (after * has_after) * coef[1]
                    + ext_scr[0:tm, sl] * coef[2] + ext_scr[2 * HALO:2 * HALO + tm, sl] * coef[3])
        return ext_scr[HALO:HALO + tm, sl] * (1.0 - mu) + before * coef[0] + after * coef[1]

    def mixed(c0, c1):
        return jnp.concatenate([mixed_cols(c) for c in range(c0, c1)], axis=1)

    nb = D_A // LANES
    project(0, D_A)
    project(D_A, 2 * D_A)
    r = mixed(0, nb)
    project(2 * D_A, 3 * D_A)
    k = mixed(nb, 2 * nb)
    project(3 * D_A, SHIFT_W)
    v = mixed(2 * nb, 3 * nb)
    pool_o[...] = jnp.dot(h, w_ref[:, SHIFT_W:SHIFT_W + D_B], preferred_element_type=F32)
    gate_o[...] = jnp.dot(h, w_ref[:, SHIFT_W + D_B:], preferred_element_type=F32)
    w_down = mixed_cols(3 * nb)
    a_down = mixed_cols(3 * nb + 1)
    g_down = mixed_cols(3 * nb + 2)

    ones_bd = ones_ref[...]
    r_o[...] = r.astype(r_o.dtype)
    v_o[...] = v
    k_o[...] = k
    kkraw = k * kk_ref[...]
    ss = _head_sum(kkraw * kkraw, ones_bd)
    kk_o[...] = kkraw / jnp.maximum(jnp.sqrt(ss), 1e-12)
    bonus_o[...] = (_head_sum(r * k * rk_ref[...], ones_bd) * v).astype(bonus_o.dtype)
    gs_o[...] = _sigmoid(g_down)

    w_raw = _dot(jnp.tanh(w_down), wup_ref[...])
    a_raw = _dot(a_down, aup_ref[...])
    for d in range(2):
        sl = slice(d * D_A, (d + 1) * D_A)
        lw_o[d] = -DECAY_SCALE * _sigmoid(w0_ref[d:d + 1, :] + w_raw[:, sl])
        a_o[d] = _sigmoid(a0_ref[d:d + 1, :] + a_raw[:, sl])


def _prep(x, mod, mod_of_tile, p, seq_len, latent, tm, mi):
    rows = x.shape[0]
    assert seq_len % tm == 0 and tm % HALO == 0 and (latent or tm == seq_len)
    hb = tm // HALO
    n_halo_blocks = rows // HALO
    n = p['w_mix_in'].shape[1]
    row1 = lambda a: a.reshape(1, -1)
    full = lambda shape: pl.BlockSpec(shape, lambda i: (0,) * len(shape), pipeline_mode=pl.Buffered(1))
    out_cols = lambda width: pl.BlockSpec((tm, width), lambda i: (i, 0))
    out_dir = pl.BlockSpec((2, tm, D_A), lambda i: (0, i, 0))
    sds = jax.ShapeDtypeStruct
    return pl.pallas_call(
        functools.partial(_prep_kernel, seq_len=seq_len, latent=latent, tm=tm, mi=mi),
        grid=(rows // tm,),
        in_specs=[pl.BlockSpec((tm, D_MODEL), lambda i: (i, 0)),
                  pl.BlockSpec((HALO, D_MODEL), lambda i: (jnp.maximum(i * hb - 1, 0), 0)),
                  pl.BlockSpec((HALO, D_MODEL), lambda i: (jnp.minimum((i + 1) * hb, n_halo_blocks - 1), 0)),
                  pl.BlockSpec((None, N_MOD, D_MODEL), lambda i: (mod_of_tile(i, tm), 0, 0)),
                  full((D_MODEL, n)),
                  full((1, SHIFT_W)), full((LANES, 2 * D_A)), full((LANES, 2 * D_A)),
                  full((2, D_A)), full((2, D_A)), full((1, D_A)), full((1, D_A)),
                  full((LANES, LANES))],
        out_specs=[out_cols(D_A), out_cols(D_A), out_cols(D_A), out_cols(D_A), out_dir, out_dir, out_cols(D_A),
                   out_cols(LORA_G), out_cols(D_B), out_cols(n - SHIFT_W - D_B)],
        out_shape=[sds((rows, D_A), BF16), sds((rows, D_A), F32), sds((rows, D_A), F32), sds((rows, D_A), F32),
                   sds((2, rows, D_A), F32), sds((2, rows, D_A), F32),
                   sds((rows, D_A), BF16), sds((rows, LORA_G), F32),
                   sds((rows, D_B), F32), sds((rows, n - SHIFT_W - D_B), F32)],
        scratch_shapes=[pltpu.VMEM((tm + 2 * HALO, SHIFT_W), F32)],
        compiler_params=_cparams(("parallel",)),
        name="mix_in_prep",
    )(x, x, x, mod, p['w_mix_in'], row1(p['mu_shift']), p['wup_bd'], p['aup_bd'], p['w0'], p['a0'],
      row1(p['k_k']), row1(p['r_k']), p['ones_bd'])


SCAN_SPLITS = {
    'gram': (1, 1), 'akv': (1, 1), 'apply': (1, 1), 'pbt': (1, 1),
    'inv_a': (1, 1), 'inv_b': (1, 1),
    'trans': (1, 1), 'sadd': (1, 1), 'y_state': (1, 1), 's_state': (1, 1),
}
SCAN_GROUP = 8
SCAN_SEQS = 2


def _parts(x, n):
    return (x.astype(BF16),) if n == 1 else _split2(x)


def _mmp(ap, bp, dims=_NN):
    out = _dg(ap[0], bp[0], dims)
    if len(bp) > 1:
        out = out + _dg(ap[0], bp[1], dims)
    if len(ap) > 1:
        out = out + _dg(ap[1], bp[0], dims)
    return out


def _mms(site, a, b, dims=_NN):
    na, nb = SCAN_SPLITS[site]
    return _mmp(_parts(a, na), _parts(b, nb), dims)


def _scan_kernel(*refs, has_s0, want_state, group, nseq):
    if has_s0:
        (r_ref, v_ref, kk_ref, k_ref, lw_ref, a_ref, ka_ref, s0_ref), rest = refs[:8], refs[8:]
    else:
        (r_ref, v_ref, kk_ref, k_ref, lw_ref, a_ref, ka_ref), rest = refs[:7], refs[7:]
    if want_state:
        y_ref, sout_ref, s_scr = rest
    else:
        y_ref, s_scr = rest
    d = pl.program_id(0)
    c = pl.program_id(3)
    C = CHUNK

    chains = [(bi, p) for bi in range(nseq) for p in range(group)]
    keys = [(bi, slice(p * PAIR, (p + 1) * PAIR)) for bi, p in chains]

    @pl.when(c == 0)
    def _():
        if has_s0:
            z = jnp.zeros((HEAD_DIM, HEAD_DIM), F32)
            for j, (bi, p) in enumerate(chains):
                top = jnp.concatenate([s0_ref[bi, 2 * p], z], axis=1)
                bot = jnp.concatenate([z, s0_ref[bi, 2 * p + 1]], axis=1)
                s_scr[j] = jnp.concatenate([top, bot], axis=0)
        else:
            s_scr[...] = jnp.zeros_like(s_scr)

    row = lax.broadcasted_iota(jnp.int32, (PAIR, PAIR), 0)
    col = lax.broadcasted_iota(jnp.int32, (PAIR, PAIR), 1)
    same_head = _div_p2(row, HEAD_DIM) == _div_p2(col, HEAD_DIM)
    sign = 1 - 2 * d
    order = (_mod_p2(row, C) - _mod_p2(col, C)) * sign
    strict = (order > 0) & same_head
    incl = (order >= 0)[:C, :]
    head0 = lax.broadcasted_iota(jnp.int32, (C, PAIR), 1) < HEAD_DIM

    def stack(z):
        return jnp.concatenate([jnp.where(head0, z, 0.0), jnp.where(head0, 0.0, z)], axis=0)

    def each(fn, *lists):
        return [fn(*args) for args in zip(*lists)]

    trow = lax.broadcasted_iota(jnp.int32, (C, PAIR), 0)
    rev = d.astype(F32)

    def cumsum(lw, tot):
        acc = lw
        s = 1
        while s < C:
            acc = acc + jnp.where(trow >= s, pltpu.roll(acc, s, axis=0), 0.0)
            s *= 2
        return rev * (tot + lw) + (1.0 - 2.0 * rev) * acc

    def rd(ref, key):
        return ref[key[0], :, key[1]]

    lw = [rd(lw_ref, k) for k in keys]
    tot = [jnp.sum(z, axis=0, keepdims=True) for z in lw]
    cs = each(cumsum, lw, tot)
    w_inv = [jnp.exp(-z) for z in cs]
    w_rest = each(lambda t, z: jnp.exp(t - z), tot, cs)
    r_t = each(lambda k, z: rd(r_ref, k).astype(F32) * jnp.exp(z), keys, cs)
    kk = [rd(kk_ref, k) for k in keys]
    a = [rd(a_ref, k) for k in keys]
    kd = each(lambda k, ai: rd(k_ref, k) * (1.0 + (ai - 1.0) * ka_ref[:, k[1]]), keys, a)
    b = each(lambda ki, ai: ki * ai, kk, a)
    kk_st = each(lambda ki, z, l: stack(ki * jnp.exp(z - l)), kk, cs, lw)
    v_st = [stack(rd(v_ref, k)) for k in keys]
    kb_st = each(lambda kdi, bi, wi: jnp.concatenate([stack(kdi * wi), stack(bi * wi)], axis=0),
                 kd, b, w_inv)
    m = each(lambda a, rt, kb: _mms('gram', jnp.concatenate([a, rt], axis=0), kb, _NT), kk_st, r_t, kb_st)
    x = [jnp.where(strict, -z[:2 * C, 2 * C:], 0.0) for z in m]
    p_b = [jnp.where(incl, z[2 * C:, 2 * C:], 0.0) for z in m]
    apk_mask = jnp.concatenate([strict, incl], axis=0)
    apk_v = each(lambda z, vs: _mms('akv', jnp.where(apk_mask, z[:, :2 * C], 0.0), vs), m, v_st)

    ti, tj = _mod_p2(row, C), _mod_p2(col, C)
    eye = jnp.where(row == col, 1.0, 0.0)

    def off_block(n):
        return (_div_p2(ti, 2 * n) == _div_p2(tj, 2 * n)) & (_div_p2(ti, n) != _div_p2(tj, n))

    t_inv = [eye + jnp.where(off_block(1), z, 0.0) for z in x]
    n = 2
    while n < C:
        keep = off_block(n)
        mid = each(lambda z, t: _mms('inv_a', jnp.where(keep, z, 0.0), t), x, t_inv)
        t_inv = each(lambda t, mm: t + _mms('inv_b', t, mm), t_inv, mid)
        n *= 2
    pbt = each(lambda pb, t: _mms('pbt', pb, t), p_b, t_inv)
    qq = each(lambda t, pt, a, av: _mms('apply', jnp.concatenate([t, pt], axis=0),
                                        jnp.concatenate([a, av[:2 * C]], axis=1)),
              t_inv, pbt, kk_st, apk_v)
    q = [z[:2 * C] for z in qq]
    ry = each(lambda rt, av, z: jnp.concatenate([rt, av[2 * C:]], axis=1) - z[2 * C:], r_t, apk_v, qq)
    gq = each(lambda qi, bi, wr: _mms('trans', qi, stack(bi * wr), _TN), q, b, w_rest)
    s_add = each(lambda vs, kdi, wr, g: _mms('sadd', vs, stack(kdi * wr), _TN) - g[PAIR:],
                 v_st, kd, w_rest, gq)

    s_old = [s_scr[j] for j in range(len(chains))]
    for j, (bi, sl) in enumerate(keys):
        y_ref[bi, :, sl] = _mms('y_state', ry[j][:, :PAIR], s_old[j], _NT) + ry[j][:, PAIR:]
    s_new = each(lambda s, t, g, sa: s * jnp.exp(t) - _mms('s_state', s, g[:PAIR]) + sa, s_old, tot, gq, s_add)
    for j in range(len(chains)):
        s_scr[j] = s_new[j]

    if want_state:
        @pl.when(c == pl.num_programs(3) - 1)
        def _():
            for j, (bi, p) in enumerate(chains):
                sout_ref[bi, 2 * p] = s_new[j][:HEAD_DIM, :HEAD_DIM]
                sout_ref[bi, 2 * p + 1] = s_new[j][HEAD_DIM:, HEAD_DIM:]


def _scan(r, v, kk, k, lw, a, k_a, s0, batch, seq_len, want_state):
    rows = r.shape[0]
    nc = seq_len // CHUNK
    group = SCAN_GROUP
    nseq = min(SCAN_SEQS, batch)
    n_groups = D_A // (PAIR * group)
    gw = group * PAIR

    def chunk_idx(d, c):
        return c + d * (nc - 1 - 2 * c)

    shared = pl.BlockSpec((nseq, CHUNK, gw), lambda d, bi, p, c: (bi, chunk_idx(d, c), p))
    per_dir = pl.BlockSpec((None, nseq, CHUNK, gw), lambda d, bi, p, c: (d, bi, chunk_idx(d, c), p))
    state = pl.BlockSpec((nseq, None, 2 * group, HEAD_DIM, HEAD_DIM), lambda d, bi, p, c: (bi, d, p, 0, 0))
    in_specs = [shared, shared, shared, shared, per_dir, per_dir,
                pl.BlockSpec((1, gw), lambda d, bi, p, c: (0, p))]
    seqs = lambda z: z.reshape(z.shape[:-2] + (batch, seq_len, D_A))
    args = [seqs(z) for z in (r, v, kk, k, lw, a)] + [k_a.reshape(1, D_A)]
    if s0 is not None:
        in_specs.append(state)
        args.append(s0)
    out_specs = [per_dir]
    out_shape = [jax.ShapeDtypeStruct((2, batch, seq_len, D_A), F32)]
    if want_state:
        out_specs.append(state)
        out_shape.append(jax.ShapeDtypeStruct((batch, 2, N_HEADS, HEAD_DIM, HEAD_DIM), F32))
    outs = pl.pallas_call(
        functools.partial(_scan_kernel, has_s0=s0 is not None, want_state=want_state, group=group, nseq=nseq),
        grid=(2, batch // nseq, n_groups, nc),
        in_specs=in_specs,
        out_specs=out_specs,
        out_shape=out_shape,
        scratch_shapes=[pltpu.VMEM((nseq * group, PAIR, PAIR), F32)],
        compiler_params=_cparams(("parallel", "parallel", "parallel", "arbitrary")),
        name="rwkv7_scan",
    )(*args)
    return [outs[0].reshape(2, rows, D_A)] + list(outs[1:])


def _post_kernel(y_ref, bonus_ref, gs_ref, gup_ref, pool_ref, pprev_ref, pnext_ref, gates_ref, x_ref, mod_ref,
                 lnxg_ref, lnxb_ref, worwkv_ref, wpool_ref, pscale_ref, wopool_ref, wout_ref,
                 lng_ref, lnb_ref, ones_ref, band_ref, o_ref, ext_scr, *, seq_len, tm):
    i = pl.program_id(0)
    ones_bd = ones_ref[...]
    y = y_ref[0] + y_ref[1]
    mu = _head_sum(y, ones_bd) * (1.0 / HEAD_DIM)

    t0 = _mod_p2(i * tm, seq_len)
    lo_edge, hi_edge = POOL_PAD - POOL_HALO, POOL_PAD + tm + POOL_HALO
    ext_scr[0:lo_edge, :] = jnp.zeros((lo_edge, D_B), F32)
    ext_scr[lo_edge:POOL_PAD, :] = jnp.where(t0 == 0, 0.0, pprev_ref[...])
    ext_scr[POOL_PAD:POOL_PAD + tm, :] = pool_ref[...]
    ext_scr[POOL_PAD + tm:hi_edge, :] = jnp.where(t0 + tm == seq_len, 0.0, pnext_ref[...])
    ext_scr[hi_edge:, :] = jnp.zeros((lo_edge, D_B), F32)
    ext_hi, ext_lo = _split2(ext_scr[...])
    groups = [slice(gi * POOL_GROUP_DIM, (gi + 1) * POOL_GROUP_DIM) for gi in range(len(POOL_WINDOWS))]
    win = [jnp.dot(band_ref[gi], ext_hi[:, sl], preferred_element_type=F32)
           + jnp.dot(band_ref[gi], ext_lo[:, sl], preferred_element_type=F32) for gi, sl in enumerate(groups)]

    dlt = y - mu
    var = _head_sum(dlt * dlt, ones_bd) * (1.0 / HEAD_DIM)

    t = t0 + lax.broadcasted_iota(jnp.int32, (tm, LANES), 0)
    us = []
    for gi, w in enumerate(POOL_WINDOWS):
        cnt = jnp.minimum(t + (w - w // 2), seq_len) - jnp.maximum(t - w // 2, 0)
        resid = win[gi] / cnt.astype(F32) - pool_ref[:, groups[gi]]
        us.append(_dot(resid, wpool_ref[gi]))

    yn = dlt * lax.rsqrt(var + GN_EPS) * lnxg_ref[...] + lnxb_ref[...]
    ya = _dot((yn + bonus_ref[...].astype(F32)) * _dot(gs_ref[...], gup_ref[...]), worwkv_ref[...])
    u = jnp.concatenate(us, axis=1) * pscale_ref[...]
    yb = _dot(u, wopool_ref[...])

    gates = gates_ref[...]
    merged = _sigmoid(gates[:, :D_MODEL]) * ya + _sigmoid(gates[:, D_MODEL:]) * yb
    mix = _dot(merged, wout_ref[...])
    z = ALPHA * x_ref[...] + mod_ref[5:6, :] * mix
    o_ref[...] = _layer_norm(z, lng_ref[...], lnb_ref[...])


def _pool_bands(tm):
    i = lax.broadcasted_iota(jnp.int32, (tm, tm + 2 * POOL_PAD), 0)
    j = lax.broadcasted_iota(jnp.int32, (tm, tm + 2 * POOL_PAD), 1) - POOL_PAD
    return jnp.stack([((j >= i - w // 2) & (j < i + w - w // 2)).astype(BF16) for w in POOL_WINDOWS])


def _post(y, bonus, gs, pool_in, gates, x, mod, mod_of_tile, p, seq_len, tm):
    rows = x.shape[0]
    hb = tm // POOL_HALO
    n_halo_blocks = rows // POOL_HALO
    row1 = lambda a: a.reshape(1, -1)
    full = lambda shape: pl.BlockSpec(shape, lambda i: (0,) * len(shape))
    tile = lambda width: pl.BlockSpec((tm, width), lambda i: (i, 0))
    return pl.pallas_call(
        functools.partial(_post_kernel, seq_len=seq_len, tm=tm),
        grid=(rows // tm,),
        in_specs=[pl.BlockSpec((2, tm, D_A), lambda i: (0, i, 0)), tile(D_A), tile(LORA_G), full((LORA_G, D_A)),
                  tile(D_B),
                  pl.BlockSpec((POOL_HALO, D_B), lambda i: (jnp.maximum(i * hb - 1, 0), 0)),
                  pl.BlockSpec((POOL_HALO, D_B), lambda i: (jnp.minimum((i + 1) * hb, n_halo_blocks - 1), 0)),
                  tile(2 * D_MODEL), tile(D_MODEL),
                  pl.BlockSpec((None, N_MOD, D_MODEL), lambda i: (mod_of_tile(i, tm), 0, 0)),
                  full((1, D_A)), full((1, D_A)), full((D_A, D_MODEL)),
                  full((len(POOL_WINDOWS), POOL_GROUP_DIM, POOL_GROUP_DIM)), full((1, D_B)),
                  full((D_B, D_MODEL)), full((D_MODEL, D_MODEL)), full((1, D_MODEL)), full((1, D_MODEL)),
                  full((LANES, LANES)), full((len(POOL_WINDOWS), tm, tm + 2 * POOL_PAD))],
        out_specs=tile(D_MODEL),
        out_shape=jax.ShapeDtypeStruct((rows, D_MODEL), F32),
        scratch_shapes=[pltpu.VMEM((tm + 2 * POOL_PAD, D_B), F32)],
        compiler_params=_cparams(("parallel",)),
        name="mixer_out",
    )(y, bonus, gs, p['g_up'], pool_in, pool_in, pool_in, gates, x, mod, row1(p['lnx_g']), row1(p['lnx_b']),
      p['w_o_rwkv'], p['w_pool'], row1(p['pool_scale']), p['w_o_pool'], p['w_out'],
      row1(p['ln_g'][1]), row1(p['ln_b'][1]), p['ones_bd'], _pool_bands(tm))


def _trunk_layer(x, mod, mod_of_tile, s0, batch, seq_len, latent, want_state, p):
    tm = 256
    x1 = _ffn(x, mod, mod_of_tile, p['ffn_in'], p['ffn_out'], p['ln_g'][0], p['ln_b'][0], 0, 0, 512)
    r, v, kk, k, lw, a, bonus, gs, pool_in, gates = _prep(x1, mod, mod_of_tile, p, seq_len, latent, tm, 3)
    outs = _scan(r, v, kk, k, lw, a, p['k_a'], s0, batch, seq_len, want_state)
    y = outs[0]
    x2 = _post(y, bonus, gs, pool_in, gates, x1, mod, mod_of_tile, p, seq_len, tm)
    x3 = _ffn(x2, mod, mod_of_tile, p['ffn_in'], p['ffn_out'], p['ln_g'][2], p['ln_b'][2], 1, 6, 512)
    return x3, (outs[1] if want_state else None)


def _block_diag2(w):
    z = jnp.zeros_like(w[0])
    return jnp.concatenate([jnp.concatenate([w[0], z], axis=1), jnp.concatenate([z, w[1]], axis=1)], axis=0)


def kernel(x_prompt, x_sample, c, state_rwkv, c_ctx, w_mod, b_mod, ln_g, ln_b, ffn_in, ffn_out, w_mix_in,
           mu_shift, w0, w_up, a0, a_up, g_up, k_k, k_a, r_k, lnx_g, lnx_b, w_o_rwkv, w_pool, pool_scale,
           w_o_pool, w_out):
    batch, seq, _ = x_prompt.shape
    dec_batch, dec_seq, _ = x_sample.shape
    y_p = x_prompt.reshape(batch * seq, D_MODEL)
    y_s = x_sample.reshape(dec_batch * dec_seq, D_MODEL)
    cvecs = jnp.concatenate([c_ctx[None, :], c, jnp.zeros((8 - 1 - dec_batch, D_MODEL), F32)], axis=0)
    ii = lax.broadcasted_iota(jnp.int32, (LANES, LANES), 0) // HEAD_DIM
    jj = lax.broadcasted_iota(jnp.int32, (LANES, LANES), 1) // HEAD_DIM
    ones_bd = (ii == jj).astype(BF16)
    ctx_states = []
    for l in range(DEPTH):
        p = {
            'ln_g': ln_g[l], 'ln_b': ln_b[l],
            'ffn_in': ffn_in[l].astype(BF16), 'ffn_out': ffn_out[l].astype(BF16),
            'w_mix_in': w_mix_in[l].astype(BF16),
            'mu_shift': mu_shift[l], 'w0': w0[l], 'a0': a0[l],
            'wup_bd': _block_diag2(w_up[l]).astype(BF16), 'aup_bd': _block_diag2(a_up[l]).astype(BF16),
            'g_up': g_up[l].astype(BF16), 'k_k': k_k[l], 'k_a': k_a[l], 'r_k': r_k[l],
            'lnx_g': lnx_g[l], 'lnx_b': lnx_b[l], 'w_o_rwkv': w_o_rwkv[l].astype(BF16),
            'w_pool': w_pool[l].astype(BF16), 'pool_scale': pool_scale[l],
            'w_o_pool': w_o_pool[l].astype(BF16), 'w_out': w_out[l].astype(BF16), 'ones_bd': ones_bd,
        }
        mod = _modulation(cvecs, w_mod[l], b_mod[l]).reshape(8, N_MOD, D_MODEL)
        y_p, s_ctx = _trunk_layer(y_p, mod, lambda i, tm: 0, None, batch, seq, False, True, p)
        ctx_states.append(s_ctx)
        y_s, _ = _trunk_layer(y_s, mod, lambda i, tm: 1 + (i * tm) // dec_seq, state_rwkv[:, l],
                              dec_batch, dec_seq, True, False, p)
    new_state = jnp.stack(ctx_states, axis=1).astype(x_prompt.dtype)
    return (y_p.reshape(batch, seq, D_MODEL), y_s.reshape(dec_batch, dec_seq, D_MODEL), new_state)
```

```python
import functools
import math

import jax
import jax.numpy as jnp
from jax import lax
from jax.experimental import pallas as pl
from jax.experimental.pallas import tpu as pltpu

F32 = jnp.float32
BF16 = jnp.bfloat16

D_MODEL = 1024
DEPTH = 1
GRID_W = 64
HEAD_DIM = 64
D_A = D_MODEL
N_HEADS = D_A // HEAD_DIM
D_B = D_MODEL // 2
POOL_WINDOWS = (2, 4, 8, 16)
POOL_GROUP_DIM = D_B // len(POOL_WINDOWS)
LORA_W = 64
LORA_A = 64
LORA_G = 128
D_FF = 2816
N_MOD = 9
SHIFT_W = 3 * D_A + 2 * LORA_W + 2 * LORA_A + LORA_G
ALPHA = (2 * DEPTH) ** 0.25
LN_EPS = 1e-5
GN_EPS = 64e-5
DECAY_SCALE = math.exp(-0.5)

LANES = 128
HALO = 64
POOL_HALO = 8
POOL_PAD = 64
CHUNK = 64
PAIR = 2 * HEAD_DIM
VMEM_LIMIT = 56 * 1024 * 1024


def _cparams(sem):
    return pltpu.CompilerParams(dimension_semantics=sem, vmem_limit_bytes=VMEM_LIMIT)


def _mod_p2(x, n):
    assert n & (n - 1) == 0
    return x & (n - 1)


def _div_p2(x, n):
    assert n & (n - 1) == 0
    return x >> (n.bit_length() - 1)


def _sigmoid(x):
    return 0.5 * jnp.tanh(0.5 * x) + 0.5


def _silu(x):
    return x / (1.0 + jnp.exp(-x))


def _dot(a, b):
    return jnp.dot(a.astype(BF16), b.astype(BF16), preferred_element_type=F32)


def _split2(x):
    hi = x.astype(BF16)
    lo = (x - hi.astype(F32)).astype(BF16)
    return hi, lo


def _split3(x):
    hi = x.astype(BF16)
    r1 = x - hi.astype(F32)
    mid = r1.astype(BF16)
    lo = (r1 - mid.astype(F32)).astype(BF16)
    return hi, mid, lo


_NN = (((1,), (0,)), ((), ()))
_NT = (((1,), (1,)), ((), ()))
_TN = (((0,), (0,)), ((), ()))


def _dg(a, b, dims):
    return lax.dot_general(a, b, dims, preferred_element_type=F32)


def _mm3(a, b, dims=_NN):
    ah, al = _split2(a)
    bh, bl = _split2(b)
    return _dg(ah, bh, dims) + (_dg(ah, bl, dims) + _dg(al, bh, dims))


def _layer_norm(z, g, b):
    mu = jnp.mean(z, axis=-1, keepdims=True)
    d = z - mu
    var = jnp.mean(d * d, axis=-1, keepdims=True)
    return d * lax.rsqrt(var + LN_EPS) * g + b


def _head_sum(x, ones_bd):
    xb = x.astype(BF16)
    return jnp.concatenate([jnp.dot(xb[:, c * LANES:(c + 1) * LANES], ones_bd, preferred_element_type=F32)
                            for c in range(x.shape[1] // LANES)], axis=1)


def _mod_kernel(c_ref, w_ref, b_ref, o_ref):
    o_ref[...] = _dot(_silu(c_ref[...]), w_ref[...]) + b_ref[...]


def _modulation(cvecs, w_mod, b_mod):
    rows = cvecs.shape[0]
    n = w_mod.shape[1]
    tn = 1024
    return pl.pallas_call(
        _mod_kernel,
        grid=(n // tn,),
        in_specs=[pl.BlockSpec((rows, D_MODEL), lambda j: (0, 0)),
                  pl.BlockSpec((D_MODEL, tn), lambda j: (0, j)),
                  pl.BlockSpec((1, tn), lambda j: (0, j))],
        out_specs=pl.BlockSpec((rows, tn), lambda j: (0, j)),
        out_shape=jax.ShapeDtypeStruct((rows, n), F32),
        compiler_params=_cparams(("arbitrary",)),
        name="modulation",
    )(cvecs, w_mod, b_mod.reshape(1, n))


def _ffn_kernel(x_ref, mod_ref, wi_ref, wo_ref, lng_ref, lnb_ref, o_ref, *, mi):
    x = x_ref[...]
    h = (x * (1.0 + mod_ref[mi + 1:mi + 2, :]) + mod_ref[mi:mi + 1, :]).astype(BF16)
    gate = jnp.dot(h, wi_ref[:, :D_FF], preferred_element_type=F32)
    up = jnp.dot(h, wi_ref[:, D_FF:], preferred_element_type=F32)
    ff = _dot(_silu(gate) * up, wo_ref[...])
    z = ALPHA * x + 0.5 * mod_ref[mi + 2:mi + 3, :] * ff
    o_ref[...] = _layer_norm(z, lng_ref[...], lnb_ref[...])


def _ffn(x, mod, mod_of_tile, w_in, w_out, ln_g, ln_b, which, mi, tm):
    rows = x.shape[0]
    resident = lambda shape: pl.BlockSpec(shape, lambda i: (0,) * len(shape), pipeline_mode=pl.Buffered(1))
    picked = lambda shape: pl.BlockSpec((None,) + shape, lambda i: (which, 0, 0), pipeline_mode=pl.Buffered(1))
    return pl.pallas_call(
        functools.partial(_ffn_kernel, mi=mi),
        grid=(rows // tm,),
        in_specs=[pl.BlockSpec((tm, D_MODEL), lambda i: (i, 0)),
                  pl.BlockSpec((None, N_MOD, D_MODEL), lambda i: (mod_of_tile(i, tm), 0, 0)),
                  picked((D_MODEL, 2 * D_FF)), picked((D_FF, D_MODEL)),
                  resident((1, D_MODEL)), resident((1, D_MODEL))],
        out_specs=pl.BlockSpec((tm, D_MODEL), lambda i: (i, 0)),
        out_shape=jax.ShapeDtypeStruct((rows, D_MODEL), F32),
        compiler_params=_cparams(("parallel",)),
        name="ffn",
    )(x, mod, w_in, w_out, ln_g.reshape(1, -1), ln_b.reshape(1, -1))


def _prep_kernel(x_ref, xprev_ref, xnext_ref, mod_ref, w_ref, mu_ref, wup_ref, aup_ref, w0_ref, a0_ref,
                 kk_ref, ka_ref, rk_ref, ones_ref,
                 r_o, v_o, kk_o, lw_o, kd_o, b_o, bonus_o, gs_o, pool_o, gate_o, ext_scr,
                 *, seq_len, latent, tm, mi):
    i = pl.program_id(0)
    t0 = _mod_p2(i * tm, seq_len)
    shift = mod_ref[mi:mi + 1, :]
    scale = mod_ref[mi + 1:mi + 2, :]

    def modulated(ref):
        return (ref[...] * (1.0 + scale) + shift).astype(BF16)

    h = modulated(x_ref)
    h_ext = jnp.concatenate([modulated(xprev_ref), h, modulated(xnext_ref)], axis=0) if latent else None

    def project(c0, c1):
        if latent:
            ext_scr[:, c0:c1] = jnp.dot(h_ext, w_ref[:, c0:c1], preferred_element_type=F32)
            ext_scr[0:HALO, c0:c1] = jnp.where(t0 == 0, 0.0, ext_scr[0:HALO, c0:c1])
            ext_scr[HALO + tm:, c0:c1] = jnp.where(t0 + tm == seq_len, 0.0, ext_scr[HALO + tm:, c0:c1])
        else:
            ext_scr[0:HALO, c0:c1] = jnp.zeros((HALO, c1 - c0), F32)
            ext_scr[HALO:HALO + tm, c0:c1] = jnp.dot(h, w_ref[:, c0:c1], preferred_element_type=F32)
            ext_scr[HALO + tm:, c0:c1] = jnp.zeros((HALO, c1 - c0), F32)

    t = t0 + lax.broadcasted_iota(jnp.int32, (tm, LANES), 0)
    lane = lax.broadcasted_iota(jnp.int32, (1, LANES), 1)
    n_src = 4 if latent else 2
    if latent:
        col = _mod_p2(t, GRID_W)
        has_before = jnp.where(col > 0, 1.0, 0.0)
        has_after = jnp.where(col < GRID_W - 1, 1.0, 0.0)

    def mixed_cols(c):
        sl = slice(c * LANES, (c + 1) * LANES)
        mu = mu_ref[:, sl]
        coef = [jnp.where(_mod_p2(lane, n_src) == s, mu, 0.0) for s in range(n_src)]
        before = ext_scr[HALO - 1:HALO - 1 + tm, sl]
        after = ext_scr[HALO + 1:HALO + 1 + tm, sl]
        if latent:
            return (ext_scr[HALO:HALO + tm, sl] * (1.0 - mu) + (before * has_before) * coef[0]
                    + (after * has_after) * coef[1]
                    + ext_scr[0:tm, sl] * coef[2] + ext_scr[2 * HALO:2 * HALO + tm, sl] * coef[3])
        return ext_scr[HALO:HALO + tm, sl] * (1.0 - mu) + before * coef[0] + after * coef[1]

    def mixed(c0, c1):
        return jnp.concatenate([mixed_cols(c) for c in range(c0, c1)], axis=1)

    nb = D_A // LANES
    project(0, D_A)
    project(D_A, 2 * D_A)
    r = mixed(0, nb)
    project(2 * D_A, 3 * D_A)
    k = mixed(nb, 2 * nb)
    project(3 * D_A, SHIFT_W)
    v = mixed(2 * nb, 3 * nb)
    pool_o[...] = jnp.dot(h, w_ref[:, SHIFT_W:SHIFT_W + D_B], preferred_element_type=F32)
    gate_o[...] = jnp.dot(h, w_ref[:, SHIFT_W + D_B:], preferred_element_type=F32)
    w_down = mixed_cols(3 * nb)
    a_down = mixed_cols(3 * nb + 1)
    g_down = mixed_cols(3 * nb + 2)

    ones_bd = ones_ref[...]
    r_o[...] = r.astype(r_o.dtype)
    v_o[...] = v
    kkraw = k * kk_ref[...]
    ss = _head_sum(kkraw * kkraw, ones_bd)
    kk = kkraw / jnp.maximum(jnp.sqrt(ss), 1e-12)
    kk_o[...] = kk
    bonus_o[...] = (_head_sum(r * k * rk_ref[...], ones_bd) * v).astype(bonus_o.dtype)
    gs_o[...] = _sigmoid(g_down)

    w_raw = _dot(jnp.tanh(w_down), wup_ref[...])
    a_raw = _dot(a_down, aup_ref[...])
    for d in range(2):
        sl = slice(d * D_A, (d + 1) * D_A)
        lw_o[d] = -DECAY_SCALE * _sigmoid(w0_ref[d:d + 1, :] + w_raw[:, sl])
        a = _sigmoid(a0_ref[d:d + 1, :] + a_raw[:, sl])
        kd_o[d] = k * (1.0 + (a - 1.0) * ka_ref[...])
        b_o[d] = kk * a


def _prep(x, mod, mod_of_tile, p, seq_len, latent, tm, mi):
    rows = x.shape[0]
    assert seq_len % tm == 0 and tm % HALO == 0 and (latent or tm == seq_len)
    hb = tm // HALO
    n_halo_blocks = rows // HALO
    n = p['w_mix_in'].shape[1]
    row1 = lambda a: a.reshape(1, -1)
    full = lambda shape: pl.BlockSpec(shape, lambda i: (0,) * len(shape), pipeline_mode=pl.Buffered(1))
    out_cols = lambda width: pl.BlockSpec((tm, width), lambda i: (i, 0))
    out_dir = pl.BlockSpec((2, tm, D_A), lambda i: (0, i, 0))
    sds = jax.ShapeDtypeStruct
    return pl.pallas_call(
        functools.partial(_prep_kernel, seq_len=seq_len, latent=latent, tm=tm, mi=mi),
        grid=(rows // tm,),
        in_specs=[pl.BlockSpec((tm, D_MODEL), lambda i: (i, 0)),
                  pl.BlockSpec((HALO, D_MODEL), lambda i: (jnp.maximum(i * hb - 1, 0), 0)),
                  pl.BlockSpec((HALO, D_MODEL), lambda i: (jnp.minimum((i + 1) * hb, n_halo_blocks - 1), 0)),
                  pl.BlockSpec((None, N_MOD, D_MODEL), lambda i: (mod_of_tile(i, tm), 0, 0)),
                  full((D_MODEL, n)),
                  full((1, SHIFT_W)), full((LANES, 2 * D_A)), full((LANES, 2 * D_A)),
                  full((2, D_A)), full((2, D_A)), full((1, D_A)), full((1, D_A)), full((1, D_A)),
                  full((LANES, LANES))],
        out_specs=[out_cols(D_A), out_cols(D_A), out_cols(D_A), out_dir, out_dir, out_dir, out_cols(D_A),
                   out_cols(LORA_G), out_cols(D_B), out_cols(n - SHIFT_W - D_B)],
        out_shape=[sds((rows, D_A), BF16), sds((rows, D_A), F32), sds((rows, D_A), F32),
                   sds((2, rows, D_A), F32), sds((2, rows, D_A), F32), sds((2, rows, D_A), F32),
                   sds((rows, D_A), BF16), sds((rows, LORA_G), F32),
                   sds((rows, D_B), F32), sds((rows, n - SHIFT_W - D_B), F32)],
        scratch_shapes=[pltpu.VMEM((tm + 2 * HALO, SHIFT_W), F32)],
        compiler_params=_cparams(("parallel",)),
        name="mix_in_prep",
    )(x, x, x, mod, p['w_mix_in'], row1(p['mu_shift']), p['wup_bd'], p['aup_bd'], p['w0'], p['a0'],
      row1(p['k_k']), row1(p['k_a']), row1(p['r_k']), p['ones_bd'])


SCAN_SPLITS = {
    'gram': (1, 1), 'akv': (1, 1), 'apply': (1, 1), 'pbt': (1, 1),
    'inv_a': (1, 1), 'inv_b': (1, 1),
    'trans': (1, 1), 'sadd': (1, 1), 'y_state': (1, 1), 's_state': (1, 1),
}
SCAN_GROUP = 8
SCAN_SEQS = 2


def _parts(x, n):
    return (x.astype(BF16),) if n == 1 else _split2(x)


def _mmp(ap, bp, dims=_NN):
    out = _dg(ap[0], bp[0], dims)
    if len(bp) > 1:
        out = out + _dg(ap[0], bp[1], dims)
    if len(ap) > 1:
        out = out + _dg(ap[1], bp[0], dims)
    return out


def _mms(site, a, b, dims=_NN):
    na, nb = SCAN_SPLITS[site]
    return _mmp(_parts(a, na), _parts(b, nb), dims)


def _scan_kernel(*refs, has_s0, want_state, group, nseq):
    if has_s0:
        (r_ref, v_ref, kk_ref, lw_ref, kd_ref, b_ref, s0_ref), rest = refs[:7], refs[7:]
    else:
        (r_ref, v_ref, kk_ref, lw_ref, kd_ref, b_ref), rest = refs[:6], refs[6:]
    if want_state:
        y_ref, sout_ref, s_scr = rest
    else:
        y_ref, s_scr = rest
    d = pl.program_id(0)
    c = pl.program_id(3)
    C = CHUNK

    chains = [(bi, p) for bi in range(nseq) for p in range(group)]
    keys = [(bi, slice(p * PAIR, (p + 1) * PAIR)) for bi, p in chains]

    @pl.when(c == 0)
    def _():
        if has_s0:
            z = jnp.zeros((HEAD_DIM, HEAD_DIM), F32)
            for j, (bi, p) in enumerate(chains):
                top = jnp.concatenate([s0_ref[bi, 2 * p], z], axis=1)
                bot = jnp.concatenate([z, s0_ref[bi, 2 * p + 1]], axis=1)
                s_scr[j] = jnp.concatenate([top, bot], axis=0)
        else:
            s_scr[...] = jnp.zeros_like(s_scr)

    row = lax.broadcasted_iota(jnp.int32, (PAIR, PAIR), 0)
    col = lax.broadcasted_iota(jnp.int32, (PAIR, PAIR), 1)
    same_head = _div_p2(row, HEAD_DIM) == _div_p2(col, HEAD_DIM)
    sign = 1 - 2 * d
    order = (_mod_p2(row, C) - _mod_p2(col, C)) * sign
    strict = (order > 0) & same_head
    incl = (order >= 0)[:C, :]
    head0 = lax.broadcasted_iota(jnp.int32, (C, PAIR), 1) < HEAD_DIM

    def stack(z):
        return jnp.concatenate([jnp.where(head0, z, 0.0), jnp.where(head0, 0.0, z)], axis=0)

    def each(fn, *lists):
        return [fn(*args) for args in zip(*lists)]

    trow = lax.broadcasted_iota(jnp.int32, (C, PAIR), 0)
    rev = d.astype(F32)

    def cumsum(lw, tot):
        acc = lw
        s = 1
        while s < C:
            acc = acc + jnp.where(trow >= s, pltpu.roll(acc, s, axis=0), 0.0)
            s *= 2
        return rev * (tot + lw) + (1.0 - 2.0 * rev) * acc

    def rd(ref, key):
        return ref[key[0], :, key[1]]

    lw = [rd(lw_ref, k) for k in keys]
    tot = [jnp.sum(z, axis=0, keepdims=True) for z in lw]
    cs = each(cumsum, lw, tot)
    w_inv = [jnp.exp(-z) for z in cs]
    w_rest = each(lambda t, z: jnp.exp(t - z), tot, cs)
    r_t = each(lambda k, z: rd(r_ref, k).astype(F32) * jnp.exp(z), keys, cs)
    kd = [rd(kd_ref, k) for k in keys]
    b = [rd(b_ref, k) for k in keys]
    kk_st = each(lambda k, z, l: stack(rd(kk_ref, k) * jnp.exp(z - l)), keys, cs, lw)
    v_st = [stack(rd(v_ref, k)) for k in keys]
    kb_st = each(lambda kdi, bi, wi: jnp.concatenate([stack(kdi * wi), stack(bi * wi)], axis=0),
                 kd, b, w_inv)
    m = each(lambda a, rt, kb: _mms('gram', jnp.concatenate([a, rt], axis=0), kb, _NT), kk_st, r_t, kb_st)
    x = [jnp.where(strict, -z[:2 * C, 2 * C:], 0.0) for z in m]
    p_b = [jnp.where(incl, z[2 * C:, 2 * C:], 0.0) for z in m]
    apk_mask = jnp.concatenate([strict, incl], axis=0)
    apk_v = each(lambda z, vs: _mms('akv', jnp.where(apk_mask, z[:, :2 * C], 0.0), vs), m, v_st)

    ti, tj = _mod_p2(row, C), _mod_p2(col, C)
    eye = jnp.where(row == col, 1.0, 0.0)

    def off_block(n):
        return (_div_p2(ti, 2 * n) == _div_p2(tj, 2 * n)) & (_div_p2(ti, n) != _div_p2(tj, n))

    t_inv = [eye + jnp.where(off_block(1), z, 0.0) for z in x]
    n = 2
    while n < C:
        keep = off_block(n)
        mid = each(lambda z, t: _mms('inv_a', jnp.where(keep, z, 0.0), t), x, t_inv)
        t_inv = each(lambda t, mm: t + _mms('inv_b', t, mm), t_inv, mid)
        n *= 2
    pbt = each(lambda pb, t: _mms('pbt', pb, t), p_b, t_inv)
    qq = each(lambda t, pt, a, av: _mms('apply', jnp.concatenate([t, pt], axis=0),
                                        jnp.concatenate([a, av[:2 * C]], axis=1)),
              t_inv, pbt, kk_st, apk_v)
    q = [z[:2 * C] for z in qq]
    ry = each(lambda rt, av, z: jnp.concatenate([rt, av[2 * C:]], axis=1) - z[2 * C:], r_t, apk_v, qq)
    gq = each(lambda qi, bi, wr: _mms('trans', qi, stack(bi * wr), _TN), q, b, w_rest)
    s_add = each(lambda vs, kdi, wr, g: _mms('sadd', vs, stack(kdi * wr), _TN) - g[PAIR:],
                 v_st, kd, w_rest, gq)

    s_old = [s_scr[j] for j in range(len(chains))]
    for j, (bi, sl) in enumerate(keys):
        y_ref[bi, :, sl] = _mms('y_state', ry[j][:, :PAIR], s_old[j], _NT) + ry[j][:, PAIR:]
    s_new = each(lambda s, t, g, sa: s * jnp.exp(t) - _mms('s_state', s, g[:PAIR]) + sa, s_old, tot, gq, s_add)
    for j in range(len(chains)):
        s_scr[j] = s_new[j]

    if want_state:
        @pl.when(c == pl.num_programs(3) - 1)
        def _():
            for j, (bi, p) in enumerate(chains):
                sout_ref[bi, 2 * p] = s_new[j][:HEAD_DIM, :HEAD_DIM]
                sout_ref[bi, 2 * p + 1] = s_new[j][HEAD_DIM:, HEAD_DIM:]


def _scan(r, v, kk, lw, kd, b, s0, batch, seq_len, want_state):
    rows = r.shape[0]
    nc = seq_len // CHUNK
    group = SCAN_GROUP
    nseq = min(SCAN_SEQS, batch)
    n_groups = D_A // (PAIR * group)
    gw = group * PAIR

    def chunk_idx(d, c):
        return c + d * (nc - 1 - 2 * c)

    shared = pl.BlockSpec((nseq, CHUNK, gw), lambda d, bi, p, c: (bi, chunk_idx(d, c), p))
    per_dir = pl.BlockSpec((None, nseq, CHUNK, gw), lambda d, bi, p, c: (d, bi, chunk_idx(d, c), p))
    state = pl.BlockSpec((nseq, None, 2 * group, HEAD_DIM, HEAD_DIM), lambda d, bi, p, c: (bi, d, p, 0, 0))
    in_specs = [shared, shared, shared, per_dir, per_dir, per_dir]
    seqs = lambda z: z.reshape(z.shape[:-2] + (batch, seq_len, D_A))
    args = [seqs(z) for z in (r, v, kk, lw, kd, b)]
    if s0 is not None:
        in_specs.append(state)
        args.append(s0)
    out_specs = [per_dir]
    out_shape = [jax.ShapeDtypeStruct((2, batch, seq_len, D_A), F32)]
    if want_state:
        out_specs.append(state)
        out_shape.append(jax.ShapeDtypeStruct((batch, 2, N_HEADS, HEAD_DIM, HEAD_DIM), F32))
    outs = pl.pallas_call(
        functools.partial(_scan_kernel, has_s0=s0 is not None, want_state=want_state, group=group, nseq=nseq),
        grid=(2, batch // nseq, n_groups, nc),
        in_specs=in_specs,
        out_specs=out_specs,
        out_shape=out_shape,
        scratch_shapes=[pltpu.VMEM((nseq * group, PAIR, PAIR), F32)],
        compiler_params=_cparams(("parallel", "parallel", "parallel", "arbitrary")),
        name="rwkv7_scan",
    )(*args)
    return [outs[0].reshape(2, rows, D_A)] + list(outs[1:])


def _post_kernel(y_ref, bonus_ref, gs_ref, gup_ref, pool_ref, pprev_ref, pnext_ref, gates_ref, x_ref, mod_ref,
                 lnxg_ref, lnxb_ref, worwkv_ref, wpool_ref, pscale_ref, wopool_ref, wout_ref,
                 lng_ref, lnb_ref, ones_ref, band_ref, o_ref, ext_scr, *, seq_len, tm):
    i = pl.program_id(0)
    ones_bd = ones_ref[...]
    y = y_ref[0] + y_ref[1]
    mu = _head_sum(y, ones_bd) * (1.0 / HEAD_DIM)

    t0 = _mod_p2(i * tm, seq_len)
    lo_edge, hi_edge = POOL_PAD - POOL_HALO, POOL_PAD + tm + POOL_HALO
    ext_scr[0:lo_edge, :] = jnp.zeros((lo_edge, D_B), F32)
    ext_scr[lo_edge:POOL_PAD, :] = jnp.where(t0 == 0, 0.0, pprev_ref[...])
    ext_scr[POOL_PAD:POOL_PAD + tm, :] = pool_ref[...]
    ext_scr[POOL_PAD + tm:hi_edge, :] = jnp.where(t0 + tm == seq_len, 0.0, pnext_ref[...])
    ext_scr[hi_edge:, :] = jnp.zeros((lo_edge, D_B), F32)
    ext = ext_scr[...].astype(BF16)
    groups = [slice(gi * POOL_GROUP_DIM, (gi + 1) * POOL_GROUP_DIM) for gi in range(len(POOL_WINDOWS))]
    win = [jnp.dot(band_ref[gi], ext[:, sl], preferred_element_type=F32) for gi, sl in enumerate(groups)]

    dlt = y - mu
    var = _head_sum(dlt * dlt, ones_bd) * (1.0 / HEAD_DIM)

    t = t0 + lax.broadcasted_iota(jnp.int32, (tm, LANES), 0)
    us = []
    for gi, w in enumerate(POOL_WINDOWS):
        cnt = jnp.minimum(t + (w - w // 2), seq_len) - jnp.maximum(t - w // 2, 0)
        resid = win[gi] / cnt.astype(F32) - pool_ref[:, groups[gi]]
        us.append(_dot(resid, wpool_ref[gi]))

    yn = dlt * lax.rsqrt(var + GN_EPS) * lnxg_ref[...] + lnxb_ref[...]
    ya = _dot((yn + bonus_ref[...].astype(F32)) * _dot(gs_ref[...], gup_ref[...]), worwkv_ref[...])
    u = jnp.concatenate(us, axis=1) * pscale_ref[...]
    yb = _dot(u, wopool_ref[...])

    gates = gates_ref[...]
    merged = _sigmoid(gates[:, :D_MODEL]) * ya + _sigmoid(gates[:, D_MODEL:]) * yb
    mix = _dot(merged, wout_ref[...])
    z = ALPHA * x_ref[...] + mod_ref[5:6, :] * mix
    o_ref[...] = _layer_norm(z, lng_ref[...], lnb_ref[...])


def _pool_bands(tm):
    i = lax.broadcasted_iota(jnp.int32, (tm, tm + 2 * POOL_PAD), 0)
    j = lax.broadcasted_iota(jnp.int32, (tm, tm + 2 * POOL_PAD), 1) - POOL_PAD
    return jnp.stack([((j >= i - w // 2) & (j < i + w - w // 2)).astype(BF16) for w in POOL_WINDOWS])


def _post(y, bonus, gs, pool_in, gates, x, mod, mod_of_tile, p, seq_len, tm):
    rows = x.shape[0]
    hb = tm // POOL_HALO
    n_halo_blocks = rows // POOL_HALO
    row1 = lambda a: a.reshape(1, -1)
    full = lambda shape: pl.BlockSpec(shape, lambda i: (0,) * len(shape))
    tile = lambda width: pl.BlockSpec((tm, width), lambda i: (i, 0))
    return pl.pallas_call(
        functools.partial(_post_kernel, seq_len=seq_len, tm=tm),
        grid=(rows // tm,),
        in_specs=[pl.BlockSpec((2, tm, D_A), lambda i: (0, i, 0)), tile(D_A), tile(LORA_G), full((LORA_G, D_A)),
                  tile(D_B),
                  pl.BlockSpec((POOL_HALO, D_B), lambda i: (jnp.maximum(i * hb - 1, 0), 0)),
                  pl.BlockSpec((POOL_HALO, D_B), lambda i: (jnp.minimum((i + 1) * hb, n_halo_blocks - 1), 0)),
                  tile(2 * D_MODEL), tile(D_MODEL),
                  pl.BlockSpec((None, N_MOD, D_MODEL), lambda i: (mod_of_tile(i, tm), 0, 0)),
                  full((1, D_A)), full((1, D_A)), full((D_A, D_MODEL)),
                  full((len(POOL_WINDOWS), POOL_GROUP_DIM, POOL_GROUP_DIM)), full((1, D_B)),
                  full((D_B, D_MODEL)), full((D_MODEL, D_MODEL)), full((1, D_MODEL)), full((1, D_MODEL)),
                  full((LANES, LANES)), full((len(POOL_WINDOWS), tm, tm + 2 * POOL_PAD))],
        out_specs=tile(D_MODEL),
        out_shape=jax.ShapeDtypeStruct((rows, D_MODEL), F32),
        scratch_shapes=[pltpu.VMEM((tm + 2 * POOL_PAD, D_B), F32)],
        compiler_params=_cparams(("parallel",)),
        name="mixer_out",
    )(y, bonus, gs, p['g_up'], pool_in, pool_in, pool_in, gates, x, mod, row1(p['lnx_g']), row1(p['lnx_b']),
      p['w_o_rwkv'], p['w_pool'], row1(p['pool_scale']), p['w_o_pool'], p['w_out'],
      row1(p['ln_g'][1]), row1(p['ln_b'][1]), p['ones_bd'], _pool_bands(tm))


def _trunk_layer(x, mod, mod_of_tile, s0, batch, seq_len, latent, want_state, p):
    tm = 256
    x1 = _ffn(x, mod, mod_of_tile, p['ffn_in'], p['ffn_out'], p['ln_g'][0], p['ln_b'][0], 0, 0, 512)
    r, v, kk, lw, kd, b, bonus, gs, pool_in, gates = _prep(x1, mod, mod_of_tile, p, seq_len, latent, tm, 3)
    outs = _scan(r, v, kk, lw, kd, b, s0, batch, seq_len, want_state)
    y = outs[0]
    x2 = _post(y, bonus, gs, pool_in, gates, x1, mod, mod_of_tile, p, seq_len, tm)
    x3 = _ffn(x2, mod, mod_of_tile, p['ffn_in'], p['ffn_out'], p['ln_g'][2], p['ln_b'][2], 1, 6, 512)
    return x3, (outs[1] if want_state else None)


def _block_diag2(w):
    z = jnp.zeros_like(w[0])
    return jnp.concatenate([jnp.concatenate([w[0], z], axis=1), jnp.concatenate([z, w[1]], axis=1)], axis=0)


def kernel(x_prompt, x_sample, c, state_rwkv, c_ctx, w_mod, b_mod, ln_g, ln_b, ffn_in, ffn_out, w_mix_in,
           mu_shift, w0, w_up, a0, a_up, g_up, k_k, k_a, r_k, lnx_g, lnx_b, w_o_rwkv, w_pool, pool_scale,
           w_o_pool, w_out):
    batch, seq, _ = x_prompt.shape
    dec_batch, dec_seq, _ = x_sample.shape
    y_p = x_prompt.reshape(batch * seq, D_MODEL)
    y_s = x_sample.reshape(dec_batch * dec_seq, D_MODEL)
    cvecs = jnp.concatenate([c_ctx[None, :], c, jnp.zeros((8 - 1 - dec_batch, D_MODEL), F32)], axis=0)
    ii = lax.broadcasted_iota(jnp.int32, (LANES, LANES), 0) // HEAD_DIM
    jj = lax.broadcasted_iota(jnp.int32, (LANES, LANES), 1) // HEAD_DIM
    ones_bd = (ii == jj).astype(BF16)
    ctx_states = []
    for l in range(DEPTH):
        p = {
            'ln_g': ln_g[l], 'ln_b': ln_b[l],
            'ffn_in': ffn_in[l].astype(BF16), 'ffn_out': ffn_out[l].astype(BF16),
            'w_mix_in': w_mix_in[l].astype(BF16),
            'mu_shift': mu_shift[l], 'w0': w0[l], 'a0': a0[l],
            'wup_bd': _block_diag2(w_up[l]).astype(BF16), 'aup_bd': _block_diag2(a_up[l]).astype(BF16),
            'g_up': g_up[l].astype(BF16), 'k_k': k_k[l], 'k_a': k_a[l], 'r_k': r_k[l],
            'lnx_g': lnx_g[l], 'lnx_b': lnx_b[l], 'w_o_rwkv': w_o_rwkv[l].astype(BF16),
            'w_pool': w_pool[l].astype(BF16), 'pool_scale': pool_scale[l],
            'w_o_pool': w_o_pool[l].astype(BF16), 'w_out': w_out[l].astype(BF16), 'ones_bd': ones_bd,
        }
        mod = _modulation(cvecs, w_mod[l], b_mod[l]).reshape(8, N_MOD, D_MODEL)
        y_p, s_ctx = _trunk_layer(y_p, mod, lambda i, tm: 0, None, batch, seq, False, True, p)
        ctx_states.append(s_ctx)
        y_s, _ = _trunk_layer(y_s, mod, lambda i, tm: 1 + (i * tm) // dec_seq, state_rwkv[:, l],
                              dec_batch, dec_seq, True, False, p)
    new_state = jnp.stack(ctx_states, axis=1).astype(x_prompt.dtype)
    return (y_p.reshape(batch, seq, D_MODEL), y_s.reshape(dec_batch, dec_seq, D_MODEL), new_state)
```

```python
import functools
import math

import jax
import jax.numpy as jnp
from jax import lax
from jax.experimental import pallas as pl
from jax.experimental.pallas import tpu as pltpu

F32 = jnp.float32
BF16 = jnp.bfloat16

D_MODEL = 1024
DEPTH = 1
GRID_W = 64
HEAD_DIM = 64
D_A = D_MODEL
N_HEADS = D_A // HEAD_DIM
D_B = D_MODEL // 2
POOL_WINDOWS = (2, 4, 8, 16)
POOL_GROUP_DIM = D_B // len(POOL_WINDOWS)
LORA_W = 64
LORA_A = 64
LORA_G = 128
D_FF = 2816
N_MOD = 9
SHIFT_W = 3 * D_A + 2 * LORA_W + 2 * LORA_A + LORA_G
ALPHA = (2 * DEPTH) ** 0.25
LN_EPS = 1e-5
GN_EPS = 64e-5
DECAY_SCALE = math.exp(-0.5)

LANES = 128
HALO = 64
POOL_HALO = 8
POOL_PAD = 64
CHUNK = 64
PAIR = 2 * HEAD_DIM
SUBLANES = 8
VMEM_LIMIT = 56 * 1024 * 1024
FFN_TILE = 512
MIX_TILE = 256


def _cparams(sem):
    return pltpu.CompilerParams(dimension_semantics=sem, vmem_limit_bytes=VMEM_LIMIT)


def _mod_p2(x, n):
    assert n & (n - 1) == 0
    return x & (n - 1)


def _div_p2(x, n):
    assert n & (n - 1) == 0
    return x >> (n.bit_length() - 1)


def _sigmoid(x):
    return 0.5 * jnp.tanh(0.5 * x) + 0.5


def _silu(x):
    return x / (1.0 + jnp.exp(-x))


_NN = (((1,), (0,)), ((), ()))
_NT = (((1,), (1,)), ((), ()))
_TN = (((0,), (0,)), ((), ()))


def _dot(a, b, dims=_NN):
    return lax.dot_general(a.astype(BF16), b.astype(BF16), dims, preferred_element_type=F32)


def _layer_norm(z, g, b):
    mu = jnp.mean(z, axis=-1, keepdims=True)
    d = z - mu
    var = jnp.mean(d * d, axis=-1, keepdims=True)
    return d * lax.rsqrt(var + LN_EPS) * g + b


def _head_sum(x, ones_bd):
    xb = x.astype(BF16)
    return jnp.concatenate([jnp.dot(xb[:, c * LANES:(c + 1) * LANES], ones_bd, preferred_element_type=F32)
                            for c in range(x.shape[1] // LANES)], axis=1)


def _mod_kernel(c_ref, w_ref, b_ref, o_ref):
    o_ref[...] = _dot(_silu(c_ref[...]), w_ref[...]) + b_ref[...]


def _modulation(cvecs, w_mod, b_mod):
    rows = cvecs.shape[0]
    n = w_mod.shape[1]
    tn = 1024
    return pl.pallas_call(
        _mod_kernel,
        grid=(n // tn,),
        in_specs=[pl.BlockSpec((rows, D_MODEL), lambda j: (0, 0)),
                  pl.BlockSpec((D_MODEL, tn), lambda j: (0, j)),
                  pl.BlockSpec((1, tn), lambda j: (0, j))],
        out_specs=pl.BlockSpec((rows, tn), lambda j: (0, j)),
        out_shape=jax.ShapeDtypeStruct((rows, n), F32),
        compiler_params=_cparams(("arbitrary",)),
        name="modulation",
    )(cvecs, w_mod, b_mod.reshape(1, n))


def _ffn_kernel(x_ref, mod_ref, wi_ref, wo_ref, lng_ref, lnb_ref, o_ref, *, mi):
    x = x_ref[...]
    h = (x * (1.0 + mod_ref[mi + 1:mi + 2, :]) + mod_ref[mi:mi + 1, :]).astype(BF16)
    gate = jnp.dot(h, wi_ref[:, :D_FF], preferred_element_type=F32)
    up = jnp.dot(h, wi_ref[:, D_FF:], preferred_element_type=F32)
    ff = _dot(_silu(gate) * up, wo_ref[...])
    z = ALPHA * x + 0.5 * mod_ref[mi + 2:mi + 3, :] * ff
    o_ref[...] = _layer_norm(z, lng_ref[...], lnb_ref[...])


def _ffn(x, mod, mod_of_tile, w_in, w_out, ln_g, ln_b, which, mi, tm):
    rows = x.shape[0]
    resident = lambda shape: pl.BlockSpec(shape, lambda i: (0,) * len(shape), pipeline_mode=pl.Buffered(1))
    picked = lambda shape: pl.BlockSpec((None,) + shape, lambda i: (which, 0, 0), pipeline_mode=pl.Buffered(1))
    return pl.pallas_call(
        functools.partial(_ffn_kernel, mi=mi),
        grid=(rows // tm,),
        in_specs=[pl.BlockSpec((tm, D_MODEL), lambda i: (i, 0)),
                  pl.BlockSpec((None, N_MOD, D_MODEL), lambda i: (mod_of_tile(i, tm), 0, 0)),
                  picked((D_MODEL, 2 * D_FF)), picked((D_FF, D_MODEL)),
                  resident((1, D_MODEL)), resident((1, D_MODEL))],
        out_specs=pl.BlockSpec((tm, D_MODEL), lambda i: (i, 0)),
        out_shape=jax.ShapeDtypeStruct((rows, D_MODEL), F32),
        compiler_params=_cparams(("parallel",)),
        name="ffn",
    )(x, mod, w_in, w_out, ln_g.reshape(1, -1), ln_b.reshape(1, -1))


def _prep_kernel(x_ref, xprev_ref, xnext_ref, mod_ref, w_ref, mu_ref, wup_ref, aup_ref, w0_ref, a0_ref,
                 kk_ref, ka_ref, rk_ref, ones_ref,
                 r_o, v_o, kk_o, lw_o, kd_o, b_o, bonus_o, gs_o, pool_o, gate_o, ext_scr,
                 *, seq_len, latent, tm, mi):
    i = pl.program_id(0)
    t0 = _mod_p2(i * tm, seq_len)
    shift = mod_ref[mi:mi + 1, :]
    scale = mod_ref[mi + 1:mi + 2, :]

    def modulated(ref):
        return (ref[...] * (1.0 + scale) + shift).astype(BF16)

    h = modulated(x_ref)
    h_ext = jnp.concatenate([modulated(xprev_ref), h, modulated(xnext_ref)], axis=0) if latent else None

    def project(c0, c1):
        if latent:
            ext_scr[:, c0:c1] = jnp.dot(h_ext, w_ref[:, c0:c1], preferred_element_type=F32)
            ext_scr[0:HALO, c0:c1] = jnp.where(t0 == 0, 0.0, ext_scr[0:HALO, c0:c1])
            ext_scr[HALO + tm:, c0:c1] = jnp.where(t0 + tm == seq_len, 0.0, ext_scr[HALO + tm:, c0:c1])
        else:
            ext_scr[0:HALO, c0:c1] = jnp.zeros((HALO, c1 - c0), F32)
            ext_scr[HALO:HALO + tm, c0:c1] = jnp.dot(h, w_ref[:, c0:c1], preferred_element_type=F32)
            ext_scr[HALO + tm:, c0:c1] = jnp.zeros((HALO, c1 - c0), F32)

    t = t0 + lax.broadcasted_iota(jnp.int32, (tm, LANES), 0)
    lane = lax.broadcasted_iota(jnp.int32, (1, LANES), 1)
    n_src = 4 if latent else 2
    if latent:
        col = _mod_p2(t, GRID_W)
        has_before = jnp.where(col > 0, 1.0, 0.0)
        has_after = jnp.where(col < GRID_W - 1, 1.0, 0.0)

    def mixed_cols(c):
        sl = slice(c * LANES, (c + 1) * LANES)
        mu = mu_ref[:, sl]
        coef = [jnp.where(_mod_p2(lane, n_src) == s, mu, 0.0) for s in range(n_src)]
        before = ext_scr[HALO - 1:HALO - 1 + tm, sl]
        after = ext_scr[HALO + 1:HALO + 1 + tm, sl]
        if latent:
            return (ext_scr[HALO:HALO + tm, sl] * (1.0 - mu) + (before * has_before) * coef[0]
                    + (after * has_after) * coef[1]
                    + ext_scr[0:tm, sl] * coef[2] + ext_scr[2 * HALO:2 * HALO + tm, sl] * coef[3])
        return ext_scr[HALO:HALO + tm, sl] * (1.0 - mu) + before * coef[0] + after * coef[1]

    def mixed(c0, c1):
        return jnp.concatenate([mixed_cols(c) for c in range(c0, c1)], axis=1)

    nb = D_A // LANES
    project(0, D_A)
    project(D_A, 2 * D_A)
    r = mixed(0, nb)
    project(2 * D_A, 3 * D_A)
    k = mixed(nb, 2 * nb)
    project(3 * D_A, SHIFT_W)
    v = mixed(2 * nb, 3 * nb)
    pool_o[...] = jnp.dot(h, w_ref[:, SHIFT_W:SHIFT_W + D_B], preferred_element_type=F32)
    gate_o[...] = jnp.dot(h, w_ref[:, SHIFT_W + D_B:], preferred_element_type=F32)
    w_down = mixed_cols(3 * nb)
    a_down = mixed_cols(3 * nb + 1)
    g_down = mixed_cols(3 * nb + 2)

    ones_bd = ones_ref[...]
    r_o[...] = r.astype(r_o.dtype)
    v_o[...] = v
    kkraw = k * kk_ref[...]
    ss = _head_sum(kkraw * kkraw, ones_bd)
    kk = kkraw / jnp.maximum(jnp.sqrt(ss), 1e-12)
    kk_o[...] = kk
    bonus_o[...] = (_head_sum(r * k * rk_ref[...], ones_bd) * v).astype(bonus_o.dtype)
    gs_o[...] = _sigmoid(g_down)

    w_raw = _dot(jnp.tanh(w_down), wup_ref[...])
    a_raw = _dot(a_down, aup_ref[...])
    for d in range(2):
        sl = slice(d * D_A, (d + 1) * D_A)
        lw_o[d] = -DECAY_SCALE * _sigmoid(w0_ref[d:d + 1, :] + w_raw[:, sl])
        a = _sigmoid(a0_ref[d:d + 1, :] + a_raw[:, sl])
        kd_o[d] = k * (1.0 + (a - 1.0) * ka_ref[...])
        b_o[d] = kk * a


def _prep(x, mod, mod_of_tile, p, seq_len, latent, tm, mi):
    rows = x.shape[0]
    assert seq_len % tm == 0 and tm % HALO == 0 and (latent or tm == seq_len)
    hb = tm // HALO
    n_halo_blocks = rows // HALO
    n = p['w_mix_in'].shape[1]
    row1 = lambda a: a.reshape(1, -1)
    full = lambda shape: pl.BlockSpec(shape, lambda i: (0,) * len(shape), pipeline_mode=pl.Buffered(1))
    out_cols = lambda width: pl.BlockSpec((tm, width), lambda i: (i, 0))
    out_dir = pl.BlockSpec((2, tm, D_A), lambda i: (0, i, 0))
    sds = jax.ShapeDtypeStruct
    return pl.pallas_call(
        functools.partial(_prep_kernel, seq_len=seq_len, latent=latent, tm=tm, mi=mi),
        grid=(rows // tm,),
        in_specs=[pl.BlockSpec((tm, D_MODEL), lambda i: (i, 0)),
                  pl.BlockSpec((HALO, D_MODEL), lambda i: (jnp.maximum(i * hb - 1, 0), 0)),
                  pl.BlockSpec((HALO, D_MODEL), lambda i: (jnp.minimum((i + 1) * hb, n_halo_blocks - 1), 0)),
                  pl.BlockSpec((None, N_MOD, D_MODEL), lambda i: (mod_of_tile(i, tm), 0, 0)),
                  full((D_MODEL, n)),
                  full((1, SHIFT_W)), full((LANES, 2 * D_A)), full((LANES, 2 * D_A)),
                  full((2, D_A)), full((2, D_A)), full((1, D_A)), full((1, D_A)), full((1, D_A)),
                  full((LANES, LANES))],
        out_specs=[out_cols(D_A), out_cols(D_A), out_cols(D_A), out_dir, out_dir, out_dir, out_cols(D_A),
                   out_cols(LORA_G), out_cols(D_B), out_cols(n - SHIFT_W - D_B)],
        out_shape=[sds((rows, D_A), BF16), sds((rows, D_A), F32), sds((rows, D_A), F32),
                   sds((2, rows, D_A), F32), sds((2, rows, D_A), F32), sds((2, rows, D_A), F32),
                   sds((rows, D_A), BF16), sds((rows, LORA_G), F32),
                   sds((rows, D_B), F32), sds((rows, n - SHIFT_W - D_B), F32)],
        scratch_shapes=[pltpu.VMEM((tm + 2 * HALO, SHIFT_W), F32)],
        compiler_params=_cparams(("parallel",)),
        name="mix_in_prep",
    )(x, x, x, mod, p['w_mix_in'], row1(p['mu_shift']), p['wup_bd'], p['aup_bd'], p['w0'], p['a0'],
      row1(p['k_k']), row1(p['k_a']), row1(p['r_k']), p['ones_bd'])


SCAN_GROUP = 8
SCAN_SEQS = 2


def _scan_kernel(*refs, has_s0, want_state, group, nseq):
    if has_s0:
        (r_ref, v_ref, kk_ref, lw_ref, kd_ref, b_ref, s0_ref), rest = refs[:7], refs[7:]
    else:
        (r_ref, v_ref, kk_ref, lw_ref, kd_ref, b_ref), rest = refs[:6], refs[6:]
    if want_state:
        y_ref, sout_ref, s_scr = rest
    else:
        y_ref, s_scr = rest
    d = pl.program_id(0)
    c = pl.program_id(3)
    C = CHUNK

    chains = [(bi, p) for bi in range(nseq) for p in range(group)]
    keys = [(bi, slice(p * PAIR, (p + 1) * PAIR)) for bi, p in chains]

    @pl.when(c == 0)
    def _():
        if has_s0:
            z = jnp.zeros((HEAD_DIM, HEAD_DIM), F32)
            for j, (bi, p) in enumerate(chains):
                top = jnp.concatenate([s0_ref[bi, 2 * p], z], axis=1)
                bot = jnp.concatenate([z, s0_ref[bi, 2 * p + 1]], axis=1)
                s_scr[j] = jnp.concatenate([top, bot], axis=0)
        else:
            s_scr[...] = jnp.zeros_like(s_scr)

    row = lax.broadcasted_iota(jnp.int32, (PAIR, PAIR), 0)
    col = lax.broadcasted_iota(jnp.int32, (PAIR, PAIR), 1)
    same_head = _div_p2(row, HEAD_DIM) == _div_p2(col, HEAD_DIM)
    sign = 1 - 2 * d
    order = (_mod_p2(row, C) - _mod_p2(col, C)) * sign
    strict = (order > 0) & same_head
    incl = (order >= 0)[:C, :]
    head0 = lax.broadcasted_iota(jnp.int32, (C, PAIR), 1) < HEAD_DIM

    def stack(z):
        return jnp.concatenate([jnp.where(head0, z, 0.0), jnp.where(head0, 0.0, z)], axis=0)

    def each(fn, *lists):
        return [fn(*args) for args in zip(*lists)]

    trow = lax.broadcasted_iota(jnp.int32, (C, PAIR), 0)
    rev = d.astype(F32)

    def cumsum(lw, tot):
        acc = lw
        s = 1
        while s < C:
            acc = acc + jnp.where(trow >= s, pltpu.roll(acc, s, axis=0), 0.0)
            s *= 2
        return rev * (tot + lw) + (1.0 - 2.0 * rev) * acc

    def rd(ref, key):
        return ref[key[0], :, key[1]]

    lw = [rd(lw_ref, k) for k in keys]
    tot = [jnp.sum(z, axis=0, keepdims=True) for z in lw]
    cs = each(cumsum, lw, tot)
    w_inv = [jnp.exp(-z) for z in cs]
    w_rest = each(lambda t, z: jnp.exp(t - z), tot, cs)
    r_t = each(lambda k, z: rd(r_ref, k).astype(F32) * jnp.exp(z), keys, cs)
    kd = [rd(kd_ref, k) for k in keys]
    b = [rd(b_ref, k) for k in keys]
    kk_st = each(lambda k, z, l: stack(rd(kk_ref, k) * jnp.exp(z - l)), keys, cs, lw)
    v_st = [stack(rd(v_ref, k)) for k in keys]
    kb_st = each(lambda kdi, bi, wi: jnp.concatenate([stack(kdi * wi), stack(bi * wi)], axis=0),
                 kd, b, w_inv)
    m = each(lambda a, rt, kb: _dot(jnp.concatenate([a, rt], axis=0), kb, _NT), kk_st, r_t, kb_st)
    x = [jnp.where(strict, -z[:2 * C, 2 * C:], 0.0) for z in m]
    p_b = [jnp.where(incl, z[2 * C:, 2 * C:], 0.0) for z in m]
    apk_mask = jnp.concatenate([strict, incl], axis=0)
    apk_v = each(lambda z, vs: _dot(jnp.where(apk_mask, z[:, :2 * C], 0.0), vs), m, v_st)

    ti, tj = _mod_p2(row, C), _mod_p2(col, C)
    eye = jnp.where(row == col, 1.0, 0.0)

    def off_block(n):
        return (_div_p2(ti, 2 * n) == _div_p2(tj, 2 * n)) & (_div_p2(ti, n) != _div_p2(tj, n))

    t_inv = [eye + jnp.where(off_block(1), z, 0.0) for z in x]
    n = 2
    while n < C:
        keep = off_block(n)
        mid = each(lambda z, t: _dot(jnp.where(keep, z, 0.0), t), x, t_inv)
        t_inv = each(lambda t, mm: t + _dot(t, mm), t_inv, mid)
        n *= 2
    pbt = each(lambda pb, t: _dot(pb, t), p_b, t_inv)
    qq = each(lambda t, pt, a, av: _dot(jnp.concatenate([t, pt], axis=0),
                                        jnp.concatenate([a, av[:2 * C]], axis=1)),
              t_inv, pbt, kk_st, apk_v)
    q = [z[:2 * C] for z in qq]
    ry = each(lambda rt, av, z: jnp.concatenate([rt, av[2 * C:]], axis=1) - z[2 * C:], r_t, apk_v, qq)
    gq = each(lambda qi, bi, wr: _dot(qi, stack(bi * wr), _TN), q, b, w_rest)
    s_add = each(lambda vs, kdi, wr, g: _dot(vs, stack(kdi * wr), _TN) - g[PAIR:],
                 v_st, kd, w_rest, gq)

    s_old = [s_scr[j] for j in range(len(chains))]
    for j, (bi, sl) in enumerate(keys):
        y_ref[bi, :, sl] = _dot(ry[j][:, :PAIR], s_old[j], _NT) + ry[j][:, PAIR:]
    s_new = each(lambda s, t, g, sa: s * jnp.exp(t) - _dot(s, g[:PAIR]) + sa, s_old, tot, gq, s_add)
    for j in range(len(chains)):
        s_scr[j] = s_new[j]

    if want_state:
        @pl.when(c == pl.num_programs(3) - 1)
        def _():
            for j, (bi, p) in enumerate(chains):
                sout_ref[bi, 2 * p] = s_new[j][:HEAD_DIM, :HEAD_DIM]
                sout_ref[bi, 2 * p + 1] = s_new[j][HEAD_DIM:, HEAD_DIM:]


def _scan(r, v, kk, lw, kd, b, s0, batch, seq_len, want_state):
    rows = r.shape[0]
    nc = seq_len // CHUNK
    group = SCAN_GROUP
    nseq = min(SCAN_SEQS, batch)
    n_groups = D_A // (PAIR * group)
    gw = group * PAIR

    def chunk_idx(d, c):
        return c + d * (nc - 1 - 2 * c)

    shared = pl.BlockSpec((nseq, CHUNK, gw), lambda d, bi, p, c: (bi, chunk_idx(d, c), p))
    per_dir = pl.BlockSpec((None, nseq, CHUNK, gw), lambda d, bi, p, c: (d, bi, chunk_idx(d, c), p))
    state = pl.BlockSpec((nseq, None, 2 * group, HEAD_DIM, HEAD_DIM), lambda d, bi, p, c: (bi, d, p, 0, 0))
    in_specs = [shared, shared, shared, per_dir, per_dir, per_dir]
    seqs = lambda z: z.reshape(z.shape[:-2] + (batch, seq_len, D_A))
    args = [seqs(z) for z in (r, v, kk, lw, kd, b)]
    if s0 is not None:
        in_specs.append(state)
        args.append(s0)
    out_specs = [per_dir]
    out_shape = [jax.ShapeDtypeStruct((2, batch, seq_len, D_A), F32)]
    if want_state:
        out_specs.append(state)
        out_shape.append(jax.ShapeDtypeStruct((batch, 2, N_HEADS, HEAD_DIM, HEAD_DIM), F32))
    outs = pl.pallas_call(
        functools.partial(_scan_kernel, has_s0=s0 is not None, want_state=want_state, group=group, nseq=nseq),
        grid=(2, batch // nseq, n_groups, nc),
        in_specs=in_specs,
        out_specs=out_specs,
        out_shape=out_shape,
        scratch_shapes=[pltpu.VMEM((nseq * group, PAIR, PAIR), F32)],
        compiler_params=_cparams(("parallel", "parallel", "parallel", "arbitrary")),
        name="rwkv7_scan",
    )(*args)
    return [outs[0].reshape(2, rows, D_A)] + list(outs[1:])


def _post_kernel(y_ref, bonus_ref, gs_ref, gup_ref, pool_ref, pprev_ref, pnext_ref, gates_ref, x_ref, mod_ref,
                 lnxg_ref, lnxb_ref, worwkv_ref, wpool_ref, pscale_ref, wopool_ref, wout_ref,
                 lng_ref, lnb_ref, ones_ref, band_ref, o_ref, ext_scr, *, seq_len, tm):
    i = pl.program_id(0)
    ones_bd = ones_ref[...]
    y = y_ref[0] + y_ref[1]
    mu = _head_sum(y, ones_bd) * (1.0 / HEAD_DIM)

    t0 = _mod_p2(i * tm, seq_len)
    lo_edge, hi_edge = POOL_PAD - POOL_HALO, POOL_PAD + tm + POOL_HALO
    ext_scr[0:lo_edge, :] = jnp.zeros((lo_edge, D_B), F32)
    ext_scr[lo_edge:POOL_PAD, :] = jnp.where(t0 == 0, 0.0, pprev_ref[...])
    ext_scr[POOL_PAD:POOL_PAD + tm, :] = pool_ref[...]
    ext_scr[POOL_PAD + tm:hi_edge, :] = jnp.where(t0 + tm == seq_len, 0.0, pnext_ref[...])
    ext_scr[hi_edge:, :] = jnp.zeros((lo_edge, D_B), F32)
    ext = ext_scr[...].astype(BF16)
    groups = [slice(gi * POOL_GROUP_DIM, (gi + 1) * POOL_GROUP_DIM) for gi in range(len(POOL_WINDOWS))]
    win = [jnp.dot(band_ref[gi], ext[:, sl], preferred_element_type=F32) for gi, sl in enumerate(groups)]

    dlt = y - mu
    var = _head_sum(dlt * dlt, ones_bd) * (1.0 / HEAD_DIM)

    t = t0 + lax.broadcasted_iota(jnp.int32, (tm, LANES), 0)
    us = []
    for gi, w in enumerate(POOL_WINDOWS):
        cnt = jnp.minimum(t + (w - w // 2), seq_len) - jnp.maximum(t - w // 2, 0)
        resid = win[gi] / cnt.astype(F32) - pool_ref[:, groups[gi]]
        us.append(_dot(resid, wpool_ref[gi]))

    yn = dlt * lax.rsqrt(var + GN_EPS) * lnxg_ref[...] + lnxb_ref[...]
    ya = _dot((yn + bonus_ref[...].astype(F32)) * _dot(gs_ref[...], gup_ref[...]), worwkv_ref[...])
    u = jnp.concatenate(us, axis=1) * pscale_ref[...]
    yb = _dot(u, wopool_ref[...])

    gates = gates_ref[...]
    merged = _sigmoid(gates[:, :D_MODEL]) * ya + _sigmoid(gates[:, D_MODEL:]) * yb
    mix = _dot(merged, wout_ref[...])
    z = ALPHA * x_ref[...] + mod_ref[5:6, :] * mix
    o_ref[...] = _layer_norm(z, lng_ref[...], lnb_ref[...])


def _pool_bands(tm):
    i = lax.broadcasted_iota(jnp.int32, (tm, tm + 2 * POOL_PAD), 0)
    j = lax.broadcasted_iota(jnp.int32, (tm, tm + 2 * POOL_PAD), 1) - POOL_PAD
    return jnp.stack([((j >= i - w // 2) & (j < i + w - w // 2)).astype(BF16) for w in POOL_WINDOWS])


def _post(y, bonus, gs, pool_in, gates, x, mod, mod_of_tile, p, seq_len, tm):
    rows = x.shape[0]
    assert seq_len % tm == 0 and tm % POOL_HALO == 0
    hb = tm // POOL_HALO
    n_halo_blocks = rows // POOL_HALO
    row1 = lambda a: a.reshape(1, -1)
    full = lambda shape: pl.BlockSpec(shape, lambda i: (0,) * len(shape))
    tile = lambda width: pl.BlockSpec((tm, width), lambda i: (i, 0))
    return pl.pallas_call(
        functools.partial(_post_kernel, seq_len=seq_len, tm=tm),
        grid=(rows // tm,),
        in_specs=[pl.BlockSpec((2, tm, D_A), lambda i: (0, i, 0)), tile(D_A), tile(LORA_G), full((LORA_G, D_A)),
                  tile(D_B),
                  pl.BlockSpec((POOL_HALO, D_B), lambda i: (jnp.maximum(i * hb - 1, 0), 0)),
                  pl.BlockSpec((POOL_HALO, D_B), lambda i: (jnp.minimum((i + 1) * hb, n_halo_blocks - 1), 0)),
                  tile(2 * D_MODEL), tile(D_MODEL),
                  pl.BlockSpec((None, N_MOD, D_MODEL), lambda i: (mod_of_tile(i, tm), 0, 0)),
                  full((1, D_A)), full((1, D_A)), full((D_A, D_MODEL)),
                  full((len(POOL_WINDOWS), POOL_GROUP_DIM, POOL_GROUP_DIM)), full((1, D_B)),
                  full((D_B, D_MODEL)), full((D_MODEL, D_MODEL)), full((1, D_MODEL)), full((1, D_MODEL)),
                  full((LANES, LANES)), full((len(POOL_WINDOWS), tm, tm + 2 * POOL_PAD))],
        out_specs=tile(D_MODEL),
        out_shape=jax.ShapeDtypeStruct((rows, D_MODEL), F32),
        scratch_shapes=[pltpu.VMEM((tm + 2 * POOL_PAD, D_B), F32)],
        compiler_params=_cparams(("parallel",)),
        name="mixer_out",
    )(y, bonus, gs, p['g_up'], pool_in, pool_in, pool_in, gates, x, mod, row1(p['lnx_g']), row1(p['lnx_b']),
      p['w_o_rwkv'], p['w_pool'], row1(p['pool_scale']), p['w_o_pool'], p['w_out'],
      row1(p['ln_g'][1]), row1(p['ln_b'][1]), p['ones_bd'], _pool_bands(tm))


def _trunk_layer(x, mod, mod_of_tile, s0, batch, seq_len, latent, want_state, p):
    tm = MIX_TILE
    x1 = _ffn(x, mod, mod_of_tile, p['ffn_in'], p['ffn_out'], p['ln_g'][0], p['ln_b'][0], 0, 0, FFN_TILE)
    r, v, kk, lw, kd, b, bonus, gs, pool_in, gates = _prep(x1, mod, mod_of_tile, p, seq_len, latent, tm, 3)
    outs = _scan(r, v, kk, lw, kd, b, s0, batch, seq_len, want_state)
    y = outs[0]
    x2 = _post(y, bonus, gs, pool_in, gates, x1, mod, mod_of_tile, p, seq_len, tm)
    x3 = _ffn(x2, mod, mod_of_tile, p['ffn_in'], p['ffn_out'], p['ln_g'][2], p['ln_b'][2], 1, 6, FFN_TILE)
    return x3, (outs[1] if want_state else None)


def _block_diag2(w):
    z = jnp.zeros_like(w[0])
    return jnp.concatenate([jnp.concatenate([w[0], z], axis=1), jnp.concatenate([z, w[1]], axis=1)], axis=0)


def kernel(x_prompt, x_sample, c, state_rwkv, c_ctx, w_mod, b_mod, ln_g, ln_b, ffn_in, ffn_out, w_mix_in,
           mu_shift, w0, w_up, a0, a_up, g_up, k_k, k_a, r_k, lnx_g, lnx_b, w_o_rwkv, w_pool, pool_scale,
           w_o_pool, w_out):
    batch, seq, _ = x_prompt.shape
    dec_batch, dec_seq, _ = x_sample.shape
    y_p = x_prompt.reshape(batch * seq, D_MODEL)
    y_s = x_sample.reshape(dec_batch * dec_seq, D_MODEL)
    cvecs = jnp.concatenate([c_ctx[None, :], c, jnp.zeros((SUBLANES - 1 - dec_batch, D_MODEL), F32)], axis=0)
    ii = lax.broadcasted_iota(jnp.int32, (LANES, LANES), 0) // HEAD_DIM
    jj = lax.broadcasted_iota(jnp.int32, (LANES, LANES), 1) // HEAD_DIM
    ones_bd = (ii == jj).astype(BF16)
    ctx_states = []
    for l in range(DEPTH):
        p = {
            'ln_g': ln_g[l], 'ln_b': ln_b[l],
            'ffn_in': ffn_in[l].astype(BF16), 'ffn_out': ffn_out[l].astype(BF16),
            'w_mix_in': w_mix_in[l].astype(BF16),
            'mu_shift': mu_shift[l], 'w0': w0[l], 'a0': a0[l],
            'wup_bd': _block_diag2(w_up[l]).astype(BF16), 'aup_bd': _block_diag2(a_up[l]).astype(BF16),
            'g_up': g_up[l].astype(BF16), 'k_k': k_k[l], 'k_a': k_a[l], 'r_k': r_k[l],
            'lnx_g': lnx_g[l], 'lnx_b': lnx_b[l], 'w_o_rwkv': w_o_rwkv[l].astype(BF16),
            'w_pool': w_pool[l].astype(BF16), 'pool_scale': pool_scale[l],
            'w_o_pool': w_o_pool[l].astype(BF16), 'w_out': w_out[l].astype(BF16), 'ones_bd': ones_bd,
        }
        mod = _modulation(cvecs, w_mod[l], b_mod[l]).reshape(SUBLANES, N_MOD, D_MODEL)
        y_p, s_ctx = _trunk_layer(y_p, mod, lambda i, tm: 0, None, batch, seq, False, True, p)
        ctx_states.append(s_ctx)
        y_s, _ = _trunk_layer(y_s, mod, lambda i, tm: 1 + (i * tm) // dec_seq, state_rwkv[:, l],
                              dec_batch, dec_seq, True, False, p)
    new_state = jnp.stack(ctx_states, axis=1).astype(x_prompt.dtype)
    return (y_p.reshape(batch, seq, D_MODEL), y_s.reshape(dec_batch, dec_seq, D_MODEL), new_state)
```

```python
import functools
import math

import jax
import jax.numpy as jnp
from jax import lax
from jax.experimental import pallas as pl
from jax.experimental.pallas import tpu as pltpu

F32 = jnp.float32
BF16 = jnp.bfloat16

D_MODEL = 1024
DEPTH = 1
GRID_W = 64
HEAD_DIM = 64
D_A = D_MODEL
N_HEADS = D_A // HEAD_DIM
D_B = D_MODEL // 2
POOL_WINDOWS = (2, 4, 8, 16)
POOL_GROUP_DIM = D_B // len(POOL_WINDOWS)
LORA_W = 64
LORA_A = 64
LORA_G = 128
D_FF = 2816
N_MOD = 9
SHIFT_W = 3 * D_A + 2 * LORA_W + 2 * LORA_A + LORA_G
ALPHA = (2 * DEPTH) ** 0.25
LN_EPS = 1e-5
GN_EPS = 64e-5
DECAY_SCALE = math.exp(-0.5)

LANES = 128
HALO = 64
POOL_HALO = 8
POOL_PAD = 64
CHUNK = 64
PAIR = 2 * HEAD_DIM
SUBLANES = 8
VMEM_LIMIT = 56 * 1024 * 1024
FFN_TILE = 512
MIX_TILE = 256


def _cparams(sem):
    return pltpu.CompilerParams(dimension_semantics=sem, vmem_limit_bytes=VMEM_LIMIT)


def _mod_p2(x, n):
    assert n & (n - 1) == 0
    return x & (n - 1)


def _div_p2(x, n):
    assert n & (n - 1) == 0
    return x >> (n.bit_length() - 1)


def _sigmoid(x):
    return 0.5 * jnp.tanh(0.5 * x) + 0.5


def _silu(x):
    return x / (1.0 + jnp.exp(-x))


_NN = (((1,), (0,)), ((), ()))
_NT = (((1,), (1,)), ((), ()))
_TN = (((0,), (0,)), ((), ()))


def _dot(a, b, dims=_NN):
    return lax.dot_general(a.astype(BF16), b.astype(BF16), dims, preferred_element_type=F32)


def _layer_norm(z, g, b):
    mu = jnp.mean(z, axis=-1, keepdims=True)
    d = z - mu
    var = jnp.mean(d * d, axis=-1, keepdims=True)
    return d * lax.rsqrt(var + LN_EPS) * g + b


def _head_sum(x, ones_bd):
    xb = x.astype(BF16)
    return jnp.concatenate([jnp.dot(xb[:, c * LANES:(c + 1) * LANES], ones_bd, preferred_element_type=F32)
                            for c in range(x.shape[1] // LANES)], axis=1)


def _mod_kernel(c_ref, w_ref, b_ref, o_ref):
    o_ref[...] = _dot(_silu(c_ref[...]), w_ref[...]) + b_ref[...]


def _modulation(cvecs, w_mod, b_mod):
    rows = cvecs.shape[0]
    n = w_mod.shape[1]
    tn = 1024
    return pl.pallas_call(
        _mod_kernel,
        grid=(n // tn,),
        in_specs=[pl.BlockSpec((rows, D_MODEL), lambda j: (0, 0)),
                  pl.BlockSpec((D_MODEL, tn), lambda j: (0, j)),
                  pl.BlockSpec((1, tn), lambda j: (0, j))],
        out_specs=pl.BlockSpec((rows, tn), lambda j: (0, j)),
        out_shape=jax.ShapeDtypeStruct((rows, n), F32),
        compiler_params=_cparams(("arbitrary",)),
        name="modulation",
    )(cvecs, w_mod, b_mod.reshape(1, n))


def _ffn_kernel(x_ref, mod_ref, wi_ref, wo_ref, lng_ref, lnb_ref, o_ref, *, mi):
    x = x_ref[...]
    h = (x * (1.0 + mod_ref[mi + 1:mi + 2, :]) + mod_ref[mi:mi + 1, :]).astype(BF16)
    gate = jnp.dot(h, wi_ref[:, :D_FF], preferred_element_type=F32)
    up = jnp.dot(h, wi_ref[:, D_FF:], preferred_element_type=F32)
    ff = _dot(_silu(gate) * up, wo_ref[...])
    z = ALPHA * x + 0.5 * mod_ref[mi + 2:mi + 3, :] * ff
    o_ref[...] = _layer_norm(z, lng_ref[...], lnb_ref[...])


def _ffn(x, mod, mod_of_tile, w_in, w_out, ln_g, ln_b, which, mi, tm):
    rows = x.shape[0]
    resident = lambda shape: pl.BlockSpec(shape, lambda i: (0,) * len(shape), pipeline_mode=pl.Buffered(1))
    picked = lambda shape: pl.BlockSpec((None,) + shape, lambda i: (which, 0, 0), pipeline_mode=pl.Buffered(1))
    return pl.pallas_call(
        functools.partial(_ffn_kernel, mi=mi),
        grid=(rows // tm,),
        in_specs=[pl.BlockSpec((tm, D_MODEL), lambda i: (i, 0)),
                  pl.BlockSpec((None, N_MOD, D_MODEL), lambda i: (mod_of_tile(i, tm), 0, 0)),
                  picked((D_MODEL, 2 * D_FF)), picked((D_FF, D_MODEL)),
                  resident((1, D_MODEL)), resident((1, D_MODEL))],
        out_specs=pl.BlockSpec((tm, D_MODEL), lambda i: (i, 0)),
        out_shape=jax.ShapeDtypeStruct((rows, D_MODEL), F32),
        compiler_params=_cparams(("parallel",)),
        name="ffn",
    )(x, mod, w_in, w_out, ln_g.reshape(1, -1), ln_b.reshape(1, -1))


def _prep_kernel(x_ref, xprev_ref, xnext_ref, mod_ref, w_ref, mu_ref, wup_ref, aup_ref, w0_ref, a0_ref,
                 kk_ref, ka_ref, rk_ref, ones_ref,
                 r_o, v_o, kk_o, lw_o, kd_o, b_o, bonus_o, gs_o, pool_o, gate_o, ext_scr,
                 *, seq_len, latent, tm, mi):
    i = pl.program_id(0)
    t0 = _mod_p2(i * tm, seq_len)
    shift = mod_ref[mi:mi + 1, :]
    scale = mod_ref[mi + 1:mi + 2, :]

    def modulated(ref):
        return (ref[...] * (1.0 + scale) + shift).astype(BF16)

    h = modulated(x_ref)
    h_ext = jnp.concatenate([modulated(xprev_ref), h, modulated(xnext_ref)], axis=0) if latent else None

    def project(c0, c1):
        if latent:
            ext_scr[:, c0:c1] = jnp.dot(h_ext, w_ref[:, c0:c1], preferred_element_type=F32)
            ext_scr[0:HALO, c0:c1] = jnp.where(t0 == 0, 0.0, ext_scr[0:HALO, c0:c1])
            ext_scr[HALO + tm:, c0:c1] = jnp.where(t0 + tm == seq_len, 0.0, ext_scr[HALO + tm:, c0:c1])
        else:
            ext_scr[0:HALO, c0:c1] = jnp.zeros((HALO, c1 - c0), F32)
            ext_scr[HALO:HALO + tm, c0:c1] = jnp.dot(h, w_ref[:, c0:c1], preferred_element_type=F32)
            ext_scr[HALO + tm:, c0:c1] = jnp.zeros((HALO, c1 - c0), F32)

    t = t0 + lax.broadcasted_iota(jnp.int32, (tm, LANES), 0)
    lane = lax.broadcasted_iota(jnp.int32, (1, LANES), 1)
    n_src = 4 if latent else 2
    if latent:
        col = _mod_p2(t, GRID_W)
        has_before = jnp.where(col > 0, 1.0, 0.0)
        has_after = jnp.where(col < GRID_W - 1, 1.0, 0.0)

    def mixed_cols(c):
        sl = slice(c * LANES, (c + 1) * LANES)
        mu = mu_ref[:, sl]
        coef = [jnp.where(_mod_p2(lane, n_src) == s, mu, 0.0) for s in range(n_src)]
        before = ext_scr[HALO - 1:HALO - 1 + tm, sl]
        after = ext_scr[HALO + 1:HALO + 1 + tm, sl]
        if latent:
            return (ext_scr[HALO:HALO + tm, sl] * (1.0 - mu) + (before * has_before) * coef[0]
                    + (after * has_after) * coef[1]
                    + ext_scr[0:tm, sl] * coef[2] + ext_scr[2 * HALO:2 * HALO + tm, sl] * coef[3])
        return ext_scr[HALO:HALO + tm, sl] * (1.0 - mu) + before * coef[0] + after * coef[1]

    def mixed(c0, c1):
        return jnp.concatenate([mixed_cols(c) for c in range(c0, c1)], axis=1)

    nb = D_A // LANES
    project(0, D_A)
    project(D_A, 2 * D_A)
    r = mixed(0, nb)
    project(2 * D_A, 3 * D_A)
    k = mixed(nb, 2 * nb)
    project(3 * D_A, SHIFT_W)
    v = mixed(2 * nb, 3 * nb)
    pool_o[...] = jnp.dot(h, w_ref[:, SHIFT_W:SHIFT_W + D_B], preferred_element_type=F32)
    gate_o[...] = jnp.dot(h, w_ref[:, SHIFT_W + D_B:], preferred_element_type=F32)
    w_down = mixed_cols(3 * nb)
    a_down = mixed_cols(3 * nb + 1)
    g_down = mixed_cols(3 * nb + 2)

    ones_bd = ones_ref[...]
    r_o[...] = r.astype(r_o.dtype)
    v_o[...] = v
    kkraw = k * kk_ref[...]
    ss = _head_sum(kkraw * kkraw, ones_bd)
    kk = kkraw / jnp.maximum(jnp.sqrt(ss), 1e-12)
    kk_o[...] = kk
    bonus_o[...] = (_head_sum(r * k * rk_ref[...], ones_bd) * v).astype(bonus_o.dtype)
    gs_o[...] = _sigmoid(g_down)

    w_raw = _dot(jnp.tanh(w_down), wup_ref[...])
    a_raw = _dot(a_down, aup_ref[...])
    for d in range(2):
        sl = slice(d * D_A, (d + 1) * D_A)
        lw_o[d] = -DECAY_SCALE * _sigmoid(w0_ref[d:d + 1, :] + w_raw[:, sl])
        a = _sigmoid(a0_ref[d:d + 1, :] + a_raw[:, sl])
        kd_o[d] = k * (1.0 + (a - 1.0) * ka_ref[...])
        b_o[d] = kk * a


def _prep(x, mod, mod_of_tile, p, seq_len, latent, tm, mi):
    rows = x.shape[0]
    assert seq_len % tm == 0 and tm % HALO == 0 and (latent or tm == seq_len)
    hb = tm // HALO
    n_halo_blocks = rows // HALO
    n = p['w_mix_in'].shape[1]
    row1 = lambda a: a.reshape(1, -1)
    full = lambda shape: pl.BlockSpec(shape, lambda i: (0,) * len(shape), pipeline_mode=pl.Buffered(1))
    out_cols = lambda width: pl.BlockSpec((tm, width), lambda i: (i, 0))
    out_dir = pl.BlockSpec((2, tm, D_A), lambda i: (0, i, 0))
    sds = jax.ShapeDtypeStruct
    return pl.pallas_call(
        functools.partial(_prep_kernel, seq_len=seq_len, latent=latent, tm=tm, mi=mi),
        grid=(rows // tm,),
        in_specs=[pl.BlockSpec((tm, D_MODEL), lambda i: (i, 0)),
                  pl.BlockSpec((HALO, D_MODEL), lambda i: (jnp.maximum(i * hb - 1, 0), 0)),
                  pl.BlockSpec((HALO, D_MODEL), lambda i: (jnp.minimum((i + 1) * hb, n_halo_blocks - 1), 0)),
                  pl.BlockSpec((None, N_MOD, D_MODEL), lambda i: (mod_of_tile(i, tm), 0, 0)),
                  full((D_MODEL, n)),
                  full((1, SHIFT_W)), full((LANES, 2 * D_A)), full((LANES, 2 * D_A)),
                  full((2, D_A)), full((2, D_A)), full((1, D_A)), full((1, D_A)), full((1, D_A)),
                  full((LANES, LANES))],
        out_specs=[out_cols(D_A), out_cols(D_A), out_cols(D_A), out_dir, out_dir, out_dir, out_cols(D_A),
                   out_cols(LORA_G), out_cols(D_B), out_cols(n - SHIFT_W - D_B)],
        out_shape=[sds((rows, D_A), BF16), sds((rows, D_A), F32), sds((rows, D_A), F32),
                   sds((2, rows, D_A), F32), sds((2, rows, D_A), F32), sds((2, rows, D_A), F32),
                   sds((rows, D_A), BF16), sds((rows, LORA_G), F32),
                   sds((rows, D_B), F32), sds((rows, n - SHIFT_W - D_B), F32)],
        scratch_shapes=[pltpu.VMEM((tm + 2 * HALO, SHIFT_W), F32)],
        compiler_params=_cparams(("parallel",)),
        name="mix_in_prep",
    )(x, x, x, mod, p['w_mix_in'], row1(p['mu_shift']), p['wup_bd'], p['aup_bd'], p['w0'], p['a0'],
      row1(p['k_k']), row1(p['k_a']), row1(p['r_k']), p['ones_bd'])


SCAN_GROUP = 8
SCAN_SEQS = 2
SCAN_CHUNKS = 4


def _scan_kernel(*refs, has_s0, want_state, group, nseq, nsub):
    if has_s0:
        (r_ref, v_ref, kk_ref, lw_ref, kd_ref, b_ref, s0_ref), rest = refs[:7], refs[7:]
    else:
        (r_ref, v_ref, kk_ref, lw_ref, kd_ref, b_ref), rest = refs[:6], refs[6:]
    if want_state:
        y_ref, sout_ref, s_scr = rest
    else:
        y_ref, s_scr = rest
    d = pl.program_id(0)
    c = pl.program_id(3)
    C = CHUNK

    chains = [(bi, p) for bi in range(nseq) for p in range(group)]
    keys = [(bi, slice(p * PAIR, (p + 1) * PAIR)) for bi, p in chains]

    @pl.when(c == 0)
    def _():
        if has_s0:
            z = jnp.zeros((HEAD_DIM, HEAD_DIM), F32)
            for j, (bi, p) in enumerate(chains):
                top = jnp.concatenate([s0_ref[bi, 2 * p], z], axis=1)
                bot = jnp.concatenate([z, s0_ref[bi, 2 * p + 1]], axis=1)
                s_scr[j] = jnp.concatenate([top, bot], axis=0)
        else:
            s_scr[...] = jnp.zeros_like(s_scr)

    row = lax.broadcasted_iota(jnp.int32, (PAIR, PAIR), 0)
    col = lax.broadcasted_iota(jnp.int32, (PAIR, PAIR), 1)
    same_head = _div_p2(row, HEAD_DIM) == _div_p2(col, HEAD_DIM)
    sign = 1 - 2 * d
    order = (_mod_p2(row, C) - _mod_p2(col, C)) * sign
    strict = (order > 0) & same_head
    incl = (order >= 0)[:C, :]
    head0 = lax.broadcasted_iota(jnp.int32, (C, PAIR), 1) < HEAD_DIM

    def stack(z):
        return jnp.concatenate([jnp.where(head0, z, 0.0), jnp.where(head0, 0.0, z)], axis=0)

    def each(fn, *lists):
        return [fn(*args) for args in zip(*lists)]

    trow = lax.broadcasted_iota(jnp.int32, (C, PAIR), 0)
    rev = d.astype(F32)

    def cumsum(lw, tot):
        acc = lw
        s = 1
        while s < C:
            acc = acc + jnp.where(trow >= s, pltpu.roll(acc, s, axis=0), 0.0)
            s *= 2
        return rev * (tot + lw) + (1.0 - 2.0 * rev) * acc

    def prologue(off):
        def rd(ref, key):
            return ref[key[0], pl.ds(off, C), key[1]]

        lw = [rd(lw_ref, k) for k in keys]
        tot = [jnp.sum(z, axis=0, keepdims=True) for z in lw]
        cs = each(cumsum, lw, tot)
        w_inv = [jnp.exp(-z) for z in cs]
        w_rest = each(lambda t, z: jnp.exp(t - z), tot, cs)
        r_t = each(lambda k, z: rd(r_ref, k).astype(F32) * jnp.exp(z), keys, cs)
        kd = [rd(kd_ref, k) for k in keys]
        b = [rd(b_ref, k) for k in keys]
        kk_st = each(lambda k, z, l: stack(rd(kk_ref, k) * jnp.exp(z - l)), keys, cs, lw)
        v_st = [stack(rd(v_ref, k)) for k in keys]
        kb_st = each(lambda kdi, bi, wi: jnp.concatenate([stack(kdi * wi), stack(bi * wi)], axis=0),
                     kd, b, w_inv)
        kdec_st = each(lambda kdi, wr: stack(kdi * wr), kd, w_rest)
        bdec_st = each(lambda bi, wr: stack(bi * wr), b, w_rest)
        return tot, r_t, kk_st, v_st, kb_st, kdec_st, bdec_st

    def precompute(pro):
        tot, r_t, kk_st, v_st, kb_st, kdec_st, bdec_st = pro
        m = each(lambda a, rt, kb: _dot(jnp.concatenate([a, rt], axis=0), kb, _NT), kk_st, r_t, kb_st)
        x = [jnp.where(strict, -z[:2 * C, 2 * C:], 0.0) for z in m]
        p_b = [jnp.where(incl, z[2 * C:, 2 * C:], 0.0) for z in m]
        apk_mask = jnp.concatenate([strict, incl], axis=0)
        apk_v = each(lambda z, vs: _dot(jnp.where(apk_mask, z[:, :2 * C], 0.0), vs), m, v_st)

        ti, tj = _mod_p2(row, C), _mod_p2(col, C)
        eye = jnp.where(row == col, 1.0, 0.0)

        def off_block(n):
            return (_div_p2(ti, 2 * n) == _div_p2(tj, 2 * n)) & (_div_p2(ti, n) != _div_p2(tj, n))

        t_inv = [eye + jnp.where(off_block(1), z, 0.0) for z in x]
        n = 2
        while n < C:
            keep = off_block(n)
            mid = each(lambda z, t: _dot(jnp.where(keep, z, 0.0), t), x, t_inv)
            t_inv = each(lambda t, mm: t + _dot(t, mm), t_inv, mid)
            n *= 2
        pbt = each(lambda pb, t: _dot(pb, t), p_b, t_inv)
        qq = each(lambda t, pt, a, av: _dot(jnp.concatenate([t, pt], axis=0),
                                            jnp.concatenate([a, av[:2 * C]], axis=1)),
                  t_inv, pbt, kk_st, apk_v)
        q = [z[:2 * C] for z in qq]
        ry = each(lambda rt, av, z: jnp.concatenate([rt, av[2 * C:]], axis=1) - z[2 * C:], r_t, apk_v, qq)
        gq = each(lambda qi, bd: _dot(qi, bd, _TN), q, bdec_st)
        s_add = each(lambda vs, kd, g: _dot(vs, kd, _TN) - g[PAIR:], v_st, kdec_st, gq)

        return ry, tot, gq, s_add

    offs = [pl.multiple_of(C * (n + d * (nsub - 1 - 2 * n)), C) for n in range(nsub)]
    pre = [precompute(pro) for pro in [prologue(off) for off in offs]]
    s_new = [s_scr[j] for j in range(len(chains))]
    for off, (ry, tot, gq, s_add) in zip(offs, pre):
        s_old = s_new
        for j, (bi, sl) in enumerate(keys):
            y_ref[bi, pl.ds(off, C), sl] = _dot(ry[j][:, :PAIR], s_old[j], _NT) + ry[j][:, PAIR:]
        s_new = each(lambda s, t, g, sa: s * jnp.exp(t) - _dot(s, g[:PAIR]) + sa, s_old, tot, gq, s_add)
    for j in range(len(chains)):
        s_scr[j] = s_new[j]

    if want_state:
        @pl.when(c == pl.num_programs(3) - 1)
        def _():
            for j, (bi, p) in enumerate(chains):
                sout_ref[bi, 2 * p] = s_new[j][:HEAD_DIM, :HEAD_DIM]
                sout_ref[bi, 2 * p + 1] = s_new[j][HEAD_DIM:, HEAD_DIM:]


def _scan(r, v, kk, lw, kd, b, s0, batch, seq_len, want_state):
    rows = r.shape[0]
    nsub = SCAN_CHUNKS
    nc = seq_len // (CHUNK * nsub)
    group = SCAN_GROUP
    nseq = min(SCAN_SEQS, batch)
    n_groups = D_A // (PAIR * group)
    gw = group * PAIR

    def chunk_idx(d, c):
        return c + d * (nc - 1 - 2 * c)

    shared = pl.BlockSpec((nseq, nsub * CHUNK, gw), lambda d, bi, p, c: (bi, chunk_idx(d, c), p))
    per_dir = pl.BlockSpec((None, nseq, nsub * CHUNK, gw), lambda d, bi, p, c: (d, bi, chunk_idx(d, c), p))
    state = pl.BlockSpec((nseq, None, 2 * group, HEAD_DIM, HEAD_DIM), lambda d, bi, p, c: (bi, d, p, 0, 0))
    in_specs = [shared, shared, shared, per_dir, per_dir, per_dir]
    seqs = lambda z: z.reshape(z.shape[:-2] + (batch, seq_len, D_A))
    args = [seqs(z) for z in (r, v, kk, lw, kd, b)]
    if s0 is not None:
        in_specs.append(state)
        args.append(s0)
    out_specs = [per_dir]
    out_shape = [jax.ShapeDtypeStruct((2, batch, seq_len, D_A), F32)]
    if want_state:
        out_specs.append(state)
        out_shape.append(jax.ShapeDtypeStruct((batch, 2, N_HEADS, HEAD_DIM, HEAD_DIM), F32))
    outs = pl.pallas_call(
        functools.partial(_scan_kernel, has_s0=s0 is not None, want_state=want_state, group=group, nseq=nseq,
                          nsub=nsub),
        grid=(2, batch // nseq, n_groups, nc),
        in_specs=in_specs,
        out_specs=out_specs,
        out_shape=out_shape,
        scratch_shapes=[pltpu.VMEM((nseq * group, PAIR, PAIR), F32)],
        compiler_params=_cparams(("parallel", "parallel", "parallel", "arbitrary")),
        name="rwkv7_scan",
    )(*args)
    return [outs[0].reshape(2, rows, D_A)] + list(outs[1:])


def _post_kernel(y_ref, bonus_ref, gs_ref, gup_ref, pool_ref, pprev_ref, pnext_ref, gates_ref, x_ref, mod_ref,
                 lnxg_ref, lnxb_ref, worwkv_ref, wpool_ref, pscale_ref, wopool_ref, wout_ref,
                 lng_ref, lnb_ref, ones_ref, band_ref, o_ref, ext_scr, *, seq_len, tm):
    i = pl.program_id(0)
    ones_bd = ones_ref[...]
    y = y_ref[0] + y_ref[1]
    mu = _head_sum(y, ones_bd) * (1.0 / HEAD_DIM)

    t0 = _mod_p2(i * tm, seq_len)
    lo_edge, hi_edge = POOL_PAD - POOL_HALO, POOL_PAD + tm + POOL_HALO
    ext_scr[0:lo_edge, :] = jnp.zeros((lo_edge, D_B), F32)
    ext_scr[lo_edge:POOL_PAD, :] = jnp.where(t0 == 0, 0.0, pprev_ref[...])
    ext_scr[POOL_PAD:POOL_PAD + tm, :] = pool_ref[...]
    ext_scr[POOL_PAD + tm:hi_edge, :] = jnp.where(t0 + tm == seq_len, 0.0, pnext_ref[...])
    ext_scr[hi_edge:, :] = jnp.zeros((lo_edge, D_B), F32)
    ext = ext_scr[...].astype(BF16)
    groups = [slice(gi * POOL_GROUP_DIM, (gi + 1) * POOL_GROUP_DIM) for gi in range(len(POOL_WINDOWS))]
    win = [jnp.dot(band_ref[gi], ext[:, sl], preferred_element_type=F32) for gi, sl in enumerate(groups)]

    dlt = y - mu
    var = _head_sum(dlt * dlt, ones_bd) * (1.0 / HEAD_DIM)

    t = t0 + lax.broadcasted_iota(jnp.int32, (tm, LANES), 0)
    us = []
    for gi, w in enumerate(POOL_WINDOWS):
        cnt = jnp.minimum(t + (w - w // 2), seq_len) - jnp.maximum(t - w // 2, 0)
        resid = win[gi] / cnt.astype(F32) - pool_ref[:, groups[gi]]
        us.append(_dot(resid, wpool_ref[gi]))

    yn = dlt * lax.rsqrt(var + GN_EPS) * lnxg_ref[...] + lnxb_ref[...]
    ya = _dot((yn + bonus_ref[...].astype(F32)) * _dot(gs_ref[...], gup_ref[...]), worwkv_ref[...])
    u = jnp.concatenate(us, axis=1) * pscale_ref[...]
    yb = _dot(u, wopool_ref[...])

    gates = gates_ref[...]
    merged = _sigmoid(gates[:, :D_MODEL]) * ya + _sigmoid(gates[:, D_MODEL:]) * yb
    mix = _dot(merged, wout_ref[...])
    z = ALPHA * x_ref[...] + mod_ref[5:6, :] * mix
    o_ref[...] = _layer_norm(z, lng_ref[...], lnb_ref[...])


def _pool_bands(tm):
    i = lax.broadcasted_iota(jnp.int32, (tm, tm + 2 * POOL_PAD), 0)
    j = lax.broadcasted_iota(jnp.int32, (tm, tm + 2 * POOL_PAD), 1) - POOL_PAD
    return jnp.stack([((j >= i - w // 2) & (j < i + w - w // 2)).astype(BF16) for w in POOL_WINDOWS])


def _post(y, bonus, gs, pool_in, gates, x, mod, mod_of_tile, p, seq_len, tm):
    rows = x.shape[0]
    assert seq_len % tm == 0 and tm % POOL_HALO == 0
    hb = tm // POOL_HALO
    n_halo_blocks = rows // POOL_HALO
    row1 = lambda a: a.reshape(1, -1)
    full = lambda shape: pl.BlockSpec(shape, lambda i: (0,) * len(shape))
    tile = lambda width: pl.BlockSpec((tm, width), lambda i: (i, 0))
    return pl.pallas_call(
        functools.partial(_post_kernel, seq_len=seq_len, tm=tm),
        grid=(rows // tm,),
        in_specs=[pl.BlockSpec((2, tm, D_A), lambda i: (0, i, 0)), tile(D_A), tile(LORA_G), full((LORA_G, D_A)),
                  tile(D_B),
                  pl.BlockSpec((POOL_HALO, D_B), lambda i: (jnp.maximum(i * hb - 1, 0), 0)),
                  pl.BlockSpec((POOL_HALO, D_B), lambda i: (jnp.minimum((i + 1) * hb, n_halo_blocks - 1), 0)),
                  tile(2 * D_MODEL), tile(D_MODEL),
                  pl.BlockSpec((None, N_MOD, D_MODEL), lambda i: (mod_of_tile(i, tm), 0, 0)),
                  full((1, D_A)), full((1, D_A)), full((D_A, D_MODEL)),
                  full((len(POOL_WINDOWS), POOL_GROUP_DIM, POOL_GROUP_DIM)), full((1, D_B)),
                  full((D_B, D_MODEL)), full((D_MODEL, D_MODEL)), full((1, D_MODEL)), full((1, D_MODEL)),
                  full((LANES, LANES)), full((len(POOL_WINDOWS), tm, tm + 2 * POOL_PAD))],
        out_specs=tile(D_MODEL),
        out_shape=jax.ShapeDtypeStruct((rows, D_MODEL), F32),
        scratch_shapes=[pltpu.VMEM((tm + 2 * POOL_PAD, D_B), F32)],
        compiler_params=_cparams(("parallel",)),
        name="mixer_out",
    )(y, bonus, gs, p['g_up'], pool_in, pool_in, pool_in, gates, x, mod, row1(p['lnx_g']), row1(p['lnx_b']),
      p['w_o_rwkv'], p['w_pool'], row1(p['pool_scale']), p['w_o_pool'], p['w_out'],
      row1(p['ln_g'][1]), row1(p['ln_b'][1]), p['ones_bd'], _pool_bands(tm))


def _trunk_layer(x, mod, mod_of_tile, s0, batch, seq_len, latent, want_state, p):
    tm = MIX_TILE
    x1 = _ffn(x, mod, mod_of_tile, p['ffn_in'], p['ffn_out'], p['ln_g'][0], p['ln_b'][0], 0, 0, FFN_TILE)
    r, v, kk, lw, kd, b, bonus, gs, pool_in, gates = _prep(x1, mod, mod_of_tile, p, seq_len, latent, tm, 3)
    outs = _scan(r, v, kk, lw, kd, b, s0, batch, seq_len, want_state)
    y = outs[0]
    x2 = _post(y, bonus, gs, pool_in, gates, x1, mod, mod_of_tile, p, seq_len, tm)
    x3 = _ffn(x2, mod, mod_of_tile, p['ffn_in'], p['ffn_out'], p['ln_g'][2], p['ln_b'][2], 1, 6, FFN_TILE)
    return x3, (outs[1] if want_state else None)


def _block_diag2(w):
    z = jnp.zeros_like(w[0])
    return jnp.concatenate([jnp.concatenate([w[0], z], axis=1), jnp.concatenate([z, w[1]], axis=1)], axis=0)


def kernel(x_prompt, x_sample, c, state_rwkv, c_ctx, w_mod, b_mod, ln_g, ln_b, ffn_in, ffn_out, w_mix_in,
           mu_shift, w0, w_up, a0, a_up, g_up, k_k, k_a, r_k, lnx_g, lnx_b, w_o_rwkv, w_pool, pool_scale,
           w_o_pool, w_out):
    batch, seq, _ = x_prompt.shape
    dec_batch, dec_seq, _ = x_sample.shape
    y_p = x_prompt.reshape(batch * seq, D_MODEL)
    y_s = x_sample.reshape(dec_batch * dec_seq, D_MODEL)
    cvecs = jnp.concatenate([c_ctx[None, :], c, jnp.zeros((SUBLANES - 1 - dec_batch, D_MODEL), F32)], axis=0)
    ii = lax.broadcasted_iota(jnp.int32, (LANES, LANES), 0) // HEAD_DIM
    jj = lax.broadcasted_iota(jnp.int32, (LANES, LANES), 1) // HEAD_DIM
    ones_bd = (ii == jj).astype(BF16)
    ctx_states = []
    for l in range(DEPTH):
        p = {
            'ln_g': ln_g[l], 'ln_b': ln_b[l],
            'ffn_in': ffn_in[l].astype(BF16), 'ffn_out': ffn_out[l].astype(BF16),
            'w_mix_in': w_mix_in[l].astype(BF16),
            'mu_shift': mu_shift[l], 'w0': w0[l], 'a0': a0[l],
            'wup_bd': _block_diag2(w_up[l]).astype(BF16), 'aup_bd': _block_diag2(a_up[l]).astype(BF16),
            'g_up': g_up[l].astype(BF16), 'k_k': k_k[l], 'k_a': k_a[l], 'r_k': r_k[l],
            'lnx_g': lnx_g[l], 'lnx_b': lnx_b[l], 'w_o_rwkv': w_o_rwkv[l].astype(BF16),
            'w_pool': w_pool[l].astype(BF16), 'pool_scale': pool_scale[l],
            'w_o_pool': w_o_pool[l].astype(BF16), 'w_out': w_out[l].astype(BF16), 'ones_bd': ones_bd,
        }
        mod = _modulation(cvecs, w_mod[l], b_mod[l]).reshape(SUBLANES, N_MOD, D_MODEL)
        y_p, s_ctx = _trunk_layer(y_p, mod, lambda i, tm: 0, None, batch, seq, False, True, p)
        ctx_states.append(s_ctx)
        y_s, _ = _trunk_layer(y_s, mod, lambda i, tm: 1 + (i * tm) // dec_seq, state_rwkv[:, l],
                              dec_batch, dec_seq, True, False, p)
    new_state = jnp.stack(ctx_states, axis=1).astype(x_prompt.dtype)
    return (y_p.reshape(batch, seq, D_MODEL), y_s.reshape(dec_batch, dec_seq, D_MODEL), new_state)
```

```python
import functools
import math

import jax
import jax.numpy as jnp
from jax import lax
from jax.experimental import pallas as pl
from jax.experimental.pallas import tpu as pltpu

F32 = jnp.float32
BF16 = jnp.bfloat16

D_MODEL = 1024
DEPTH = 1
GRID_W = 64
HEAD_DIM = 64
D_A = D_MODEL
N_HEADS = D_A // HEAD_DIM
D_B = D_MODEL // 2
POOL_WINDOWS = (2, 4, 8, 16)
POOL_GROUP_DIM = D_B // len(POOL_WINDOWS)
LORA_W = 64
LORA_A = 64
LORA_G = 128
D_FF = 2816
N_MOD = 9
SHIFT_W = 3 * D_A + 2 * LORA_W + 2 * LORA_A + LORA_G
ALPHA = (2 * DEPTH) ** 0.25
LN_EPS = 1e-5
GN_EPS = 64e-5
DECAY_SCALE = math.exp(-0.5)

LANES = 128
HALO = 64
POOL_HALO = 8
POOL_PAD = 64
CHUNK = 64
PAIR = 2 * HEAD_DIM
SUBLANES = 8
VMEM_LIMIT = 56 * 1024 * 1024
FFN_TILE = 512
MIX_TILE = 256


def _cparams(sem):
    return pltpu.CompilerParams(dimension_semantics=sem, vmem_limit_bytes=VMEM_LIMIT)


def _mod_p2(x, n):
    assert n & (n - 1) == 0
    return x & (n - 1)


def _div_p2(x, n):
    assert n & (n - 1) == 0
    return x >> (n.bit_length() - 1)


def _sigmoid(x):
    return 0.5 * jnp.tanh(0.5 * x) + 0.5


def _silu(x):
    return x / (1.0 + jnp.exp(-x))


_NN = (((1,), (0,)), ((), ()))
_NT = (((1,), (1,)), ((), ()))
_TN = (((0,), (0,)), ((), ()))


def _dot(a, b, dims=_NN):
    return lax.dot_general(a.astype(BF16), b.astype(BF16), dims, preferred_element_type=F32)


def _layer_norm(z, g, b):
    mu = jnp.mean(z, axis=-1, keepdims=True)
    d = z - mu
    var = jnp.mean(d * d, axis=-1, keepdims=True)
    return d * lax.rsqrt(var + LN_EPS) * g + b


def _head_sum(x, ones_bd):
    xb = x.astype(BF16)
    return jnp.concatenate([jnp.dot(xb[:, c * LANES:(c + 1) * LANES], ones_bd, preferred_element_type=F32)
                            for c in range(x.shape[1] // LANES)], axis=1)


def _mod_kernel(c_ref, w_ref, b_ref, o_ref):
    o_ref[...] = _dot(_silu(c_ref[...]), w_ref[...]) + b_ref[...]


def _modulation(cvecs, w_mod, b_mod):
    rows = cvecs.shape[0]
    n = w_mod.shape[1]
    tn = 1024
    return pl.pallas_call(
        _mod_kernel,
        grid=(n // tn,),
        in_specs=[pl.BlockSpec((rows, D_MODEL), lambda j: (0, 0)),
                  pl.BlockSpec((D_MODEL, tn), lambda j: (0, j)),
                  pl.BlockSpec((1, tn), lambda j: (0, j))],
        out_specs=pl.BlockSpec((rows, tn), lambda j: (0, j)),
        out_shape=jax.ShapeDtypeStruct((rows, n), F32),
        compiler_params=_cparams(("arbitrary",)),
        name="modulation",
    )(cvecs, w_mod, b_mod.reshape(1, n))


def _ffn_kernel(xc_ref, xl_ref, mod_ref, wi_ref, wo_ref, lng_ref, lnb_ref, oc_ref, ol_ref, *, mi, n_ctx):
    is_ctx = pl.program_id(0) < n_ctx

    def sub_layer(x_ref, o_ref):
        x = x_ref[...]
        h = (x * (1.0 + mod_ref[mi + 1:mi + 2, :]) + mod_ref[mi:mi + 1, :]).astype(BF16)
        gate = jnp.dot(h, wi_ref[:, :D_FF], preferred_element_type=F32)
        up = jnp.dot(h, wi_ref[:, D_FF:], preferred_element_type=F32)
        ff = _dot(_silu(gate) * up, wo_ref[...])
        z = ALPHA * x + 0.5 * mod_ref[mi + 2:mi + 3, :] * ff
        o_ref[...] = _layer_norm(z, lng_ref[...], lnb_ref[...])

    @pl.when(is_ctx)
    def _():
        sub_layer(xc_ref, oc_ref)

    @pl.when(jnp.logical_not(is_ctx))
    def _():
        sub_layer(xl_ref, ol_ref)


def _ffn(x_ctx, x_lat, mod, lat_seq, w_in, w_out, ln_g, ln_b, which, mi, tm):
    n_ctx, n_lat = x_ctx.shape[0] // tm, x_lat.shape[0] // tm
    assert x_ctx.shape[0] % tm == 0 and x_lat.shape[0] % tm == 0 and lat_seq % tm == 0
    resident = lambda shape: pl.BlockSpec(shape, lambda i: (0,) * len(shape), pipeline_mode=pl.Buffered(1))
    picked = lambda shape: pl.BlockSpec((None,) + shape, lambda i: (which, 0, 0), pipeline_mode=pl.Buffered(1))
    ctx_tile = pl.BlockSpec((tm, D_MODEL), lambda i: (jnp.minimum(i, n_ctx - 1), 0))
    lat_tile = pl.BlockSpec((tm, D_MODEL), lambda i: (jnp.maximum(i - n_ctx, 0), 0))
    mod_row = lambda i: jnp.where(i < n_ctx, 0, 1 + (jnp.maximum(i - n_ctx, 0) * tm) // lat_seq)
    return pl.pallas_call(
        functools.partial(_ffn_kernel, mi=mi, n_ctx=n_ctx),
        grid=(n_ctx + n_lat,),
        in_specs=[ctx_tile, lat_tile,
                  pl.BlockSpec((None, N_MOD, D_MODEL), lambda i: (mod_row(i), 0, 0)),
                  picked((D_MODEL, 2 * D_FF)), picked((D_FF, D_MODEL)),
                  resident((1, D_MODEL)), resident((1, D_MODEL))],
        out_specs=[ctx_tile, lat_tile],
        out_shape=[jax.ShapeDtypeStruct(x_ctx.shape, F32), jax.ShapeDtypeStruct(x_lat.shape, F32)],
        compiler_params=_cparams(("arbitrary",)),
        name="ffn",
    )(x_ctx, x_lat, mod, w_in, w_out, ln_g.reshape(1, -1), ln_b.reshape(1, -1))


def _prep_kernel(x_ref, xprev_ref, xnext_ref, mod_ref, w_ref, mu_ref, wup_ref, aup_ref, w0_ref, a0_ref,
                 kk_ref, ka_ref, rk_ref, ones_ref,
                 r_o, v_o, kk_o, lw_o, kd_o, b_o, bonus_o, gs_o, pool_o, gate_o, ext_scr,
                 *, seq_len, latent, tm, mi):
    i = pl.program_id(0)
    t0 = _mod_p2(i * tm, seq_len)
    shift = mod_ref[mi:mi + 1, :]
    scale = mod_ref[mi + 1:mi + 2, :]

    def modulated(ref):
        return (ref[...] * (1.0 + scale) + shift).astype(BF16)

    h = modulated(x_ref)
    h_ext = jnp.concatenate([modulated(xprev_ref), h, modulated(xnext_ref)], axis=0) if latent else None

    def project(c0, c1):
        if latent:
            ext_scr[:, c0:c1] = jnp.dot(h_ext, w_ref[:, c0:c1], preferred_element_type=F32)
            ext_scr[0:HALO, c0:c1] = jnp.where(t0 == 0, 0.0, ext_scr[0:HALO, c0:c1])
            ext_scr[HALO + tm:, c0:c1] = jnp.where(t0 + tm == seq_len, 0.0, ext_scr[HALO + tm:, c0:c1])
        else:
            ext_scr[0:HALO, c0:c1] = jnp.zeros((HALO, c1 - c0), F32)
            ext_scr[HALO:HALO + tm, c0:c1] = jnp.dot(h, w_ref[:, c0:c1], preferred_element_type=F32)
            ext_scr[HALO + tm:, c0:c1] = jnp.zeros((HALO, c1 - c0), F32)

    t = t0 + lax.broadcasted_iota(jnp.int32, (tm, LANES), 0)
    lane = lax.broadcasted_iota(jnp.int32, (1, LANES), 1)
    n_src = 4 if latent else 2
    if latent:
        col = _mod_p2(t, GRID_W)
        has_before = jnp.where(col > 0, 1.0, 0.0)
        has_after = jnp.where(col < GRID_W - 1, 1.0, 0.0)

    def mixed_cols(c):
        sl = slice(c * LANES, (c + 1) * LANES)
        mu = mu_ref[:, sl]
        coef = [jnp.where(_mod_p2(lane, n_src) == s, mu, 0.0) for s in range(n_src)]
        before = ext_scr[HALO - 1:HALO - 1 + tm, sl]
        after = ext_scr[HALO + 1:HALO + 1 + tm, sl]
        if latent:
            return (ext_scr[HALO:HALO + tm, sl] * (1.0 - mu) + (before * has_before) * coef[0]
                    + (after * has_after) * coef[1]
                    + ext_scr[0:tm, sl] * coef[2] + ext_scr[2 * HALO:2 * HALO + tm, sl] * coef[3])
        return ext_scr[HALO:HALO + tm, sl] * (1.0 - mu) + before * coef[0] + after * coef[1]

    def mixed(c0, c1):
        return jnp.concatenate([mixed_cols(c) for c in range(c0, c1)], axis=1)

    nb = D_A // LANES
    project(0, D_A)
    project(D_A, 2 * D_A)
    r = mixed(0, nb)
    project(2 * D_A, 3 * D_A)
    k = mixed(nb, 2 * nb)
    project(3 * D_A, SHIFT_W)
    v = mixed(2 * nb, 3 * nb)
    pool_o[...] = jnp.dot(h, w_ref[:, SHIFT_W:SHIFT_W + D_B], preferred_element_type=F32)
    gate_o[...] = jnp.dot(h, w_ref[:, SHIFT_W + D_B:], preferred_element_type=F32)
    w_down = mixed_cols(3 * nb)
    a_down = mixed_cols(3 * nb + 1)
    g_down = mixed_cols(3 * nb + 2)

    ones_bd = ones_ref[...]
    r_o[...] = r.astype(r_o.dtype)
    v_o[...] = v
    kkraw = k * kk_ref[...]
    ss = _head_sum(kkraw * kkraw, ones_bd)
    kk = kkraw / jnp.maximum(jnp.sqrt(ss), 1e-12)
    kk_o[...] = kk
    bonus_o[...] = (_head_sum(r * k * rk_ref[...], ones_bd) * v).astype(bonus_o.dtype)
    gs_o[...] = _sigmoid(g_down)

    w_raw = _dot(jnp.tanh(w_down), wup_ref[...])
    a_raw = _dot(a_down, aup_ref[...])
    for d in range(2):
        sl = slice(d * D_A, (d + 1) * D_A)
        lw_o[d] = -DECAY_SCALE * _sigmoid(w0_ref[d:d + 1, :] + w_raw[:, sl])
        a = _sigmoid(a0_ref[d:d + 1, :] + a_raw[:, sl])
        kd_o[d] = k * (1.0 + (a - 1.0) * ka_ref[...])
        b_o[d] = kk * a


def _prep(x, mod, mod_of_tile, p, seq_len, latent, tm, mi):
    rows = x.shape[0]
    assert seq_len % tm == 0 and tm % HALO == 0 and (latent or tm == seq_len)
    hb = tm // HALO
    n_halo_blocks = rows // HALO
    n = p['w_mix_in'].shape[1]
    row1 = lambda a: a.reshape(1, -1)
    full = lambda shape: pl.BlockSpec(shape, lambda i: (0,) * len(shape), pipeline_mode=pl.Buffered(1))
    out_cols = lambda width: pl.BlockSpec((tm, width), lambda i: (i, 0))
    out_dir = pl.BlockSpec((2, tm, D_A), lambda i: (0, i, 0))
    sds = jax.ShapeDtypeStruct
    return pl.pallas_call(
        functools.partial(_prep_kernel, seq_len=seq_len, latent=latent, tm=tm, mi=mi),
        grid=(rows // tm,),
        in_specs=[pl.BlockSpec((tm, D_MODEL), lambda i: (i, 0)),
                  pl.BlockSpec((HALO, D_MODEL), lambda i: (jnp.maximum(i * hb - 1, 0), 0)),
                  pl.BlockSpec((HALO, D_MODEL), lambda i: (jnp.minimum((i + 1) * hb, n_halo_blocks - 1), 0)),
                  pl.BlockSpec((None, N_MOD, D_MODEL), lambda i: (mod_of_tile(i, tm), 0, 0)),
                  full((D_MODEL, n)),
                  full((1, SHIFT_W)), full((LANES, 2 * D_A)), full((LANES, 2 * D_A)),
                  full((2, D_A)), full((2, D_A)), full((1, D_A)), full((1, D_A)), full((1, D_A)),
                  full((LANES, LANES))],
        out_specs=[out_cols(D_A), out_cols(D_A), out_cols(D_A), out_dir, out_dir, out_dir, out_cols(D_A),
                   out_cols(LORA_G), out_cols(D_B), out_cols(n - SHIFT_W - D_B)],
        out_shape=[sds((rows, D_A), BF16), sds((rows, D_A), F32), sds((rows, D_A), F32),
                   sds((2, rows, D_A), F32), sds((2, rows, D_A), F32), sds((2, rows, D_A), F32),
                   sds((rows, D_A), BF16), sds((rows, LORA_G), F32),
                   sds((rows, D_B), F32), sds((rows, n - SHIFT_W - D_B), F32)],
        scratch_shapes=[pltpu.VMEM((tm + 2 * HALO, SHIFT_W), F32)],
        compiler_params=_cparams(("parallel",)),
        name="mix_in_prep",
    )(x, x, x, mod, p['w_mix_in'], row1(p['mu_shift']), p['wup_bd'], p['aup_bd'], p['w0'], p['a0'],
      row1(p['k_k']), row1(p['k_a']), row1(p['r_k']), p['ones_bd'])


SCAN_GROUP = 8
SCAN_SEQS = 2
SCAN_CHUNKS = 4


def _scan_kernel(*refs, has_s0, want_state, group, nseq, nsub):
    if has_s0:
        (r_ref, v_ref, kk_ref, lw_ref, kd_ref, b_ref, s0_ref), rest = refs[:7], refs[7:]
    else:
        (r_ref, v_ref, kk_ref, lw_ref, kd_ref, b_ref), rest = refs[:6], refs[6:]
    if want_state:
        y_ref, sout_ref, s_scr = rest
    else:
        y_ref, s_scr = rest
    d = pl.program_id(0)
    c = pl.program_id(3)
    C = CHUNK

    chains = [(bi, p) for bi in range(nseq) for p in range(group)]
    keys = [(bi, slice(p * PAIR, (p + 1) * PAIR)) for bi, p in chains]

    @pl.when(c == 0)
    def _():
        if has_s0:
            z = jnp.zeros((HEAD_DIM, HEAD_DIM), F32)
            for j, (bi, p) in enumerate(chains):
                top = jnp.concatenate([s0_ref[bi, 2 * p], z], axis=1)
                bot = jnp.concatenate([z, s0_ref[bi, 2 * p + 1]], axis=1)
                s_scr[j] = jnp.concatenate([top, bot], axis=0)
        else:
            s_scr[...] = jnp.zeros_like(s_scr)

    row = lax.broadcasted_iota(jnp.int32, (PAIR, PAIR), 0)
    col = lax.broadcasted_iota(jnp.int32, (PAIR, PAIR), 1)
    same_head = _div_p2(row, HEAD_DIM) == _div_p2(col, HEAD_DIM)
    sign = 1 - 2 * d
    order = (_mod_p2(row, C) - _mod_p2(col, C)) * sign
    strict = (order > 0) & same_head
    incl = (order >= 0)[:C, :]
    head0 = lax.broadcasted_iota(jnp.int32, (C, PAIR), 1) < HEAD_DIM

    def stack(z):
        return jnp.concatenate([jnp.where(head0, z, 0.0), jnp.where(head0, 0.0, z)], axis=0)

    def each(fn, *lists):
        return [fn(*args) for args in zip(*lists)]

    trow = lax.broadcasted_iota(jnp.int32, (C, PAIR), 0)
    rev = d.astype(F32)

    def cumsum(lw, tot):
        acc = lw
        s = 1
        while s < C:
            acc = acc + jnp.where(trow >= s, pltpu.roll(acc, s, axis=0), 0.0)
            s *= 2
        return rev * (tot + lw) + (1.0 - 2.0 * rev) * acc

    def prologue(off):
        def rd(ref, key):
            return ref[key[0], pl.ds(off, C), key[1]]

        lw = [rd(lw_ref, k) for k in keys]
        tot = [jnp.sum(z, axis=0, keepdims=True) for z in lw]
        cs = each(cumsum, lw, tot)
        w_inv = [jnp.exp(-z) for z in cs]
        w_rest = each(lambda t, z: jnp.exp(t - z), tot, cs)
        r_t = each(lambda k, z: rd(r_ref, k).astype(F32) * jnp.exp(z), keys, cs)
        kd = [rd(kd_ref, k) for k in keys]
        b = [rd(b_ref, k) for k in keys]
        kk_st = each(lambda k, z, l: stack(rd(kk_ref, k) * jnp.exp(z - l)), keys, cs, lw)
        v_st = [stack(rd(v_ref, k)) for k in keys]
        kb_st = each(lambda kdi, bi, wi: jnp.concatenate([stack(kdi * wi), stack(bi * wi)], axis=0),
                     kd, b, w_inv)
        kdec_st = each(lambda kdi, wr: stack(kdi * wr), kd, w_rest)
        bdec_st = each(lambda bi, wr: stack(bi * wr), b, w_rest)
        return tot, r_t, kk_st, v_st, kb_st, kdec_st, bdec_st

    def precompute(pro):
        tot, r_t, kk_st, v_st, kb_st, kdec_st, bdec_st = pro
        m = each(lambda a, rt, kb: _dot(jnp.concatenate([a, rt], axis=0), kb, _NT), kk_st, r_t, kb_st)
        x = [jnp.where(strict, -z[:2 * C, 2 * C:], 0.0) for z in m]
        p_b = [jnp.where(incl, z[2 * C:, 2 * C:], 0.0) for z in m]
        apk_mask = jnp.concatenate([strict, incl], axis=0)
        apk_v = each(lambda z, vs: _dot(jnp.where(apk_mask, z[:, :2 * C], 0.0), vs), m, v_st)

        ti, tj = _mod_p2(row, C), _mod_p2(col, C)
        eye = jnp.where(row == col, 1.0, 0.0)

        def off_block(n):
            return (_div_p2(ti, 2 * n) == _div_p2(tj, 2 * n)) & (_div_p2(ti, n) != _div_p2(tj, n))

        t_inv = [eye + jnp.where(off_block(1), z, 0.0) for z in x]
        n = 2
        while n < C:
            keep = off_block(n)
            mid = each(lambda z, t: _dot(jnp.where(keep, z, 0.0), t), x, t_inv)
            t_inv = each(lambda t, mm: t + _dot(t, mm), t_inv, mid)
            n *= 2
        pbt = each(lambda pb, t: _dot(pb, t), p_b, t_inv)
        qq = each(lambda t, pt, a, av: _dot(jnp.concatenate([t, pt], axis=0),
                                            jnp.concatenate([a, av[:2 * C]], axis=1)),
                  t_inv, pbt, kk_st, apk_v)
        q = [z[:2 * C] for z in qq]
        ry = each(lambda rt, av, z: jnp.concatenate([rt, av[2 * C:]], axis=1) - z[2 * C:], r_t, apk_v, qq)
        gq = each(lambda qi, bd: _dot(qi, bd, _TN), q, bdec_st)
        s_add = each(lambda vs, kd, g: _dot(vs, kd, _TN) - g[PAIR:], v_st, kdec_st, gq)

        return ry, tot, gq, s_add

    offs = [pl.multiple_of(C * (n + d * (nsub - 1 - 2 * n)), C) for n in range(nsub)]
    pre = [precompute(pro) for pro in [prologue(off) for off in offs]]
    s_new = [s_scr[j] for j in range(len(chains))]
    for off, (ry, tot, gq, s_add) in zip(offs, pre):
        s_old = s_new
        for j, (bi, sl) in enumerate(keys):
            y_ref[bi, pl.ds(off, C), sl] = _dot(ry[j][:, :PAIR], s_old[j], _NT) + ry[j][:, PAIR:]
        s_new = each(lambda s, t, g, sa: s * jnp.exp(t) - _dot(s, g[:PAIR]) + sa, s_old, tot, gq, s_add)
    for j in range(len(chains)):
        s_scr[j] = s_new[j]

    if want_state:
        @pl.when(c == pl.num_programs(3) - 1)
        def _():
            for j, (bi, p) in enumerate(chains):
                sout_ref[bi, 2 * p] = s_new[j][:HEAD_DIM, :HEAD_DIM]
                sout_ref[bi, 2 * p + 1] = s_new[j][HEAD_DIM:, HEAD_DIM:]


def _scan(r, v, kk, lw, kd, b, s0, batch, seq_len, want_state):
    rows = r.shape[0]
    nsub = SCAN_CHUNKS
    nc = seq_len // (CHUNK * nsub)
    group = SCAN_GROUP
    nseq = min(SCAN_SEQS, batch)
    n_groups = D_A // (PAIR * group)
    gw = group * PAIR

    def chunk_idx(d, c):
        return c + d * (nc - 1 - 2 * c)

    shared = pl.BlockSpec((nseq, nsub * CHUNK, gw), lambda d, bi, p, c: (bi, chunk_idx(d, c), p))
    per_dir = pl.BlockSpec((None, nseq, nsub * CHUNK, gw), lambda d, bi, p, c: (d, bi, chunk_idx(d, c), p))
    state = pl.BlockSpec((nseq, None, 2 * group, HEAD_DIM, HEAD_DIM), lambda d, bi, p, c: (bi, d, p, 0, 0))
    in_specs = [shared, shared, shared, per_dir, per_dir, per_dir]
    seqs = lambda z: z.reshape(z.shape[:-2] + (batch, seq_len, D_A))
    args = [seqs(z) for z in (r, v, kk, lw, kd, b)]
    if s0 is not None:
        in_specs.append(state)
        args.append(s0)
    out_specs = [per_dir]
    out_shape = [jax.ShapeDtypeStruct((2, batch, seq_len, D_A), F32)]
    if want_state:
        out_specs.append(state)
        out_shape.append(jax.ShapeDtypeStruct((batch, 2, N_HEADS, HEAD_DIM, HEAD_DIM), F32))
    outs = pl.pallas_call(
        functools.partial(_scan_kernel, has_s0=s0 is not None, want_state=want_state, group=group, nseq=nseq,
                          nsub=nsub),
        grid=(2, batch // nseq, n_groups, nc),
        in_specs=in_specs,
        out_specs=out_specs,
        out_shape=out_shape,
        scratch_shapes=[pltpu.VMEM((nseq * group, PAIR, PAIR), F32)],
        compiler_params=_cparams(("parallel", "parallel", "parallel", "arbitrary")),
        name="rwkv7_scan",
    )(*args)
    return [outs[0].reshape(2, rows, D_A)] + list(outs[1:])


def _post_kernel(y_ref, bonus_ref, gs_ref, gup_ref, pool_ref, pprev_ref, pnext_ref, gates_ref, x_ref, mod_ref,
                 lnxg_ref, lnxb_ref, worwkv_ref, wpool_ref, pscale_ref, wopool_ref, wout_ref,
                 lng_ref, lnb_ref, ones_ref, band_ref, o_ref, ext_scr, *, seq_len, tm):
    i = pl.program_id(0)
    ones_bd = ones_ref[...]
    y = y_ref[0] + y_ref[1]
    mu = _head_sum(y, ones_bd) * (1.0 / HEAD_DIM)

    t0 = _mod_p2(i * tm, seq_len)
    lo_edge, hi_edge = POOL_PAD - POOL_HALO, POOL_PAD + tm + POOL_HALO
    ext_scr[0:lo_edge, :] = jnp.zeros((lo_edge, D_B), F32)
    ext_scr[lo_edge:POOL_PAD, :] = jnp.where(t0 == 0, 0.0, pprev_ref[...])
    ext_scr[POOL_PAD:POOL_PAD + tm, :] = pool_ref[...]
    ext_scr[POOL_PAD + tm:hi_edge, :] = jnp.where(t0 + tm == seq_len, 0.0, pnext_ref[...])
    ext_scr[hi_edge:, :] = jnp.zeros((lo_edge, D_B), F32)
    ext = ext_scr[...].astype(BF16)
    groups = [slice(gi * POOL_GROUP_DIM, (gi + 1) * POOL_GROUP_DIM) for gi in range(len(POOL_WINDOWS))]
    win = [jnp.dot(band_ref[gi], ext[:, sl], preferred_element_type=F32) for gi, sl in enumerate(groups)]

    dlt = y - mu
    var = _head_sum(dlt * dlt, ones_bd) * (1.0 / HEAD_DIM)

    t = t0 + lax.broadcasted_iota(jnp.int32, (tm, LANES), 0)
    us = []
    for gi, w in enumerate(POOL_WINDOWS):
        cnt = jnp.minimum(t + (w - w // 2), seq_len) - jnp.maximum(t - w // 2, 0)
        resid = win[gi] / cnt.astype(F32) - pool_ref[:, groups[gi]]
        us.append(_dot(resid, wpool_ref[gi]))

    yn = dlt * lax.rsqrt(var + GN_EPS) * lnxg_ref[...] + lnxb_ref[...]
    ya = _dot((yn + bonus_ref[...].astype(F32)) * _dot(gs_ref[...], gup_ref[...]), worwkv_ref[...])
    u = jnp.concatenate(us, axis=1) * pscale_ref[...]
    yb = _dot(u, wopool_ref[...])

    gates = gates_ref[...]
    merged = _sigmoid(gates[:, :D_MODEL]) * ya + _sigmoid(gates[:, D_MODEL:]) * yb
    mix = _dot(merged, wout_ref[...])
    z = ALPHA * x_ref[...] + mod_ref[5:6, :] * mix
    o_ref[...] = _layer_norm(z, lng_ref[...], lnb_ref[...])


def _pool_bands(tm):
    i = lax.broadcasted_iota(jnp.int32, (tm, tm + 2 * POOL_PAD), 0)
    j = lax.broadcasted_iota(jnp.int32, (tm, tm + 2 * POOL_PAD), 1) - POOL_PAD
    return jnp.stack([((j >= i - w // 2) & (j < i + w - w // 2)).astype(BF16) for w in POOL_WINDOWS])


def _post(y, bonus, gs, pool_in, gates, x, mod, mod_of_tile, p, seq_len, tm):
    rows = x.shape[0]
    assert seq_len % tm == 0 and tm % POOL_HALO == 0
    hb = tm // POOL_HALO
    n_halo_blocks = rows // POOL_HALO
    row1 = lambda a: a.reshape(1, -1)
    full = lambda shape: pl.BlockSpec(shape, lambda i: (0,) * len(shape))
    tile = lambda width: pl.BlockSpec((tm, width), lambda i: (i, 0))
    return pl.pallas_call(
        functools.partial(_post_kernel, seq_len=seq_len, tm=tm),
        grid=(rows // tm,),
        in_specs=[pl.BlockSpec((2, tm, D_A), lambda i: (0, i, 0)), tile(D_A), tile(LORA_G), full((LORA_G, D_A)),
                  tile(D_B),
                  pl.BlockSpec((POOL_HALO, D_B), lambda i: (jnp.maximum(i * hb - 1, 0), 0)),
                  pl.BlockSpec((POOL_HALO, D_B), lambda i: (jnp.minimum((i + 1) * hb, n_halo_blocks - 1), 0)),
                  tile(2 * D_MODEL), tile(D_MODEL),
                  pl.BlockSpec((None, N_MOD, D_MODEL), lambda i: (mod_of_tile(i, tm), 0, 0)),
                  full((1, D_A)), full((1, D_A)), full((D_A, D_MODEL)),
                  full((len(POOL_WINDOWS), POOL_GROUP_DIM, POOL_GROUP_DIM)), full((1, D_B)),
                  full((D_B, D_MODEL)), full((D_MODEL, D_MODEL)), full((1, D_MODEL)), full((1, D_MODEL)),
                  full((LANES, LANES)), full((len(POOL_WINDOWS), tm, tm + 2 * POOL_PAD))],
        out_specs=tile(D_MODEL),
        out_shape=jax.ShapeDtypeStruct((rows, D_MODEL), F32),
        scratch_shapes=[pltpu.VMEM((tm + 2 * POOL_PAD, D_B), F32)],
        compiler_params=_cparams(("parallel",)),
        name="mixer_out",
    )(y, bonus, gs, p['g_up'], pool_in, pool_in, pool_in, gates, x, mod, row1(p['lnx_g']), row1(p['lnx_b']),
      p['w_o_rwkv'], p['w_pool'], row1(p['pool_scale']), p['w_o_pool'], p['w_out'],
      row1(p['ln_g'][1]), row1(p['ln_b'][1]), p['ones_bd'], _pool_bands(tm))


def _mixer(x1, mod, mod_of_tile, s0, batch, seq_len, latent, want_state, p):
    tm = MIX_TILE
    r, v, kk, lw, kd, b, bonus, gs, pool_in, gates = _prep(x1, mod, mod_of_tile, p, seq_len, latent, tm, 3)
    outs = _scan(r, v, kk, lw, kd, b, s0, batch, seq_len, want_state)
    x2 = _post(outs[0], bonus, gs, pool_in, gates, x1, mod, mod_of_tile, p, seq_len, min(2 * tm, seq_len))
    return x2, (outs[1] if want_state else None)


def _block_diag2(w):
    z = jnp.zeros_like(w[0])
    return jnp.concatenate([jnp.concatenate([w[0], z], axis=1), jnp.concatenate([z, w[1]], axis=1)], axis=0)


def kernel(x_prompt, x_sample, c, state_rwkv, c_ctx, w_mod, b_mod, ln_g, ln_b, ffn_in, ffn_out, w_mix_in,
           mu_shift, w0, w_up, a0, a_up, g_up, k_k, k_a, r_k, lnx_g, lnx_b, w_o_rwkv, w_pool, pool_scale,
           w_o_pool, w_out):
    batch, seq, _ = x_prompt.shape
    dec_batch, dec_seq, _ = x_sample.shape
    y_p = x_prompt.reshape(batch * seq, D_MODEL)
    y_s = x_sample.reshape(dec_batch * dec_seq, D_MODEL)
    cvecs = jnp.concatenate([c_ctx[None, :], c, jnp.zeros((SUBLANES - 1 - dec_batch, D_MODEL), F32)], axis=0)
    ii = lax.broadcasted_iota(jnp.int32, (LANES, LANES), 0) // HEAD_DIM
    jj = lax.broadcasted_iota(jnp.int32, (LANES, LANES), 1) // HEAD_DIM
    ones_bd = (ii == jj).astype(BF16)
    ctx_states = []
    for l in range(DEPTH):
        p = {
            'ln_g': ln_g[l], 'ln_b': ln_b[l],
            'ffn_in': ffn_in[l].astype(BF16), 'ffn_out': ffn_out[l].astype(BF16),
            'w_mix_in': w_mix_in[l].astype(BF16),
            'mu_shift': mu_shift[l], 'w0': w0[l], 'a0': a0[l],
            'wup_bd': _block_diag2(w_up[l]).astype(BF16), 'aup_bd': _block_diag2(a_up[l]).astype(BF16),
            'g_up': g_up[l].astype(BF16), 'k_k': k_k[l], 'k_a': k_a[l], 'r_k': r_k[l],
            'lnx_g': lnx_g[l], 'lnx_b': lnx_b[l], 'w_o_rwkv': w_o_rwkv[l].astype(BF16),
            'w_pool': w_pool[l].astype(BF16), 'pool_scale': pool_scale[l],
            'w_o_pool': w_o_pool[l].astype(BF16), 'w_out': w_out[l].astype(BF16), 'ones_bd': ones_bd,
        }
        mod = _modulation(cvecs, w_mod[l], b_mod[l]).reshape(SUBLANES, N_MOD, D_MODEL)
        y_p, y_s = _ffn(y_p, y_s, mod, dec_seq, p['ffn_in'], p['ffn_out'], p['ln_g'][0], p['ln_b'][0], 0, 0, FFN_TILE)
        y_p, s_ctx = _mixer(y_p, mod, lambda i, tm: 0, None, batch, seq, False, True, p)
        ctx_states.append(s_ctx)
        y_s, _ = _mixer(y_s, mod, lambda i, tm: 1 + (i * tm) // dec_seq, state_rwkv[:, l],
                        dec_batch, dec_seq, True, False, p)
        y_p, y_s = _ffn(y_p, y_s, mod, dec_seq, p['ffn_in'], p['ffn_out'], p['ln_g'][2], p['ln_b'][2], 1, 6, FFN_TILE)
    new_state = jnp.stack(ctx_states, axis=1).astype(x_prompt.dtype)
    return (y_p.reshape(batch, seq, D_MODEL), y_s.reshape(dec_batch, dec_seq, D_MODEL), new_state)
```

```python
import functools
import math

import jax
import jax.numpy as jnp
from jax import lax
from jax.experimental import pallas as pl
from jax.experimental.pallas import tpu as pltpu

F32 = jnp.float32
BF16 = jnp.bfloat16

D_MODEL = 1024
DEPTH = 1
GRID_W = 64
HEAD_DIM = 64
D_A = D_MODEL
N_HEADS = D_A // HEAD_DIM
D_B = D_MODEL // 2
POOL_WINDOWS = (2, 4, 8, 16)
POOL_GROUP_DIM = D_B // len(POOL_WINDOWS)
LORA_W = 64
LORA_A = 64
LORA_G = 128
D_FF = 2816
N_MOD = 9
SHIFT_W = 3 * D_A + 2 * LORA_W + 2 * LORA_A + LORA_G
ALPHA = (2 * DEPTH) ** 0.25
LN_EPS = 1e-5
GN_EPS = 64e-5
DECAY_SCALE = math.exp(-0.5)

LANES = 128
HALO = 64
POOL_HALO = 8
POOL_PAD = 64
CHUNK = 64
PAIR = 2 * HEAD_DIM
SUBLANES = 8
VMEM_LIMIT = 56 * 1024 * 1024
FFN_TILE = 512
MIX_TILE = 256


def _cparams(sem):
    return pltpu.CompilerParams(dimension_semantics=sem, vmem_limit_bytes=VMEM_LIMIT)


def _mod_p2(x, n):
    assert n & (n - 1) == 0
    return x & (n - 1)


def _div_p2(x, n):
    assert n & (n - 1) == 0
    return x >> (n.bit_length() - 1)


def _sigmoid(x):
    return 0.5 * jnp.tanh(0.5 * x) + 0.5


def _silu(x):
    return x / (1.0 + jnp.exp(-x))


_NN = (((1,), (0,)), ((), ()))
_NT = (((1,), (1,)), ((), ()))
_TN = (((0,), (0,)), ((), ()))


def _dot(a, b, dims=_NN):
    return lax.dot_general(a.astype(BF16), b.astype(BF16), dims, preferred_element_type=F32)


def _layer_norm(z, g, b):
    mu = jnp.mean(z, axis=-1, keepdims=True)
    d = z - mu
    var = jnp.mean(d * d, axis=-1, keepdims=True)
    return d * lax.rsqrt(var + LN_EPS) * g + b


def _head_sum(x, ones_bd):
    xb = x.astype(BF16)
    return jnp.concatenate([jnp.dot(xb[:, c * LANES:(c + 1) * LANES], ones_bd, preferred_element_type=F32)
                            for c in range(x.shape[1] // LANES)], axis=1)


def _mod_kernel(c_ref, w_ref, b_ref, o_ref):
    o_ref[...] = _dot(_silu(c_ref[...]), w_ref[...]) + b_ref[...]


def _modulation(cvecs, w_mod, b_mod):
    rows = cvecs.shape[0]
    n = w_mod.shape[1]
    tn = 1024
    return pl.pallas_call(
        _mod_kernel,
        grid=(n // tn,),
        in_specs=[pl.BlockSpec((rows, D_MODEL), lambda j: (0, 0)),
                  pl.BlockSpec((D_MODEL, tn), lambda j: (0, j)),
                  pl.BlockSpec((1, tn), lambda j: (0, j))],
        out_specs=pl.BlockSpec((rows, tn), lambda j: (0, j)),
        out_shape=jax.ShapeDtypeStruct((rows, n), F32),
        compiler_params=_cparams(("arbitrary",)),
        name="modulation",
    )(cvecs, w_mod, b_mod.reshape(1, n))


FFN_IN_SLAB = 512
FFN_OUT_SLAB = 256


def _ffn_kernel(xc_ref, xl_ref, mod_ref, wi_hbm, wo_hbm, lng_ref, lnb_ref, oc_ref, ol_ref,
                wi_ref, wo_ref, stage_i, stage_o, sem_i, sem_o, *, mi, n_ctx, which):
    step = pl.program_id(0)
    is_ctx = step < n_ctx

    def in_copy(k):
        return pltpu.make_async_copy(wi_hbm.at[which, :, k * FFN_IN_SLAB:(k + 1) * FFN_IN_SLAB],
                                     stage_i.at[k % 2], sem_i.at[k % 2])

    def out_copy(k):
        return pltpu.make_async_copy(wo_hbm.at[which, k * FFN_OUT_SLAB:(k + 1) * FFN_OUT_SLAB, :],
                                     stage_o.at[k % 2], sem_o.at[k % 2])

    @pl.when(step == 0)
    def _():
        n_in, n_out = 2 * D_FF // FFN_IN_SLAB, D_FF // FFN_OUT_SLAB
        in_copy(0).start()
        out_copy(0).start()
        for k in range(n_in):
            if k + 1 < n_in:
                in_copy(k + 1).start()
            in_copy(k).wait()
            wi_ref[:, k * FFN_IN_SLAB:(k + 1) * FFN_IN_SLAB] = stage_i[k % 2].astype(BF16)
        for k in range(n_out):
            if k + 1 < n_out:
                out_copy(k + 1).start()
            out_copy(k).wait()
            wo_ref[k * FFN_OUT_SLAB:(k + 1) * FFN_OUT_SLAB, :] = stage_o[k % 2].astype(BF16)

    def sub_layer(x_ref, o_ref):
        x = x_ref[...]
        h = (x * (1.0 + mod_ref[mi + 1:mi + 2, :]) + mod_ref[mi:mi + 1, :]).astype(BF16)
        gate = jnp.dot(h, wi_ref[:, :D_FF], preferred_element_type=F32)
        up = jnp.dot(h, wi_ref[:, D_FF:], preferred_element_type=F32)
        ff = _dot(_silu(gate) * up, wo_ref[...])
        z = ALPHA * x + 0.5 * mod_ref[mi + 2:mi + 3, :] * ff
        o_ref[...] = _layer_norm(z, lng_ref[...], lnb_ref[...])

    @pl.when(is_ctx)
    def _():
        sub_layer(xc_ref, oc_ref)

    @pl.when(jnp.logical_not(is_ctx))
    def _():
        sub_layer(xl_ref, ol_ref)


def _ffn(x_ctx, x_lat, mod, lat_seq, w_in, w_out, ln_g, ln_b, which, mi, tm):
    n_ctx, n_lat = x_ctx.shape[0] // tm, x_lat.shape[0] // tm
    assert x_ctx.shape[0] % tm == 0 and x_lat.shape[0] % tm == 0 and lat_seq % tm == 0
    assert (2 * D_FF) % FFN_IN_SLAB == 0 and D_FF % FFN_OUT_SLAB == 0
    resident = lambda shape: pl.BlockSpec(shape, lambda i: (0,) * len(shape), pipeline_mode=pl.Buffered(1))
    in_hbm = pl.BlockSpec(memory_space=pl.ANY)
    ctx_tile = pl.BlockSpec((tm, D_MODEL), lambda i: (jnp.minimum(i, n_ctx - 1), 0))
    lat_tile = pl.BlockSpec((tm, D_MODEL), lambda i: (jnp.maximum(i - n_ctx, 0), 0))
    mod_row = lambda i: jnp.where(i < n_ctx, 0, 1 + (jnp.maximum(i - n_ctx, 0) * tm) // lat_seq)
    return pl.pallas_call(
        functools.partial(_ffn_kernel, mi=mi, n_ctx=n_ctx, which=which),
        grid=(n_ctx + n_lat,),
        in_specs=[ctx_tile, lat_tile,
                  pl.BlockSpec((None, N_MOD, D_MODEL), lambda i: (mod_row(i), 0, 0)),
                  in_hbm, in_hbm,
                  resident((1, D_MODEL)), resident((1, D_MODEL))],
        out_specs=[ctx_tile, lat_tile],
        out_shape=[jax.ShapeDtypeStruct(x_ctx.shape, F32), jax.ShapeDtypeStruct(x_lat.shape, F32)],
        scratch_shapes=[pltpu.VMEM((D_MODEL, 2 * D_FF), BF16), pltpu.VMEM((D_FF, D_MODEL), BF16),
                        pltpu.VMEM((2, D_MODEL, FFN_IN_SLAB), F32), pltpu.VMEM((2, FFN_OUT_SLAB, D_MODEL), F32),
                        pltpu.SemaphoreType.DMA((2,)), pltpu.SemaphoreType.DMA((2,))],
        compiler_params=_cparams(("arbitrary",)),
        name="ffn",
    )(x_ctx, x_lat, mod, w_in, w_out, ln_g.reshape(1, -1), ln_b.reshape(1, -1))


def _prep_kernel(x_ref, xprev_ref, xnext_ref, mod_ref, w_ref, mu_ref, wup_ref, aup_ref, w0_ref, a0_ref,
                 kk_ref, ka_ref, rk_ref, ones_ref,
                 r_o, v_o, kk_o, lw_o, kd_o, b_o, bonus_o, gs_o, pool_o, gate_o, ext_scr,
                 *, seq_len, latent, tm, mi):
    i = pl.program_id(0)
    t0 = _mod_p2(i * tm, seq_len)
    shift = mod_ref[mi:mi + 1, :]
    scale = mod_ref[mi + 1:mi + 2, :]

    def modulated(ref):
        return (ref[...] * (1.0 + scale) + shift).astype(BF16)

    h = modulated(x_ref)
    h_ext = jnp.concatenate([modulated(xprev_ref), h, modulated(xnext_ref)], axis=0) if latent else None

    def project(c0, c1):
        if latent:
            ext_scr[:, c0:c1] = jnp.dot(h_ext, w_ref[:, c0:c1], preferred_element_type=F32)
            ext_scr[0:HALO, c0:c1] = jnp.where(t0 == 0, 0.0, ext_scr[0:HALO, c0:c1])
            ext_scr[HALO + tm:, c0:c1] = jnp.where(t0 + tm == seq_len, 0.0, ext_scr[HALO + tm:, c0:c1])
        else:
            ext_scr[0:HALO, c0:c1] = jnp.zeros((HALO, c1 - c0), F32)
            ext_scr[HALO:HALO + tm, c0:c1] = jnp.dot(h, w_ref[:, c0:c1], preferred_element_type=F32)
            ext_scr[HALO + tm:, c0:c1] = jnp.zeros((HALO, c1 - c0), F32)

    t = t0 + lax.broadcasted_iota(jnp.int32, (tm, LANES), 0)
    lane = lax.broadcasted_iota(jnp.int32, (1, LANES), 1)
    n_src = 4 if latent else 2
    if latent:
        col = _mod_p2(t, GRID_W)
        has_before = jnp.where(col > 0, 1.0, 0.0)
        has_after = jnp.where(col < GRID_W - 1, 1.0, 0.0)

    def mixed_cols(c):
        sl = slice(c * LANES, (c + 1) * LANES)
        mu = mu_ref[:, sl]
        coef = [jnp.where(_mod_p2(lane, n_src) == s, mu, 0.0) for s in range(n_src)]
        before = ext_scr[HALO - 1:HALO - 1 + tm, sl]
        after = ext_scr[HALO + 1:HALO + 1 + tm, sl]
        if latent:
            return (ext_scr[HALO:HALO + tm, sl] * (1.0 - mu) + (before * has_before) * coef[0]
                    + (after * has_after) * coef[1]
                    + ext_scr[0:tm, sl] * coef[2] + ext_scr[2 * HALO:2 * HALO + tm, sl] * coef[3])
        return ext_scr[HALO:HALO + tm, sl] * (1.0 - mu) + before * coef[0] + after * coef[1]

    def mixed(c0, c1):
        return jnp.concatenate([mixed_cols(c) for c in range(c0, c1)], axis=1)

    nb = D_A // LANES
    project(0, D_A)
    project(D_A, 2 * D_A)
    r = mixed(0, nb)
    project(2 * D_A, 3 * D_A)
    k = mixed(nb, 2 * nb)
    project(3 * D_A, SHIFT_W)
    v = mixed(2 * nb, 3 * nb)
    pool_o[...] = jnp.dot(h, w_ref[:, SHIFT_W:SHIFT_W + D_B], preferred_element_type=F32)
    gate_o[...] = jnp.dot(h, w_ref[:, SHIFT_W + D_B:], preferred_element_type=F32)
    w_down = mixed_cols(3 * nb)
    a_down = mixed_cols(3 * nb + 1)
    g_down = mixed_cols(3 * nb + 2)

    ones_bd = ones_ref[...]
    r_o[...] = r.astype(r_o.dtype)
    v_o[...] = v
    kkraw = k * kk_ref[...]
    ss = _head_sum(kkraw * kkraw, ones_bd)
    kk = kkraw / jnp.maximum(jnp.sqrt(ss), 1e-12)
    kk_o[...] = kk
    bonus_o[...] = (_head_sum(r * k * rk_ref[...], ones_bd) * v).astype(bonus_o.dtype)
    gs_o[...] = _sigmoid(g_down)

    w_raw = _dot(jnp.tanh(w_down), wup_ref[...])
    a_raw = _dot(a_down, aup_ref[...])
    for d in range(2):
        sl = slice(d * D_A, (d + 1) * D_A)
        lw_o[d] = -DECAY_SCALE * _sigmoid(w0_ref[d:d + 1, :] + w_raw[:, sl])
        a = _sigmoid(a0_ref[d:d + 1, :] + a_raw[:, sl])
        kd_o[d] = k * (1.0 + (a - 1.0) * ka_ref[...])
        b_o[d] = kk * a


def _prep(x, mod, mod_of_tile, p, seq_len, latent, tm, mi):
    rows = x.shape[0]
    assert seq_len % tm == 0 and tm % HALO == 0 and (latent or tm == seq_len)
    hb = tm // HALO
    n_halo_blocks = rows // HALO
    n = p['w_mix_in'].shape[1]
    row1 = lambda a: a.reshape(1, -1)
    full = lambda shape: pl.BlockSpec(shape, lambda i: (0,) * len(shape), pipeline_mode=pl.Buffered(1))
    out_cols = lambda width: pl.BlockSpec((tm, width), lambda i: (i, 0))
    out_dir = pl.BlockSpec((2, tm, D_A), lambda i: (0, i, 0))
    sds = jax.ShapeDtypeStruct
    return pl.pallas_call(
        functools.partial(_prep_kernel, seq_len=seq_len, latent=latent, tm=tm, mi=mi),
        grid=(rows // tm,),
        in_specs=[pl.BlockSpec((tm, D_MODEL), lambda i: (i, 0)),
                  pl.BlockSpec((HALO, D_MODEL), lambda i: (jnp.maximum(i * hb - 1, 0), 0)),
                  pl.BlockSpec((HALO, D_MODEL), lambda i: (jnp.minimum((i + 1) * hb, n_halo_blocks - 1), 0)),
                  pl.BlockSpec((None, N_MOD, D_MODEL), lambda i: (mod_of_tile(i, tm), 0, 0)),
                  full((D_MODEL, n)),
                  full((1, SHIFT_W)), full((LANES, 2 * D_A)), full((LANES, 2 * D_A)),
                  full((2, D_A)), full((2, D_A)), full((1, D_A)), full((1, D_A)), full((1, D_A)),
                  full((LANES, LANES))],
        out_specs=[out_cols(D_A), out_cols(D_A), out_cols(D_A), out_dir, out_dir, out_dir, out_cols(D_A),
                   out_cols(LORA_G), out_cols(D_B), out_cols(n - SHIFT_W - D_B)],
        out_shape=[sds((rows, D_A), BF16), sds((rows, D_A), F32), sds((rows, D_A), F32),
                   sds((2, rows, D_A), F32), sds((2, rows, D_A), F32), sds((2, rows, D_A), F32),
                   sds((rows, D_A), BF16), sds((rows, LORA_G), F32),
                   sds((rows, D_B), F32), sds((rows, n - SHIFT_W - D_B), F32)],
        scratch_shapes=[pltpu.VMEM((tm + 2 * HALO, SHIFT_W), F32)],
        compiler_params=_cparams(("parallel",)),
        name="mix_in_prep",
    )(x, x, x, mod, p['w_mix_in'], row1(p['mu_shift']), p['wup_bd'], p['aup_bd'], p['w0'], p['a0'],
      row1(p['k_k']), row1(p['k_a']), row1(p['r_k']), p['ones_bd'])


SCAN_GROUP = 8
SCAN_SEQS = 2
SCAN_CHUNKS = 4


def _scan_kernel(*refs, has_s0, want_state, group, nseq, nsub):
    if has_s0:
        (r_ref, v_ref, kk_ref, lw_ref, kd_ref, b_ref, s0_ref), rest = refs[:7], refs[7:]
    else:
        (r_ref, v_ref, kk_ref, lw_ref, kd_ref, b_ref), rest = refs[:6], refs[6:]
    if want_state:
        y_ref, sout_ref, s_scr = rest
    else:
        y_ref, s_scr = rest
    d = pl.program_id(0)
    c = pl.program_id(3)
    C = CHUNK

    chains = [(bi, p) for bi in range(nseq) for p in range(group)]
    keys = [(bi, slice(p * PAIR, (p + 1) * PAIR)) for bi, p in chains]

    @pl.when(c == 0)
    def _():
        if has_s0:
            z = jnp.zeros((HEAD_DIM, HEAD_DIM), F32)
            for j, (bi, p) in enumerate(chains):
                top = jnp.concatenate([s0_ref[bi, 2 * p], z], axis=1)
                bot = jnp.concatenate([z, s0_ref[bi, 2 * p + 1]], axis=1)
                s_scr[j] = jnp.concatenate([top, bot], axis=0)
        else:
            s_scr[...] = jnp.zeros_like(s_scr)

    row = lax.broadcasted_iota(jnp.int32, (PAIR, PAIR), 0)
    col = lax.broadcasted_iota(jnp.int32, (PAIR, PAIR), 1)
    same_head = _div_p2(row, HEAD_DIM) == _div_p2(col, HEAD_DIM)
    sign = 1 - 2 * d
    order = (_mod_p2(row, C) - _mod_p2(col, C)) * sign
    strict = (order > 0) & same_head
    incl = (order >= 0)[:C, :]
    head0 = lax.broadcasted_iota(jnp.int32, (C, PAIR), 1) < HEAD_DIM

    def stack(z):
        return jnp.concatenate([jnp.where(head0, z, 0.0), jnp.where(head0, 0.0, z)], axis=0)

    def each(fn, *lists):
        return [fn(*args) for args in zip(*lists)]

    trow = lax.broadcasted_iota(jnp.int32, (C, PAIR), 0)
    rev = d.astype(F32)

    def cumsum(lw, tot):
        acc = lw
        s = 1
        while s < C:
            acc = acc + jnp.where(trow >= s, pltpu.roll(acc, s, axis=0), 0.0)
            s *= 2
        return rev * (tot + lw) + (1.0 - 2.0 * rev) * acc

    def prologue(off):
        def rd(ref, key):
            return ref[key[0], pl.ds(off, C), key[1]]

        lw = [rd(lw_ref, k) for k in keys]
        tot = [jnp.sum(z, axis=0, keepdims=True) for z in lw]
        cs = each(cumsum, lw, tot)
        w_inv = [jnp.exp(-z) for z in cs]
        w_rest = each(lambda t, z: jnp.exp(t - z), tot, cs)
        r_t = each(lambda k, z: rd(r_ref, k).astype(F32) * jnp.exp(z), keys, cs)
        kd = [rd(kd_ref, k) for k in keys]
        b = [rd(b_ref, k) for k in keys]
        kk_st = each(lambda k, z, l: stack(rd(kk_ref, k) * jnp.exp(z - l)), keys, cs, lw)
        v_st = [stack(rd(v_ref, k)) for k in keys]
        kb_st = each(lambda kdi, bi, wi: jnp.concatenate([stack(kdi * wi), stack(bi * wi)], axis=0),
                     kd, b, w_inv)
        kdec_st = each(lambda kdi, wr: stack(kdi * wr), kd, w_rest)
        bdec_st = each(lambda bi, wr: stack(bi * wr), b, w_rest)
        return tot, r_t, kk_st, v_st, kb_st, kdec_st, bdec_st

    def precompute(pro):
        tot, r_t, kk_st, v_st, kb_st, kdec_st, bdec_st = pro
        m = each(lambda a, rt, kb: _dot(jnp.concatenate([a, rt], axis=0), kb, _NT), kk_st, r_t, kb_st)
        x = [jnp.where(strict, -z[:2 * C, 2 * C:], 0.0) for z in m]
        p_b = [jnp.where(incl, z[2 * C:, 2 * C:], 0.0) for z in m]
        apk_mask = jnp.concatenate([strict, incl], axis=0)
        apk_v = each(lambda z, vs: _dot(jnp.where(apk_mask, z[:, :2 * C], 0.0), vs), m, v_st)

        ti, tj = _mod_p2(row, C), _mod_p2(col, C)
        eye = jnp.where(row == col, 1.0, 0.0)

        def off_block(n):
            return (_div_p2(ti, 2 * n) == _div_p2(tj, 2 * n)) & (_div_p2(ti, n) != _div_p2(tj, n))

        t_inv = [eye + jnp.where(off_block(1), z, 0.0) for z in x]
        n = 2
        while n < C:
            keep = off_block(n)
            mid = each(lambda z, t: _dot(jnp.where(keep, z, 0.0), t), x, t_inv)
            t_inv = each(lambda t, mm: t + _dot(t, mm), t_inv, mid)
            n *= 2
        pbt = each(lambda pb, t: _dot(pb, t), p_b, t_inv)
        qq = each(lambda t, pt, a, av: _dot(jnp.concatenate([t, pt], axis=0),
                                            jnp.concatenate([a, av[:2 * C]], axis=1)),
                  t_inv, pbt, kk_st, apk_v)
        q = [z[:2 * C] for z in qq]
        ry = each(lambda rt, av, z: jnp.concatenate([rt, av[2 * C:]], axis=1) - z[2 * C:], r_t, apk_v, qq)
        gq = each(lambda qi, bd: _dot(qi, bd, _TN), q, bdec_st)
        s_add = each(lambda vs, kd, g: _dot(vs, kd, _TN) - g[PAIR:], v_st, kdec_st, gq)

        return ry, tot, gq, s_add

    offs = [pl.multiple_of(C * (n + d * (nsub - 1 - 2 * n)), C) for n in range(nsub)]
    pre = [precompute(pro) for pro in [prologue(off) for off in offs]]
    s_new = [s_scr[j] for j in range(len(chains))]
    for off, (ry, tot, gq, s_add) in zip(offs, pre):
        s_old = s_new
        for j, (bi, sl) in enumerate(keys):
            y_ref[bi, pl.ds(off, C), sl] = _dot(ry[j][:, :PAIR], s_old[j], _NT) + ry[j][:, PAIR:]
        s_new = each(lambda s, t, g, sa: s * jnp.exp(t) - _dot(s, g[:PAIR]) + sa, s_old, tot, gq, s_add)
    for j in range(len(chains)):
        s_scr[j] = s_new[j]

    if want_state:
        @pl.when(c == pl.num_programs(3) - 1)
        def _():
            for j, (bi, p) in enumerate(chains):
                sout_ref[bi, 2 * p] = s_new[j][:HEAD_DIM, :HEAD_DIM]
                sout_ref[bi, 2 * p + 1] = s_new[j][HEAD_DIM:, HEAD_DIM:]


def _scan(r, v, kk, lw, kd, b, s0, batch, seq_len, want_state):
    rows = r.shape[0]
    nsub = SCAN_CHUNKS
    nc = seq_len // (CHUNK * nsub)
    group = SCAN_GROUP
    nseq = min(SCAN_SEQS, batch)
    n_groups = D_A // (PAIR * group)
    gw = group * PAIR

    def chunk_idx(d, c):
        return c + d * (nc - 1 - 2 * c)

    shared = pl.BlockSpec((nseq, nsub * CHUNK, gw), lambda d, bi, p, c: (bi, chunk_idx(d, c), p))
    per_dir = pl.BlockSpec((None, nseq, nsub * CHUNK, gw), lambda d, bi, p, c: (d, bi, chunk_idx(d, c), p))
    state = pl.BlockSpec((nseq, None, 2 * group, HEAD_DIM, HEAD_DIM), lambda d, bi, p, c: (bi, d, p, 0, 0))
    in_specs = [shared, shared, shared, per_dir, per_dir, per_dir]
    seqs = lambda z: z.reshape(z.shape[:-2] + (batch, seq_len, D_A))
    args = [seqs(z) for z in (r, v, kk, lw, kd, b)]
    if s0 is not None:
        in_specs.append(state)
        args.append(s0)
    out_specs = [per_dir]
    out_shape = [jax.ShapeDtypeStruct((2, batch, seq_len, D_A), F32)]
    if want_state:
        out_specs.append(state)
        out_shape.append(jax.ShapeDtypeStruct((batch, 2, N_HEADS, HEAD_DIM, HEAD_DIM), F32))
    outs = pl.pallas_call(
        functools.partial(_scan_kernel, has_s0=s0 is not None, want_state=want_state, group=group, nseq=nseq,
                          nsub=nsub),
        grid=(2, batch // nseq, n_groups, nc),
        in_specs=in_specs,
        out_specs=out_specs,
        out_shape=out_shape,
        scratch_shapes=[pltpu.VMEM((nseq * group, PAIR, PAIR), F32)],
        compiler_params=_cparams(("parallel", "parallel", "parallel", "arbitrary")),
        name="rwkv7_scan",
    )(*args)
    return [outs[0].reshape(2, rows, D_A)] + list(outs[1:])


def _post_kernel(y_ref, bonus_ref, gs_ref, gup_ref, pool_ref, pprev_ref, pnext_ref, gates_ref, x_ref, mod_ref,
                 lnxg_ref, lnxb_ref, worwkv_ref, wpool_ref, pscale_ref, wopool_ref, wout_ref,
                 lng_ref, lnb_ref, ones_ref, band_ref, o_ref, ext_scr, *, seq_len, tm):
    i = pl.program_id(0)
    ones_bd = ones_ref[...]
    y = y_ref[0] + y_ref[1]
    mu = _head_sum(y, ones_bd) * (1.0 / HEAD_DIM)

    t0 = _mod_p2(i * tm, seq_len)
    lo_edge, hi_edge = POOL_PAD - POOL_HALO, POOL_PAD + tm + POOL_HALO
    ext_scr[0:lo_edge, :] = jnp.zeros((lo_edge, D_B), F32)
    ext_scr[lo_edge:POOL_PAD, :] = jnp.where(t0 == 0, 0.0, pprev_ref[...])
    ext_scr[POOL_PAD:POOL_PAD + tm, :] = pool_ref[...]
    ext_scr[POOL_PAD + tm:hi_edge, :] = jnp.where(t0 + tm == seq_len, 0.0, pnext_ref[...])
    ext_scr[hi_edge:, :] = jnp.zeros((lo_edge, D_B), F32)
    ext = ext_scr[...].astype(BF16)
    groups = [slice(gi * POOL_GROUP_DIM, (gi + 1) * POOL_GROUP_DIM) for gi in range(len(POOL_WINDOWS))]
    win = [jnp.dot(band_ref[gi], ext[:, sl], preferred_element_type=F32) for gi, sl in enumerate(groups)]

    dlt = y - mu
    var = _head_sum(dlt * dlt, ones_bd) * (1.0 / HEAD_DIM)

    t = t0 + lax.broadcasted_iota(jnp.int32, (tm, LANES), 0)
    us = []
    for gi, w in enumerate(POOL_WINDOWS):
        cnt = jnp.minimum(t + (w - w // 2), seq_len) - jnp.maximum(t - w // 2, 0)
        resid = win[gi] / cnt.astype(F32) - pool_ref[:, groups[gi]]
        us.append(_dot(resid, wpool_ref[gi]))

    yn = dlt * lax.rsqrt(var + GN_EPS) * lnxg_ref[...] + lnxb_ref[...]
    ya = _dot((yn + bonus_ref[...].astype(F32)) * _dot(gs_ref[...], gup_ref[...]), worwkv_ref[...])
    u = jnp.concatenate(us, axis=1) * pscale_ref[...]
    yb = _dot(u, wopool_ref[...])

    gates = gates_ref[...]
    merged = _sigmoid(gates[:, :D_MODEL]) * ya + _sigmoid(gates[:, D_MODEL:]) * yb
    mix = _dot(merged, wout_ref[...])
    z = ALPHA * x_ref[...] + mod_ref[5:6, :] * mix
    o_ref[...] = _layer_norm(z, lng_ref[...], lnb_ref[...])


def _pool_bands(tm):
    i = lax.broadcasted_iota(jnp.int32, (tm, tm + 2 * POOL_PAD), 0)
    j = lax.broadcasted_iota(jnp.int32, (tm, tm + 2 * POOL_PAD), 1) - POOL_PAD
    return jnp.stack([((j >= i - w // 2) & (j < i + w - w // 2)).astype(BF16) for w in POOL_WINDOWS])


def _post(y, bonus, gs, pool_in, gates, x, mod, mod_of_tile, p, seq_len, tm):
    rows = x.shape[0]
    assert seq_len % tm == 0 and tm % POOL_HALO == 0
    hb = tm // POOL_HALO
    n_halo_blocks = rows // POOL_HALO
    row1 = lambda a: a.reshape(1, -1)
    full = lambda shape: pl.BlockSpec(shape, lambda i: (0,) * len(shape))
    tile = lambda width: pl.BlockSpec((tm, width), lambda i: (i, 0))
    return pl.pallas_call(
        functools.partial(_post_kernel, seq_len=seq_len, tm=tm),
        grid=(rows // tm,),
        in_specs=[pl.BlockSpec((2, tm, D_A), lambda i: (0, i, 0)), tile(D_A), tile(LORA_G), full((LORA_G, D_A)),
                  tile(D_B),
                  pl.BlockSpec((POOL_HALO, D_B), lambda i: (jnp.maximum(i * hb - 1, 0), 0)),
                  pl.BlockSpec((POOL_HALO, D_B), lambda i: (jnp.minimum((i + 1) * hb, n_halo_blocks - 1), 0)),
                  tile(2 * D_MODEL), tile(D_MODEL),
                  pl.BlockSpec((None, N_MOD, D_MODEL), lambda i: (mod_of_tile(i, tm), 0, 0)),
                  full((1, D_A)), full((1, D_A)), full((D_A, D_MODEL)),
                  full((len(POOL_WINDOWS), POOL_GROUP_DIM, POOL_GROUP_DIM)), full((1, D_B)),
                  full((D_B, D_MODEL)), full((D_MODEL, D_MODEL)), full((1, D_MODEL)), full((1, D_MODEL)),
                  full((LANES, LANES)), full((len(POOL_WINDOWS), tm, tm + 2 * POOL_PAD))],
        out_specs=tile(D_MODEL),
        out_shape=jax.ShapeDtypeStruct((rows, D_MODEL), F32),
        scratch_shapes=[pltpu.VMEM((tm + 2 * POOL_PAD, D_B), F32)],
        compiler_params=_cparams(("parallel",)),
        name="mixer_out",
    )(y, bonus, gs, p['g_up'], pool_in, pool_in, pool_in, gates, x, mod, row1(p['lnx_g']), row1(p['lnx_b']),
      p['w_o_rwkv'], p['w_pool'], row1(p['pool_scale']), p['w_o_pool'], p['w_out'],
      row1(p['ln_g'][1]), row1(p['ln_b'][1]), p['ones_bd'], _pool_bands(tm))


def _mixer(x1, mod, mod_of_tile, s0, batch, seq_len, latent, want_state, p):
    tm = MIX_TILE
    r, v, kk, lw, kd, b, bonus, gs, pool_in, gates = _prep(x1, mod, mod_of_tile, p, seq_len, latent, tm, 3)
    outs = _scan(r, v, kk, lw, kd, b, s0, batch, seq_len, want_state)
    x2 = _post(outs[0], bonus, gs, pool_in, gates, x1, mod, mod_of_tile, p, seq_len, min(2 * tm, seq_len))
    return x2, (outs[1] if want_state else None)


def _block_diag2(w):
    z = jnp.zeros_like(w[0])
    return jnp.concatenate([jnp.concatenate([w[0], z], axis=1), jnp.concatenate([z, w[1]], axis=1)], axis=0)


def kernel(x_prompt, x_sample, c, state_rwkv, c_ctx, w_mod, b_mod, ln_g, ln_b, ffn_in, ffn_out, w_mix_in,
           mu_shift, w0, w_up, a0, a_up, g_up, k_k, k_a, r_k, lnx_g, lnx_b, w_o_rwkv, w_pool, pool_scale,
           w_o_pool, w_out):
    batch, seq, _ = x_prompt.shape
    dec_batch, dec_seq, _ = x_sample.shape
    y_p = x_prompt.reshape(batch * seq, D_MODEL)
    y_s = x_sample.reshape(dec_batch * dec_seq, D_MODEL)
    cvecs = jnp.concatenate([c_ctx[None, :], c, jnp.zeros((SUBLANES - 1 - dec_batch, D_MODEL), F32)], axis=0)
    ii = lax.broadcasted_iota(jnp.int32, (LANES, LANES), 0) // HEAD_DIM
    jj = lax.broadcasted_iota(jnp.int32, (LANES, LANES), 1) // HEAD_DIM
    ones_bd = (ii == jj).astype(BF16)
    ctx_states = []
    for l in range(DEPTH):
        p = {
            'ln_g': ln_g[l], 'ln_b': ln_b[l],
            'ffn_in': ffn_in[l], 'ffn_out': ffn_out[l],
            'w_mix_in': w_mix_in[l].astype(BF16),
            'mu_shift': mu_shift[l], 'w0': w0[l], 'a0': a0[l],
            'wup_bd': _block_diag2(w_up[l]).astype(BF16), 'aup_bd': _block_diag2(a_up[l]).astype(BF16),
            'g_up': g_up[l].astype(BF16), 'k_k': k_k[l], 'k_a': k_a[l], 'r_k': r_k[l],
            'lnx_g': lnx_g[l], 'lnx_b': lnx_b[l], 'w_o_rwkv': w_o_rwkv[l].astype(BF16),
            'w_pool': w_pool[l].astype(BF16), 'pool_scale': pool_scale[l],
            'w_o_pool': w_o_pool[l].astype(BF16), 'w_out': w_out[l].astype(BF16), 'ones_bd': ones_bd,
        }
        mod = _modulation(cvecs, w_mod[l], b_mod[l]).reshape(SUBLANES, N_MOD, D_MODEL)
        y_p, y_s = _ffn(y_p, y_s, mod, dec_seq, p['ffn_in'], p['ffn_out'], p['ln_g'][0], p['ln_b'][0], 0, 0, FFN_TILE)
        y_p, s_ctx = _mixer(y_p, mod, lambda i, tm: 0, None, batch, seq, False, True, p)
        ctx_states.append(s_ctx)
        y_s, _ = _mixer(y_s, mod, lambda i, tm: 1 + (i * tm) // dec_seq, state_rwkv[:, l],
                        dec_batch, dec_seq, True, False, p)
        y_p, y_s = _ffn(y_p, y_s, mod, dec_seq, p['ffn_in'], p['ffn_out'], p['ln_g'][2], p['ln_b'][2], 1, 6, FFN_TILE)
    new_state = jnp.stack(ctx_states, axis=1).astype(x_prompt.dtype)
    return (y_p.reshape(batch, seq, D_MODEL), y_s.reshape(dec_batch, dec_seq, D_MODEL), new_state)
```

```python
import functools
import math

import jax
import jax.numpy as jnp
from jax import lax
from jax.experimental import pallas as pl
from jax.experimental.pallas import tpu as pltpu

F32 = jnp.float32
BF16 = jnp.bfloat16

D_MODEL = 1024
DEPTH = 1
GRID_W = 64
HEAD_DIM = 64
D_A = D_MODEL
N_HEADS = D_A // HEAD_DIM
D_B = D_MODEL // 2
POOL_WINDOWS = (2, 4, 8, 16)
POOL_GROUP_DIM = D_B // len(POOL_WINDOWS)
LORA_W = 64
LORA_A = 64
LORA_G = 128
D_FF = 2816
N_MOD = 9
SHIFT_W = 3 * D_A + 2 * LORA_W + 2 * LORA_A + LORA_G
ALPHA = (2 * DEPTH) ** 0.25
LN_EPS = 1e-5
GN_EPS = 64e-5
DECAY_SCALE = math.exp(-0.5)

LANES = 128
HALO = 64
POOL_HALO = 8
POOL_PAD = 64
CHUNK = 64
PAIR = 2 * HEAD_DIM
SUBLANES = 8
VMEM_LIMIT = 56 * 1024 * 1024
FFN_TILE = 512
MIX_TILE = 256


def _cparams(sem):
    return pltpu.CompilerParams(dimension_semantics=sem, vmem_limit_bytes=VMEM_LIMIT)


def _mod_p2(x, n):
    assert n & (n - 1) == 0
    return x & (n - 1)


def _div_p2(x, n):
    assert n & (n - 1) == 0
    return x >> (n.bit_length() - 1)


def _sigmoid(x):
    return 0.5 * jnp.tanh(0.5 * x) + 0.5


def _silu(x):
    return x / (1.0 + jnp.exp(-x))


_NN = (((1,), (0,)), ((), ()))
_NT = (((1,), (1,)), ((), ()))
_TN = (((0,), (0,)), ((), ()))


def _dot(a, b, dims=_NN):
    return lax.dot_general(a.astype(BF16), b.astype(BF16), dims, preferred_element_type=F32)


def _layer_norm(z, g, b):
    mu = jnp.mean(z, axis=-1, keepdims=True)
    d = z - mu
    var = jnp.mean(d * d, axis=-1, keepdims=True)
    return d * lax.rsqrt(var + LN_EPS) * g + b


def _head_sum(x, ones_bd):
    xb = x.astype(BF16)
    return jnp.concatenate([jnp.dot(xb[:, c * LANES:(c + 1) * LANES], ones_bd, preferred_element_type=F32)
                            for c in range(x.shape[1] // LANES)], axis=1)


def _mod_kernel(c_ref, w_ref, b_ref, o_ref):
    o_ref[...] = _dot(_silu(c_ref[...]), w_ref[...]) + b_ref[...]


def _modulation(cvecs, w_mod, b_mod):
    rows = cvecs.shape[0]
    n = w_mod.shape[1]
    tn = 1024
    return pl.pallas_call(
        _mod_kernel,
        grid=(n // tn,),
        in_specs=[pl.BlockSpec((rows, D_MODEL), lambda j: (0, 0)),
                  pl.BlockSpec((D_MODEL, tn), lambda j: (0, j)),
                  pl.BlockSpec((1, tn), lambda j: (0, j))],
        out_specs=pl.BlockSpec((rows, tn), lambda j: (0, j)),
        out_shape=jax.ShapeDtypeStruct((rows, n), F32),
        compiler_params=_cparams(("arbitrary",)),
        name="modulation",
    )(cvecs, w_mod, b_mod.reshape(1, n))


FFN_IN_SLAB = 128
FFN_OUT_SLAB = 256


def _ffn_kernel(xc_ref, xl_ref, mod_ref, wi_hbm, wo_hbm, lng_ref, lnb_ref, oc_ref, ol_ref,
                wi_ref, wo_ref, stage_i, stage_o, sem_i, sem_o, *, mi, n_ctx, which):
    step = pl.program_id(0)
    is_ctx = step < n_ctx

    def in_copy(k):
        return pltpu.make_async_copy(wi_hbm.at[which, k * FFN_IN_SLAB:(k + 1) * FFN_IN_SLAB, :],
                                     stage_i.at[k % 2], sem_i.at[k % 2])

    def out_copy(k):
        return pltpu.make_async_copy(wo_hbm.at[which, k * FFN_OUT_SLAB:(k + 1) * FFN_OUT_SLAB, :],
                                     stage_o.at[k % 2], sem_o.at[k % 2])

    @pl.when(step == 0)
    def _():
        n_in, n_out = D_MODEL // FFN_IN_SLAB, D_FF // FFN_OUT_SLAB
        in_copy(0).start()
        out_copy(0).start()
        for k in range(n_in):
            if k + 1 < n_in:
                in_copy(k + 1).start()
            in_copy(k).wait()
            wi_ref[k * FFN_IN_SLAB:(k + 1) * FFN_IN_SLAB, :] = stage_i[k % 2].astype(BF16)
        for k in range(n_out):
            if k + 1 < n_out:
                out_copy(k + 1).start()
            out_copy(k).wait()
            wo_ref[k * FFN_OUT_SLAB:(k + 1) * FFN_OUT_SLAB, :] = stage_o[k % 2].astype(BF16)

    def sub_layer(x_ref, o_ref):
        x = x_ref[...]
        h = (x * (1.0 + mod_ref[mi + 1:mi + 2, :]) + mod_ref[mi:mi + 1, :]).astype(BF16)
        gate = jnp.dot(h, wi_ref[:, :D_FF], preferred_element_type=F32)
        up = jnp.dot(h, wi_ref[:, D_FF:], preferred_element_type=F32)
        ff = _dot(_silu(gate) * up, wo_ref[...])
        z = ALPHA * x + 0.5 * mod_ref[mi + 2:mi + 3, :] * ff
        o_ref[...] = _layer_norm(z, lng_ref[...], lnb_ref[...])

    @pl.when(is_ctx)
    def _():
        sub_layer(xc_ref, oc_ref)

    @pl.when(jnp.logical_not(is_ctx))
    def _():
        sub_layer(xl_ref, ol_ref)


def _ffn(x_ctx, x_lat, mod, lat_seq, w_in, w_out, ln_g, ln_b, which, mi, tm):
    n_ctx, n_lat = x_ctx.shape[0] // tm, x_lat.shape[0] // tm
    assert x_ctx.shape[0] % tm == 0 and x_lat.shape[0] % tm == 0 and lat_seq % tm == 0
    assert D_MODEL % FFN_IN_SLAB == 0 and D_FF % FFN_OUT_SLAB == 0
    resident = lambda shape: pl.BlockSpec(shape, lambda i: (0,) * len(shape), pipeline_mode=pl.Buffered(1))
    in_hbm = pl.BlockSpec(memory_space=pl.ANY)
    ctx_tile = pl.BlockSpec((tm, D_MODEL), lambda i: (jnp.minimum(i, n_ctx - 1), 0))
    lat_tile = pl.BlockSpec((tm, D_MODEL), lambda i: (jnp.maximum(i - n_ctx, 0), 0))
    mod_row = lambda i: jnp.where(i < n_ctx, 0, 1 + (jnp.maximum(i - n_ctx, 0) * tm) // lat_seq)
    return pl.pallas_call(
        functools.partial(_ffn_kernel, mi=mi, n_ctx=n_ctx, which=which),
        grid=(n_ctx + n_lat,),
        in_specs=[ctx_tile, lat_tile,
                  pl.BlockSpec((None, N_MOD, D_MODEL), lambda i: (mod_row(i), 0, 0)),
                  in_hbm, in_hbm,
                  resident((1, D_MODEL)), resident((1, D_MODEL))],
        out_specs=[ctx_tile, lat_tile],
        out_shape=[jax.ShapeDtypeStruct(x_ctx.shape, F32), jax.ShapeDtypeStruct(x_lat.shape, F32)],
        scratch_shapes=[pltpu.VMEM((D_MODEL, 2 * D_FF), BF16), pltpu.VMEM((D_FF, D_MODEL), BF16),
                        pltpu.VMEM((2, FFN_IN_SLAB, 2 * D_FF), F32), pltpu.VMEM((2, FFN_OUT_SLAB, D_MODEL), F32),
                        pltpu.SemaphoreType.DMA((2,)), pltpu.SemaphoreType.DMA((2,))],
        compiler_params=_cparams(("arbitrary",)),
        name="ffn",
    )(x_ctx, x_lat, mod, w_in, w_out, ln_g.reshape(1, -1), ln_b.reshape(1, -1))


def _prep_kernel(x_ref, xprev_ref, xnext_ref, mod_ref, w_ref, mu_ref, wup_ref, aup_ref, w0_ref, a0_ref,
                 kk_ref, ka_ref, rk_ref, ones_ref,
                 r_o, v_o, kk_o, lw_o, kd_o, b_o, bonus_o, gs_o, pool_o, gate_o, ext_scr,
                 *, seq_len, latent, tm, mi):
    i = pl.program_id(0)
    t0 = _mod_p2(i * tm, seq_len)
    shift = mod_ref[mi:mi + 1, :]
    scale = mod_ref[mi + 1:mi + 2, :]

    def modulated(ref):
        return (ref[...] * (1.0 + scale) + shift).astype(BF16)

    h = modulated(x_ref)
    h_ext = jnp.concatenate([modulated(xprev_ref), h, modulated(xnext_ref)], axis=0) if latent else None

    def project(c0, c1):
        if latent:
            ext_scr[:, c0:c1] = jnp.dot(h_ext, w_ref[:, c0:c1], preferred_element_type=F32)
            ext_scr[0:HALO, c0:c1] = jnp.where(t0 == 0, 0.0, ext_scr[0:HALO, c0:c1])
            ext_scr[HALO + tm:, c0:c1] = jnp.where(t0 + tm == seq_len, 0.0, ext_scr[HALO + tm:, c0:c1])
        else:
            ext_scr[0:HALO, c0:c1] = jnp.zeros((HALO, c1 - c0), F32)
            ext_scr[HALO:HALO + tm, c0:c1] = jnp.dot(h, w_ref[:, c0:c1], preferred_element_type=F32)
            ext_scr[HALO + tm:, c0:c1] = jnp.zeros((HALO, c1 - c0), F32)

    t = t0 + lax.broadcasted_iota(jnp.int32, (tm, LANES), 0)
    lane = lax.broadcasted_iota(jnp.int32, (1, LANES), 1)
    n_src = 4 if latent else 2
    if latent:
        col = _mod_p2(t, GRID_W)
        has_before = jnp.where(col > 0, 1.0, 0.0)
        has_after = jnp.where(col < GRID_W - 1, 1.0, 0.0)

    def mixed_cols(c):
        sl = slice(c * LANES, (c + 1) * LANES)
        mu = mu_ref[:, sl]
        coef = [jnp.where(_mod_p2(lane, n_src) == s, mu, 0.0) for s in range(n_src)]
        before = ext_scr[HALO - 1:HALO - 1 + tm, sl]
        after = ext_scr[HALO + 1:HALO + 1 + tm, sl]
        if latent:
            return (ext_scr[HALO:HALO + tm, sl] * (1.0 - mu) + (before * has_before) * coef[0]
                    + (after * has_after) * coef[1]
                    + ext_scr[0:tm, sl] * coef[2] + ext_scr[2 * HALO:2 * HALO + tm, sl] * coef[3])
        return ext_scr[HALO:HALO + tm, sl] * (1.0 - mu) + before * coef[0] + after * coef[1]

    def mixed(c0, c1):
        return jnp.concatenate([mixed_cols(c) for c in range(c0, c1)], axis=1)

    nb = D_A // LANES
    project(0, D_A)
    project(D_A, 2 * D_A)
    r = mixed(0, nb)
    project(2 * D_A, 3 * D_A)
    k = mixed(nb, 2 * nb)
    project(3 * D_A, SHIFT_W)
    v = mixed(2 * nb, 3 * nb)
    pool_o[...] = jnp.dot(h, w_ref[:, SHIFT_W:SHIFT_W + D_B], preferred_element_type=F32)
    gate_o[...] = jnp.dot(h, w_ref[:, SHIFT_W + D_B:], preferred_element_type=F32)
    w_down = mixed_cols(3 * nb)
    a_down = mixed_cols(3 * nb + 1)
    g_down = mixed_cols(3 * nb + 2)

    ones_bd = ones_ref[...]
    r_o[...] = r.astype(r_o.dtype)
    v_o[...] = v
    kkraw = k * kk_ref[...]
    ss = _head_sum(kkraw * kkraw, ones_bd)
    kk = kkraw / jnp.maximum(jnp.sqrt(ss), 1e-12)
    kk_o[...] = kk
    bonus_o[...] = (_head_sum(r * k * rk_ref[...], ones_bd) * v).astype(bonus_o.dtype)
    gs_o[...] = _sigmoid(g_down)

    w_raw = _dot(jnp.tanh(w_down), wup_ref[...])
    a_raw = _dot(a_down, aup_ref[...])
    for d in range(2):
        sl = slice(d * D_A, (d + 1) * D_A)
        lw_o[d] = -DECAY_SCALE * _sigmoid(w0_ref[d:d + 1, :] + w_raw[:, sl])
        a = _sigmoid(a0_ref[d:d + 1, :] + a_raw[:, sl])
        kd_o[d] = k * (1.0 + (a - 1.0) * ka_ref[...])
        b_o[d] = kk * a


def _prep(x, mod, mod_of_tile, p, seq_len, latent, tm, mi):
    rows = x.shape[0]
    assert seq_len % tm == 0 and tm % HALO == 0 and (latent or tm == seq_len)
    hb = tm // HALO
    n_halo_blocks = rows // HALO
    n = p['w_mix_in'].shape[1]
    row1 = lambda a: a.reshape(1, -1)
    full = lambda shape: pl.BlockSpec(shape, lambda i: (0,) * len(shape), pipeline_mode=pl.Buffered(1))
    out_cols = lambda width: pl.BlockSpec((tm, width), lambda i: (i, 0))
    out_dir = pl.BlockSpec((2, tm, D_A), lambda i: (0, i, 0))
    sds = jax.ShapeDtypeStruct
    return pl.pallas_call(
        functools.partial(_prep_kernel, seq_len=seq_len, latent=latent, tm=tm, mi=mi),
        grid=(rows // tm,),
        in_specs=[pl.BlockSpec((tm, D_MODEL), lambda i: (i, 0)),
                  pl.BlockSpec((HALO, D_MODEL), lambda i: (jnp.maximum(i * hb - 1, 0), 0)),
                  pl.BlockSpec((HALO, D_MODEL), lambda i: (jnp.minimum((i + 1) * hb, n_halo_blocks - 1), 0)),
                  pl.BlockSpec((None, N_MOD, D_MODEL), lambda i: (mod_of_tile(i, tm), 0, 0)),
                  full((D_MODEL, n)),
                  full((1, SHIFT_W)), full((LANES, 2 * D_A)), full((LANES, 2 * D_A)),
                  full((2, D_A)), full((2, D_A)), full((1, D_A)), full((1, D_A)), full((1, D_A)),
                  full((LANES, LANES))],
        out_specs=[out_cols(D_A), out_cols(D_A), out_cols(D_A), out_dir, out_dir, out_dir, out_cols(D_A),
                   out_cols(LORA_G), out_cols(D_B), out_cols(n - SHIFT_W - D_B)],
        out_shape=[sds((rows, D_A), BF16), sds((rows, D_A), F32), sds((rows, D_A), F32),
                   sds((2, rows, D_A), F32), sds((2, rows, D_A), F32), sds((2, rows, D_A), F32),
                   sds((rows, D_A), BF16), sds((rows, LORA_G), F32),
                   sds((rows, D_B), F32), sds((rows, n - SHIFT_W - D_B), F32)],
        scratch_shapes=[pltpu.VMEM((tm + 2 * HALO, SHIFT_W), F32)],
        compiler_params=_cparams(("parallel",)),
        name="mix_in_prep",
    )(x, x, x, mod, p['w_mix_in'], row1(p['mu_shift']), p['wup_bd'], p['aup_bd'], p['w0'], p['a0'],
      row1(p['k_k']), row1(p['k_a']), row1(p['r_k']), p['ones_bd'])


SCAN_GROUP = 8
SCAN_SEQS = 2
SCAN_CHUNKS = 4


def _scan_kernel(*refs, has_s0, want_state, group, nseq, nsub):
    if has_s0:
        (r_ref, v_ref, kk_ref, lw_ref, kd_ref, b_ref, s0_ref), rest = refs[:7], refs[7:]
    else:
        (r_ref, v_ref, kk_ref, lw_ref, kd_ref, b_ref), rest = refs[:6], refs[6:]
    if want_state:
        y_ref, sout_ref, s_scr = rest
    else:
        y_ref, s_scr = rest
    d = pl.program_id(0)
    c = pl.program_id(3)
    C = CHUNK

    chains = [(bi, p) for bi in range(nseq) for p in range(group)]
    keys = [(bi, slice(p * PAIR, (p + 1) * PAIR)) for bi, p in chains]

    @pl.when(c == 0)
    def _():
        if has_s0:
            z = jnp.zeros((HEAD_DIM, HEAD_DIM), F32)
            for j, (bi, p) in enumerate(chains):
                top = jnp.concatenate([s0_ref[bi, 2 * p], z], axis=1)
                bot = jnp.concatenate([z, s0_ref[bi, 2 * p + 1]], axis=1)
                s_scr[j] = jnp.concatenate([top, bot], axis=0)
        else:
            s_scr[...] = jnp.zeros_like(s_scr)

    row = lax.broadcasted_iota(jnp.int32, (PAIR, PAIR), 0)
    col = lax.broadcasted_iota(jnp.int32, (PAIR, PAIR), 1)
    same_head = _div_p2(row, HEAD_DIM) == _div_p2(col, HEAD_DIM)
    sign = 1 - 2 * d
    order = (_mod_p2(row, C) - _mod_p2(col, C)) * sign
    strict = (order > 0) & same_head
    incl = (order >= 0)[:C, :]
    head0 = lax.broadcasted_iota(jnp.int32, (C, PAIR), 1) < HEAD_DIM

    def stack(z):
        return jnp.concatenate([jnp.where(head0, z, 0.0), jnp.where(head0, 0.0, z)], axis=0)

    def each(fn, *lists):
        return [fn(*args) for args in zip(*lists)]

    trow = lax.broadcasted_iota(jnp.int32, (C, PAIR), 0)
    rev = d.astype(F32)

    def cumsum(lw, tot):
        acc = lw
        s = 1
        while s < C:
            acc = acc + jnp.where(trow >= s, pltpu.roll(acc, s, axis=0), 0.0)
            s *= 2
        return rev * (tot + lw) + (1.0 - 2.0 * rev) * acc

    def prologue(off):
        def rd(ref, key):
            return ref[key[0], pl.ds(off, C), key[1]]

        lw = [rd(lw_ref, k) for k in keys]
        tot = [jnp.sum(z, axis=0, keepdims=True) for z in lw]
        cs = each(cumsum, lw, tot)
        w_inv = [jnp.exp(-z) for z in cs]
        w_rest = each(lambda t, z: jnp.exp(t - z), tot, cs)
        r_t = each(lambda k, z: rd(r_ref, k).astype(F32) * jnp.exp(z), keys, cs)
        kd = [rd(kd_ref, k) for k in keys]
        b = [rd(b_ref, k) for k in keys]
        kk_st = each(lambda k, z, l: stack(rd(kk_ref, k) * jnp.exp(z - l)), keys, cs, lw)
        v_st = [stack(rd(v_ref, k)) for k in keys]
        kb_st = each(lambda kdi, bi, wi: jnp.concatenate([stack(kdi * wi), stack(bi * wi)], axis=0),
                     kd, b, w_inv)
        kdec_st = each(lambda kdi, wr: stack(kdi * wr), kd, w_rest)
        bdec_st = each(lambda bi, wr: stack(bi * wr), b, w_rest)
        return tot, r_t, kk_st, v_st, kb_st, kdec_st, bdec_st

    def precompute(pro):
        tot, r_t, kk_st, v_st, kb_st, kdec_st, bdec_st = pro
        m = each(lambda a, rt, kb: _dot(jnp.concatenate([a, rt], axis=0), kb, _NT), kk_st, r_t, kb_st)
        x = [jnp.where(strict, -z[:2 * C, 2 * C:], 0.0) for z in m]
        p_b = [jnp.where(incl, z[2 * C:, 2 * C:], 0.0) for z in m]
        apk_mask = jnp.concatenate([strict, incl], axis=0)
        apk_v = each(lambda z, vs: _dot(jnp.where(apk_mask, z[:, :2 * C], 0.0), vs), m, v_st)

        ti, tj = _mod_p2(row, C), _mod_p2(col, C)
        eye = jnp.where(row == col, 1.0, 0.0)

        def off_block(n):
            return (_div_p2(ti, 2 * n) == _div_p2(tj, 2 * n)) & (_div_p2(ti, n) != _div_p2(tj, n))

        t_inv = [eye + jnp.where(off_block(1), z, 0.0) for z in x]
        n = 2
        while n < C:
            keep = off_block(n)
            mid = each(lambda z, t: _dot(jnp.where(keep, z, 0.0), t), x, t_inv)
            t_inv = each(lambda t, mm: t + _dot(t, mm), t_inv, mid)
            n *= 2
        pbt = each(lambda pb, t: _dot(pb, t), p_b, t_inv)
        qq = each(lambda t, pt, a, av: _dot(jnp.concatenate([t, pt], axis=0),
                                            jnp.concatenate([a, av[:2 * C]], axis=1)),
                  t_inv, pbt, kk_st, apk_v)
        q = [z[:2 * C] for z in qq]
        ry = each(lambda rt, av, z: jnp.concatenate([rt, av[2 * C:]], axis=1) - z[2 * C:], r_t, apk_v, qq)
        gq = each(lambda qi, bd: _dot(qi, bd, _TN), q, bdec_st)
        s_add = each(lambda vs, kd, g: _dot(vs, kd, _TN) - g[PAIR:], v_st, kdec_st, gq)

        return ry, tot, gq, s_add

    offs = [pl.multiple_of(C * (n + d * (nsub - 1 - 2 * n)), C) for n in range(nsub)]
    pre = [precompute(pro) for pro in [prologue(off) for off in offs]]
    s_new = [s_scr[j] for j in range(len(chains))]
    for off, (ry, tot, gq, s_add) in zip(offs, pre):
        s_old = s_new
        for j, (bi, sl) in enumerate(keys):
            y_ref[bi, pl.ds(off, C), sl] = _dot(ry[j][:, :PAIR], s_old[j], _NT) + ry[j][:, PAIR:]
        s_new = each(lambda s, t, g, sa: s * jnp.exp(t) - _dot(s, g[:PAIR]) + sa, s_old, tot, gq, s_add)
    for j in range(len(chains)):
        s_scr[j] = s_new[j]

    if want_state:
        @pl.when(c == pl.num_programs(3) - 1)
        def _():
            for j, (bi, p) in enumerate(chains):
                sout_ref[bi, 2 * p] = s_new[j][:HEAD_DIM, :HEAD_DIM]
                sout_ref[bi, 2 * p + 1] = s_new[j][HEAD_DIM:, HEAD_DIM:]


def _scan(r, v, kk, lw, kd, b, s0, batch, seq_len, want_state):
    rows = r.shape[0]
    nsub = SCAN_CHUNKS
    nc = seq_len // (CHUNK * nsub)
    group = SCAN_GROUP
    nseq = min(SCAN_SEQS, batch)
    n_groups = D_A // (PAIR * group)
    gw = group * PAIR

    def chunk_idx(d, c):
        return c + d * (nc - 1 - 2 * c)

    shared = pl.BlockSpec((nseq, nsub * CHUNK, gw), lambda d, bi, p, c: (bi, chunk_idx(d, c), p))
    per_dir = pl.BlockSpec((None, nseq, nsub * CHUNK, gw), lambda d, bi, p, c: (d, bi, chunk_idx(d, c), p))
    state = pl.BlockSpec((nseq, None, 2 * group, HEAD_DIM, HEAD_DIM), lambda d, bi, p, c: (bi, d, p, 0, 0))
    in_specs = [shared, shared, shared, per_dir, per_dir, per_dir]
    seqs = lambda z: z.reshape(z.shape[:-2] + (batch, seq_len, D_A))
    args = [seqs(z) for z in (r, v, kk, lw, kd, b)]
    if s0 is not None:
        in_specs.append(state)
        args.append(s0)
    out_specs = [per_dir]
    out_shape = [jax.ShapeDtypeStruct((2, batch, seq_len, D_A), F32)]
    if want_state:
        out_specs.append(state)
        out_shape.append(jax.ShapeDtypeStruct((batch, 2, N_HEADS, HEAD_DIM, HEAD_DIM), F32))
    outs = pl.pallas_call(
        functools.partial(_scan_kernel, has_s0=s0 is not None, want_state=want_state, group=group, nseq=nseq,
                          nsub=nsub),
        grid=(2, batch // nseq, n_groups, nc),
        in_specs=in_specs,
        out_specs=out_specs,
        out_shape=out_shape,
        scratch_shapes=[pltpu.VMEM((nseq * group, PAIR, PAIR), F32)],
        compiler_params=_cparams(("parallel", "parallel", "parallel", "arbitrary")),
        name="rwkv7_scan",
    )(*args)
    return [outs[0].reshape(2, rows, D_A)] + list(outs[1:])


def _post_kernel(y_ref, bonus_ref, gs_ref, gup_ref, pool_ref, pprev_ref, pnext_ref, gates_ref, x_ref, mod_ref,
                 lnxg_ref, lnxb_ref, worwkv_ref, wpool_ref, pscale_ref, wopool_ref, wout_ref,
                 lng_ref, lnb_ref, ones_ref, band_ref, o_ref, ext_scr, *, seq_len, tm):
    i = pl.program_id(0)
    ones_bd = ones_ref[...]
    y = y_ref[0] + y_ref[1]
    mu = _head_sum(y, ones_bd) * (1.0 / HEAD_DIM)

    t0 = _mod_p2(i * tm, seq_len)
    lo_edge, hi_edge = POOL_PAD - POOL_HALO, POOL_PAD + tm + POOL_HALO
    ext_scr[0:lo_edge, :] = jnp.zeros((lo_edge, D_B), F32)
    ext_scr[lo_edge:POOL_PAD, :] = jnp.where(t0 == 0, 0.0, pprev_ref[...])
    ext_scr[POOL_PAD:POOL_PAD + tm, :] = pool_ref[...]
    ext_scr[POOL_PAD + tm:hi_edge, :] = jnp.where(t0 + tm == seq_len, 0.0, pnext_ref[...])
    ext_scr[hi_edge:, :] = jnp.zeros((lo_edge, D_B), F32)
    ext = ext_scr[...].astype(BF16)
    groups = [slice(gi * POOL_GROUP_DIM, (gi + 1) * POOL_GROUP_DIM) for gi in range(len(POOL_WINDOWS))]
    win = [jnp.dot(band_ref[gi], ext[:, sl], preferred_element_type=F32) for gi, sl in enumerate(groups)]

    dlt = y - mu
    var = _head_sum(dlt * dlt, ones_bd) * (1.0 / HEAD_DIM)

    t = t0 + lax.broadcasted_iota(jnp.int32, (tm, LANES), 0)
    us = []
    for gi, w in enumerate(POOL_WINDOWS):
        cnt = jnp.minimum(t + (w - w // 2), seq_len) - jnp.maximum(t - w // 2, 0)
        resid = win[gi] / cnt.astype(F32) - pool_ref[:, groups[gi]]
        us.append(_dot(resid, wpool_ref[gi]))

    yn = dlt * lax.rsqrt(var + GN_EPS) * lnxg_ref[...] + lnxb_ref[...]
    ya = _dot((yn + bonus_ref[...].astype(F32)) * _dot(gs_ref[...], gup_ref[...]), worwkv_ref[...])
    u = jnp.concatenate(us, axis=1) * pscale_ref[...]
    yb = _dot(u, wopool_ref[...])

    gates = gates_ref[...]
    merged = _sigmoid(gates[:, :D_MODEL]) * ya + _sigmoid(gates[:, D_MODEL:]) * yb
    mix = _dot(merged, wout_ref[...])
    z = ALPHA * x_ref[...] + mod_ref[5:6, :] * mix
    o_ref[...] = _layer_norm(z, lng_ref[...], lnb_ref[...])


def _pool_bands(tm):
    i = lax.broadcasted_iota(jnp.int32, (tm, tm + 2 * POOL_PAD), 0)
    j = lax.broadcasted_iota(jnp.int32, (tm, tm + 2 * POOL_PAD), 1) - POOL_PAD
    return jnp.stack([((j >= i - w // 2) & (j < i + w - w // 2)).astype(BF16) for w in POOL_WINDOWS])


def _post(y, bonus, gs, pool_in, gates, x, mod, mod_of_tile, p, seq_len, tm):
    rows = x.shape[0]
    assert seq_len % tm == 0 and tm % POOL_HALO == 0
    hb = tm // POOL_HALO
    n_halo_blocks = rows // POOL_HALO
    row1 = lambda a: a.reshape(1, -1)
    full = lambda shape: pl.BlockSpec(shape, lambda i: (0,) * len(shape))
    tile = lambda width: pl.BlockSpec((tm, width), lambda i: (i, 0))
    return pl.pallas_call(
        functools.partial(_post_kernel, seq_len=seq_len, tm=tm),
        grid=(rows // tm,),
        in_specs=[pl.BlockSpec((2, tm, D_A), lambda i: (0, i, 0)), tile(D_A), tile(LORA_G), full((LORA_G, D_A)),
                  tile(D_B),
                  pl.BlockSpec((POOL_HALO, D_B), lambda i: (jnp.maximum(i * hb - 1, 0), 0)),
                  pl.BlockSpec((POOL_HALO, D_B), lambda i: (jnp.minimum((i + 1) * hb, n_halo_blocks - 1), 0)),
                  tile(2 * D_MODEL), tile(D_MODEL),
                  pl.BlockSpec((None, N_MOD, D_MODEL), lambda i: (mod_of_tile(i, tm), 0, 0)),
                  full((1, D_A)), full((1, D_A)), full((D_A, D_MODEL)),
                  full((len(POOL_WINDOWS), POOL_GROUP_DIM, POOL_GROUP_DIM)), full((1, D_B)),
                  full((D_B, D_MODEL)), full((D_MODEL, D_MODEL)), full((1, D_MODEL)), full((1, D_MODEL)),
                  full((LANES, LANES)), full((len(POOL_WINDOWS), tm, tm + 2 * POOL_PAD))],
        out_specs=tile(D_MODEL),
        out_shape=jax.ShapeDtypeStruct((rows, D_MODEL), F32),
        scratch_shapes=[pltpu.VMEM((tm + 2 * POOL_PAD, D_B), F32)],
        compiler_params=_cparams(("parallel",)),
        name="mixer_out",
    )(y, bonus, gs, p['g_up'], pool_in, pool_in, pool_in, gates, x, mod, row1(p['lnx_g']), row1(p['lnx_b']),
      p['w_o_rwkv'], p['w_pool'], row1(p['pool_scale']), p['w_o_pool'], p['w_out'],
      row1(p['ln_g'][1]), row1(p['ln_b'][1]), p['ones_bd'], _pool_bands(tm))


def _mixer(x1, mod, mod_of_tile, s0, batch, seq_len, latent, want_state, p):
    tm = MIX_TILE
    r, v, kk, lw, kd, b, bonus, gs, pool_in, gates = _prep(x1, mod, mod_of_tile, p, seq_len, latent, tm, 3)
    outs = _scan(r, v, kk, lw, kd, b, s0, batch, seq_len, want_state)
    x2 = _post(outs[0], bonus, gs, pool_in, gates, x1, mod, mod_of_tile, p, seq_len, min(2 * tm, seq_len))
    return x2, (outs[1] if want_state else None)


def _block_diag2(w):
    z = jnp.zeros_like(w[0])
    return jnp.concatenate([jnp.concatenate([w[0], z], axis=1), jnp.concatenate([z, w[1]], axis=1)], axis=0)


def kernel(x_prompt, x_sample, c, state_rwkv, c_ctx, w_mod, b_mod, ln_g, ln_b, ffn_in, ffn_out, w_mix_in,
           mu_shift, w0, w_up, a0, a_up, g_up, k_k, k_a, r_k, lnx_g, lnx_b, w_o_rwkv, w_pool, pool_scale,
           w_o_pool, w_out):
    batch, seq, _ = x_prompt.shape
    dec_batch, dec_seq, _ = x_sample.shape
    y_p = x_prompt.reshape(batch * seq, D_MODEL)
    y_s = x_sample.reshape(dec_batch * dec_seq, D_MODEL)
    cvecs = jnp.concatenate([c_ctx[None, :], c, jnp.zeros((SUBLANES - 1 - dec_batch, D_MODEL), F32)], axis=0)
    ii = lax.broadcasted_iota(jnp.int32, (LANES, LANES), 0) // HEAD_DIM
    jj = lax.broadcasted_iota(jnp.int32, (LANES, LANES), 1) // HEAD_DIM
    ones_bd = (ii == jj).astype(BF16)
    ctx_states = []
    for l in range(DEPTH):
        p = {
            'ln_g': ln_g[l], 'ln_b': ln_b[l],
            'ffn_in': ffn_in[l], 'ffn_out': ffn_out[l],
            'w_mix_in': w_mix_in[l].astype(BF16),
            'mu_shift': mu_shift[l], 'w0': w0[l], 'a0': a0[l],
            'wup_bd': _block_diag2(w_up[l]).astype(BF16), 'aup_bd': _block_diag2(a_up[l]).astype(BF16),
            'g_up': g_up[l].astype(BF16), 'k_k': k_k[l], 'k_a': k_a[l], 'r_k': r_k[l],
            'lnx_g': lnx_g[l], 'lnx_b': lnx_b[l], 'w_o_rwkv': w_o_rwkv[l].astype(BF16),
            'w_pool': w_pool[l].astype(BF16), 'pool_scale': pool_scale[l],
            'w_o_pool': w_o_pool[l].astype(BF16), 'w_out': w_out[l].astype(BF16), 'ones_bd': ones_bd,
        }
        mod = _modulation(cvecs, w_mod[l], b_mod[l]).reshape(SUBLANES, N_MOD, D_MODEL)
        y_p, y_s = _ffn(y_p, y_s, mod, dec_seq, p['ffn_in'], p['ffn_out'], p['ln_g'][0], p['ln_b'][0], 0, 0, FFN_TILE)
        y_p, s_ctx = _mixer(y_p, mod, lambda i, tm: 0, None, batch, seq, False, True, p)
        ctx_states.append(s_ctx)
        y_s, _ = _mixer(y_s, mod, lambda i, tm: 1 + (i * tm) // dec_seq, state_rwkv[:, l],
                        dec_batch, dec_seq, True, False, p)
        y_p, y_s = _ffn(y_p, y_s, mod, dec_seq, p['ffn_in'], p['ffn_out'], p['ln_g'][2], p['ln_b'][2], 1, 6, FFN_TILE)
    new_state = jnp.stack(ctx_states, axis=1).astype(x_prompt.dtype)
    return (y_p.reshape(batch, seq, D_MODEL), y_s.reshape(dec_batch, dec_seq, D_MODEL), new_state)
```
